```python
import math
import jax, jax.numpy as jnp
from jax import lax
import numpy as np

D_MODEL = 1024
BATCH = 8
SEQ = 2048
DEPTH = 2
DEC_BATCH = 128
DEC_SEQ = 4
PAST_LEN = 8192
PAGE_SIZE = 128

N_EVEN = (DEPTH + 1) // 2
N_ODD = DEPTH // 2
HEAD_DIM = 64
N_Q_HEADS = 8
N_KV_HEADS = 2
Q_PER_KV = N_Q_HEADS // N_KV_HEADS
WINDOW = 128
ATTN_WIDTH = N_Q_HEADS * HEAD_DIM
KV_WIDTH = N_KV_HEADS * HEAD_DIM
SSM_WIDTH = D_MODEL // 2
SSM_GROUP = 16
SSM_GROUPS = SSM_WIDTH // SSM_GROUP
SSM_STATE = 64
IN_AB_WIDTH = ATTN_WIDTH + 2 * KV_WIDTH + SSM_WIDTH
MIX_WIDTH = ATTN_WIDTH + SSM_WIDTH
CONV_WIDTH = 31
D_CONV = D_MODEL
D_FF = -(-8 * D_MODEL // (3 * 256)) * 256
RMS_EPS = 1e-6
LN_EPS = 1e-5

kernel_name = 'swa_s5_conformer_hybrid_step'

F32 = jnp.float32


def rms_norm(x, g):
    xf = x.astype(F32)
    y = xf * lax.rsqrt(jnp.mean(xf * xf, axis=-1, keepdims=True) + RMS_EPS) * g.astype(F32)
    return y.astype(x.dtype)


def sink_attention(q, k, v, mask, sinks):
    s = jnp.einsum('...tgrd,...sgd->...grts', q, k, preferred_element_type=F32) * (HEAD_DIM ** -0.5)
    s = jnp.where(mask, s, -jnp.inf)
    sink = jnp.broadcast_to(sinks.astype(F32).reshape(N_KV_HEADS, Q_PER_KV, 1, 1), s.shape[:-1] + (1,))
    p = jax.nn.softmax(jnp.concatenate([s, sink], axis=-1), axis=-1)[..., :-1]
    return jnp.einsum('...grts,...sgd->...tgrd', p.astype(v.dtype), v)


def swa_prompt(q, k, v, sinks):
    n, L = q.shape[:2]
    nb = L // WINDOW
    qb = q.reshape(n, nb, WINDOW, N_KV_HEADS, Q_PER_KV, HEAD_DIM)

    def band(t):
        cur = t.reshape(n, nb, WINDOW, N_KV_HEADS, HEAD_DIM)
        prev = jnp.concatenate([jnp.zeros_like(cur[:, :1]), cur[:, :-1]], axis=1)
        return jnp.concatenate([prev, cur], axis=2)

    i = jnp.arange(WINDOW)[:, None]
    j = jnp.arange(2 * WINDOW)[None, :]
    rel = i + WINDOW - j
    blk = jnp.arange(nb)[:, None, None]
    mask = (rel >= 0) & (rel < WINDOW) & (blk * WINDOW - WINDOW + j >= 0)
    o = sink_attention(qb, band(k), band(v), mask[None, :, None, None], sinks)
    return o.reshape(n, L, ATTN_WIDTH)


def swa_sample(q, k_new, v_new, k_buf, v_buf, sinks):
    n, T = q.shape[:2]
    wb = k_buf.shape[1]
    k_all = jnp.concatenate([k_buf.astype(k_new.dtype), k_new], axis=1)
    v_all = jnp.concatenate([v_buf.astype(v_new.dtype), v_new], axis=1)
    kpos = jnp.concatenate([jnp.arange(wb) - wb, jnp.arange(T)])
    rel = jnp.arange(T)[:, None] - kpos[None, :]
    mask = (rel >= 0) & (rel < WINDOW)
    o = sink_attention(q, k_all, v_all, mask, sinks)
    return o.reshape(n, T, ATTN_WIDTH), k_all[:, -wb:], v_all[:, -wb:]


def s5_scan(u, h0_re, h0_im, lam_re, lam_im, log_step, b_re, b_im, c_re, c_im, d_skip):
    n, L, _ = u.shape
    ug = u.astype(F32).reshape(n, L, SSM_GROUPS, SSM_GROUP)
    lam_re = lam_re.astype(F32)
    lam_im = lam_im.astype(F32)
    dt = jnp.exp(log_step.astype(F32))[:, None]
    mag = jnp.exp(lam_re * dt)
    lb_re = mag * jnp.cos(lam_im * dt)
    lb_im = mag * jnp.sin(lam_im * dt)
    den = lam_re * lam_re + lam_im * lam_im
    coef_re = ((lb_re - 1.0) * lam_re + lb_im * lam_im) / den
    coef_im = (lb_im * lam_re - (lb_re - 1.0) * lam_im) / den
    b_re = b_re.astype(F32)
    b_im = b_im.astype(F32)
    bb_re = coef_re[..., None] * b_re - coef_im[..., None] * b_im
    bb_im = coef_re[..., None] * b_im + coef_im[..., None] * b_re
    bu_re = jnp.einsum('nlgc,gpc->nlgp', ug, bb_re)
    bu_im = jnp.einsum('nlgc,gpc->nlgp', ug, bb_im)
    a_re = jnp.broadcast_to(lb_re, bu_re.shape)
    a_im = jnp.broadcast_to(lb_im, bu_im.shape)

    def combine(e1, e2):
        a1r, a1i, b1r, b1i = e1
        a2r, a2i, b2r, b2i = e2
        return (a2r * a1r - a2i * a1i,
                a2r * a1i + a2i * a1r,
                a2r * b1r - a2i * b1i + b2r,
                a2r * b1i + a2i * b1r + b2i)

    acc_re, acc_im, s_re, s_im = lax.associative_scan(combine, (a_re, a_im, bu_re, bu_im), axis=1)
    h0_re = h0_re.astype(F32)[:, None]
    h0_im = h0_im.astype(F32)[:, None]
    h_re = s_re + acc_re * h0_re - acc_im * h0_im
    h_im = s_im + acc_re * h0_im + acc_im * h0_re
    y = (jnp.einsum('gcp,nlgp->nlgc', c_re.astype(F32), h_re)
         - jnp.einsum('gcp,nlgp->nlgc', c_im.astype(F32), h_im))
    y = y + d_skip.astype(F32) * ug
    return y.reshape(n, L, SSM_WIDTH), h_re[:, -1], h_im[:, -1]


def ab_mixer(h, k_buf, v_buf, h0_re, h0_im, w_in, w_out, sinks, ssm, w_glu):
    n, L, _ = h.shape
    z = h @ w_in
    q, k, v, u = jnp.split(z, [ATTN_WIDTH, ATTN_WIDTH + KV_WIDTH, ATTN_WIDTH + 2 * KV_WIDTH], axis=-1)
    q = q.reshape(n, L, N_KV_HEADS, Q_PER_KV, HEAD_DIM)
    k = k.reshape(n, L, N_KV_HEADS, HEAD_DIM)
    v = v.reshape(n, L, N_KV_HEADS, HEAD_DIM)
    if k_buf is None:
        a = swa_prompt(q, k, v, sinks)
        wb = min(WINDOW, L)
        k_keep, v_keep = k[:, L - wb:], v[:, L - wb:]
    else:
        a, k_keep, v_keep = swa_sample(q, k, v, k_buf, v_buf, sinks)
    y, h_re, h_im = s5_scan(u, h0_re, h0_im, *ssm)
    g = jax.nn.gelu(y)
    s = g * jax.nn.sigmoid(g @ w_glu.astype(F32))
    out = jnp.concatenate([a, s.astype(h.dtype)], axis=-1) @ w_out
    return out, k_keep, v_keep, h_re, h_im


def conv_module(h, prefix, w_pw1, b_pw1, w_dw, b_dw, ln_g, ln_b, w_pw2, b_pw2):
    a = h @ w_pw1 + b_pw1
    a1, a2 = jnp.split(a, 2, axis=-1)
    g = a1 * jax.nn.sigmoid(a2)
    gp = jnp.concatenate([prefix.astype(g.dtype), g], axis=1)
    c = lax.conv_general_dilated(gp, w_dw[:, None, :].astype(g.dtype), (1,), 'VALID',
                                 dimension_numbers=('NWC', 'WIO', 'NWC'),
                                 feature_group_count=D_CONV) + b_dw
    cf = c.astype(F32)
    mu = jnp.mean(cf, axis=-1, keepdims=True)
    var = jnp.mean(jnp.square(cf - mu), axis=-1, keepdims=True)
    cn = (cf - mu) * lax.rsqrt(var + LN_EPS) * ln_g.astype(F32) + ln_b.astype(F32)
    out = jax.nn.silu(cn).astype(h.dtype) @ w_pw2 + b_pw2
    return out, gp[:, -(CONV_WIDTH - 1):]


def swiglu(h, wg, wu, wd):
    return (jax.nn.silu(h @ wg) * (h @ wu)) @ wd


def setup_inputs(seed: int = 0) -> dict:
    key = jax.random.key(seed)
    ks = iter(jax.random.split(key, 64))

    def nrm(shape, scale):
        return jax.random.normal(next(ks), shape, F32) * scale

    def gain(shape):
        return 1.0 + nrm(shape, 0.05)

    win_buf = min(WINDOW, PAST_LEN)
    lam_im0 = math.pi * jnp.arange(SSM_STATE, dtype=F32)
    return {
        'x_prompt': nrm((BATCH, SEQ, D_MODEL), 1.0),
        'x_sample': nrm((DEC_BATCH, DEC_SEQ, D_MODEL), 1.0),
        'cache_k': nrm((N_EVEN, DEC_BATCH, win_buf, N_KV_HEADS, HEAD_DIM), 1.0),
        'cache_v': nrm((N_EVEN, DEC_BATCH, win_buf, N_KV_HEADS, HEAD_DIM), 1.0),
        'state_s5_re': nrm((N_EVEN, DEC_BATCH, SSM_GROUPS, SSM_STATE), 1.0),
        'state_s5_im': nrm((N_EVEN, DEC_BATCH, SSM_GROUPS, SSM_STATE), 1.0),
        'state_conv': nrm((N_ODD, DEC_BATCH, CONV_WIDTH - 1, D_CONV), 0.5),
        'norm_pre_mix': gain((DEPTH, D_MODEL)),
        'norm_post_mix': gain((DEPTH, D_MODEL)),
        'norm_pre_ffn': gain((DEPTH, D_MODEL)),
        'norm_post_ffn': gain((DEPTH, D_MODEL)),
        'w_in_ab': nrm((N_EVEN, D_MODEL, IN_AB_WIDTH), D_MODEL ** -0.5),
        'attn_sinks': nrm((N_EVEN, N_Q_HEADS), 0.5),
        's5_lambda_re': -0.5 + nrm((N_EVEN, SSM_GROUPS, SSM_STATE), 0.01),
        's5_lambda_im': lam_im0 + nrm((N_EVEN, SSM_GROUPS, SSM_STATE), 0.01),
        's5_log_step': jax.random.uniform(next(ks), (N_EVEN, SSM_GROUPS), F32, math.log(1e-3), math.log(1e-1)),
        's5_b_re': nrm((N_EVEN, SSM_GROUPS, SSM_STATE, SSM_GROUP), (2 * SSM_GROUP) ** -0.5),
        's5_b_im': nrm((N_EVEN, SSM_GROUPS, SSM_STATE, SSM_GROUP), (2 * SSM_GROUP) ** -0.5),
        's5_c_re': nrm((N_EVEN, SSM_GROUPS, SSM_GROUP, SSM_STATE), SSM_STATE ** -0.5),
        's5_c_im': nrm((N_EVEN, SSM_GROUPS, SSM_GROUP, SSM_STATE), SSM_STATE ** -0.5),
        's5_d': nrm((N_EVEN, SSM_GROUPS, SSM_GROUP), 1.0),
        'w_glu': nrm((N_EVEN, SSM_WIDTH, SSM_WIDTH), SSM_WIDTH ** -0.5),
        'w_out_ab': nrm((N_EVEN, MIX_WIDTH, D_MODEL), MIX_WIDTH ** -0.5),
        'w_pw1': nrm((N_ODD, D_MODEL, 2 * D_CONV), D_MODEL ** -0.5),
        'b_pw1': nrm((N_ODD, 2 * D_CONV), 0.02),
        'w_dw': nrm((N_ODD, CONV_WIDTH, D_CONV), CONV_WIDTH ** -0.5),
        'b_dw': nrm((N_ODD, D_CONV), 0.02),
        'conv_ln_g': gain((N_ODD, D_CONV)),
        'conv_ln_b': nrm((N_ODD, D_CONV), 0.02),
        'w_pw2': nrm((N_ODD, D_CONV, D_MODEL), D_CONV ** -0.5),
        'b_pw2': nrm((N_ODD, D_MODEL), 0.02),
        'w_ffn_gate': nrm((DEPTH, D_MODEL, D_FF), D_MODEL ** -0.5),
        'w_ffn_up': nrm((DEPTH, D_MODEL, D_FF), D_MODEL ** -0.5),
        'w_ffn_down': nrm((DEPTH, D_FF, D_MODEL), D_FF ** -0.5),
    }


def reference(x_prompt, x_sample, cache_k, cache_v, state_s5_re, state_s5_im, state_conv,
              norm_pre_mix, norm_post_mix, norm_pre_ffn, norm_post_ffn,
              w_in_ab, attn_sinks, s5_lambda_re, s5_lambda_im, s5_log_step,
              s5_b_re, s5_b_im, s5_c_re, s5_c_im, s5_d, w_glu, w_out_ab,
              w_pw1, b_pw1, w_dw, b_dw, conv_ln_g, conv_ln_b, w_pw2, b_pw2,
              w_ffn_gate, w_ffn_up, w_ffn_down):
    xp, xs = x_prompt, x_sample
    nP, nS = xp.shape[0], xs.shape[0]
    kp_l, vp_l, srp_l, sip_l, cp_l = [], [], [], [], []
    ks_l, vs_l, srs_l, sis_l, cs_l = [], [], [], [], []
    for layer in range(DEPTH):
        hp = rms_norm(xp, norm_pre_mix[layer])
        hs = rms_norm(xs, norm_pre_mix[layer])
        if layer % 2 == 0:
            e = layer // 2
            ssm = (s5_lambda_re[e], s5_lambda_im[e], s5_log_step[e], s5_b_re[e], s5_b_im[e],
                   s5_c_re[e], s5_c_im[e], s5_d[e])
            zeros_h = jnp.zeros((nP, SSM_GROUPS, SSM_STATE), F32)
            mp, kp, vp, srp, sip = ab_mixer(hp, None, None, zeros_h, zeros_h, w_in_ab[e], w_out_ab[e],
                                            attn_sinks[e], ssm, w_glu[e])
            ms, kS, vS, srs, sis = ab_mixer(hs, cache_k[e], cache_v[e], state_s5_re[e], state_s5_im[e],
                                            w_in_ab[e], w_out_ab[e], attn_sinks[e], ssm, w_glu[e])
            kp_l.append(kp); vp_l.append(vp); srp_l.append(srp); sip_l.append(sip)
            ks_l.append(kS); vs_l.append(vS); srs_l.append(srs); sis_l.append(sis)
        else:
            o = layer // 2
            cparams = (w_pw1[o], b_pw1[o], w_dw[o], b_dw[o], conv_ln_g[o], conv_ln_b[o], w_pw2[o], b_pw2[o])
            prefix0 = jnp.zeros((nP, CONV_WIDTH - 1, D_CONV), hp.dtype)
            mp, cp = conv_module(hp, prefix0, *cparams)
            ms, cS = conv_module(hs, state_conv[o], *cparams)
            cp_l.append(cp); cs_l.append(cS)
        xp = xp + rms_norm(mp, norm_post_mix[layer])
        xs = xs + rms_norm(ms, norm_post_mix[layer])
        fp = swiglu(rms_norm(xp, norm_pre_ffn[layer]), w_ffn_gate[layer], w_ffn_up[layer], w_ffn_down[layer])
        fs = swiglu(rms_norm(xs, norm_pre_ffn[layer]), w_ffn_gate[layer], w_ffn_up[layer], w_ffn_down[layer])
        xp = xp + rms_norm(fp, norm_post_ffn[layer])
        xs = xs + rms_norm(fs, norm_post_ffn[layer])
    k_prompt = jnp.stack(kp_l)
    v_prompt = jnp.stack(vp_l)
    s5_re_prompt = jnp.stack(srp_l)
    s5_im_prompt = jnp.stack(sip_l)
    conv_prompt = jnp.stack(cp_l)
    k_sample = jnp.stack(ks_l)
    v_sample = jnp.stack(vs_l)
    s5_re_sample = jnp.stack(srs_l)
    s5_im_sample = jnp.stack(sis_l)
    conv_sample = jnp.stack(cs_l)
    return (xp, xs, k_prompt, v_prompt, s5_re_prompt, s5_im_prompt, conv_prompt,
            k_sample, v_sample, s5_re_sample, s5_im_sample, conv_sample)
```

```python
import functools

import jax
import jax.numpy as jnp
from jax import lax
from jax.experimental import pallas as pl
from jax.experimental.pallas import tpu as pltpu

F32 = jnp.float32
BF16 = jnp.bfloat16

LANE = 128
SUBLANE = 8
VMEM_LIMIT_BYTES = 56 * 1024 * 1024

D_MODEL = 1024
HEAD_DIM = 64
N_Q_HEADS = 8
N_KV_HEADS = 2
Q_PER_KV = N_Q_HEADS // N_KV_HEADS
WINDOW = 128
ATTN_WIDTH = N_Q_HEADS * HEAD_DIM
KV_WIDTH = N_KV_HEADS * HEAD_DIM
SSM_WIDTH = 512
SSM_GROUP = 16
SSM_GROUPS = 32
SSM_STATE = 64
D_FF = 2816
CONV_WIDTH = 31
CONV_TAIL = CONV_WIDTH - 1
RMS_EPS = 1e-6
LN_EPS = 1e-5

N_STRIP = SSM_WIDTH // LANE
STRIP_STATES = (SSM_GROUPS // N_STRIP) * SSM_STATE

SAMPLE_T = SUBLANE


def _rms(x, g):
    return x * lax.rsqrt(jnp.mean(x * x, axis=-1, keepdims=True) + RMS_EPS) * g


def _dot(a, b):
    return jnp.dot(a, b, preferred_element_type=F32)


def _dot_nt(a, b):
    return lax.dot_general(a, b, (((1,), (1,)), ((), ())), preferred_element_type=F32)


def _const_spec(shape):
    nd = len(shape)
    return pl.BlockSpec(shape, lambda *_: (0,) * nd, pipeline_mode=pl.Buffered(1))


def _params(n_axes):
    return pltpu.CompilerParams(
        dimension_semantics=("arbitrary",) * n_axes, vmem_limit_bytes=VMEM_LIMIT_BYTES)


def _s5_disc_kernel(lre_ref, lim_ref, ls_ref, bre_ref, bim_ref, are_ref, aim_ref, bbre_ref, bbim_ref):
    lam_re = lre_ref[...]
    lam_im = lim_ref[...]
    dt = jnp.exp(ls_ref[...])
    mag = jnp.exp(lam_re * dt)
    lb_re = mag * jnp.cos(lam_im * dt)
    lb_im = mag * jnp.sin(lam_im * dt)
    den = lam_re * lam_re + lam_im * lam_im
    coef_re = ((lb_re - 1.0) * lam_re + lb_im * lam_im) / den
    coef_im = (lb_im * lam_re - (lb_re - 1.0) * lam_im) / den
    b_re = bre_ref[...]
    b_im = bim_ref[...]
    are_ref[...] = lb_re
    aim_ref[...] = lb_im
    bbre_ref[...] = coef_re * b_re - coef_im * b_im
    bbim_ref[...] = coef_re * b_im + coef_im * b_re


def _s5_tables(lam_re, lam_im, log_step, b_re, b_im, c_re, c_im):
    rep = lambda v: jnp.repeat(v, SSM_GROUP, axis=0)
    rows = SSM_GROUPS * SSM_GROUP
    bt = lambda b: jnp.swapaxes(b, 1, 2).reshape(rows, SSM_STATE)
    ls = jnp.broadcast_to(log_step[:, None], (SSM_GROUPS, SSM_STATE))
    shp = jax.ShapeDtypeStruct((rows, SSM_STATE), F32)
    a_re, a_im, bb_re, bb_im = pl.pallas_call(
        _s5_disc_kernel, out_shape=(shp, shp, shp, shp), name="s5_disc",
    )(rep(lam_re), rep(lam_im), rep(ls), bt(b_re), bt(b_im))
    a_re = a_re[::SSM_GROUP].reshape(N_STRIP, STRIP_STATES)
    a_im = a_im[::SSM_GROUP].reshape(N_STRIP, STRIP_STATES)
    gl = SSM_GROUPS // N_STRIP
    eye = jnp.eye(gl, dtype=F32)

    def in_blockdiag(m):
        m = m.reshape(N_STRIP, gl, SSM_GROUP, SSM_STATE)
        return jnp.einsum("jgcp,gh->jgchp", m, eye).reshape(N_STRIP, LANE, STRIP_STATES)

    def out_blockdiag(m):
        m = m.reshape(N_STRIP, gl, SSM_GROUP, SSM_STATE)
        return jnp.einsum("jgcp,gh->jgphc", m, eye).reshape(N_STRIP, STRIP_STATES, LANE)

    bb = jnp.concatenate([in_blockdiag(bb_re), in_blockdiag(bb_im)], axis=2).astype(BF16)
    cc = jnp.concatenate([out_blockdiag(c_re), -out_blockdiag(c_im)], axis=1).astype(BF16)
    return a_re, a_im, bb, cc


def _gelu_glu(y, wglu_ref):
    g = jax.nn.gelu(y)
    return g * jax.nn.sigmoid(_dot(g.astype(BF16), wglu_ref[...]))


def _band_attention(q, kb, vb, sinks, bias):
    left = lax.broadcasted_iota(jnp.int32, kb.shape, 1) < HEAD_DIM
    kr = pltpu.roll(kb, HEAD_DIM, axis=1)
    vr = pltpu.roll(vb, HEAD_DIM, axis=1)
    gw = Q_PER_KV * HEAD_DIM
    head_of_lane = lax.shift_right_logical(
        lax.broadcasted_iota(jnp.int32, (WINDOW, gw), 1), HEAD_DIM.bit_length() - 1)
    outs = []
    for g in range(N_KV_HEADS):
        kk = jnp.where(left, kb, kr) if g == 0 else jnp.where(left, kr, kb)
        vv = jnp.where(left, vb, vr) if g == 0 else jnp.where(left, vr, vb)
        k4 = jnp.concatenate([kk, kk], axis=1).astype(BF16)
        v4 = jnp.concatenate([vv, vv], axis=1).astype(BF16)
        qg = q[:, g * gw:(g + 1) * gw]
        qst = jnp.concatenate(
            [jnp.where(head_of_lane == r, qg, 0.0) for r in range(Q_PER_KV)], axis=0).astype(BF16)
        s = _dot_nt(qst, k4) + bias
        sink = jnp.concatenate(
            [jnp.full((WINDOW, 1), sinks[g * Q_PER_KV + r], F32) for r in range(Q_PER_KV)], axis=0)
        m = jnp.maximum(jnp.max(s, axis=1, keepdims=True), sink)
        p = jnp.exp(s - m)
        den = jnp.sum(p, axis=1, keepdims=True) + jnp.exp(sink - m)
        p = (p / den).astype(BF16)
        o4 = _dot(p, v4)
        og = jnp.where(head_of_lane == 0, o4[0:WINDOW], 0.0)
        for r in range(1, Q_PER_KV):
            og = jnp.where(head_of_lane == r, o4[r * WINDOW:(r + 1) * WINDOW], og)
        outs.append(og)
    return jnp.concatenate(outs, axis=1)


def _proj_attn_kernel(sink_ref, x_ref, g_ref, w_ref, a_ref, u_ref, kk_ref, vk_ref, kv_scr, *, tile):
    i = pl.program_id(1)
    h = _rms(x_ref[...], g_ref[...]).astype(BF16)
    z = _dot(h, w_ref[...])
    u_ref[...] = z[:, ATTN_WIDTH + 2 * KV_WIDTH:]

    @pl.when(i == 0)
    def _():
        kv_scr[0:WINDOW, :] = jnp.zeros((WINDOW, 2 * KV_WIDTH), F32)

    kv_scr[WINDOW:WINDOW + tile, :] = z[:, ATTN_WIDTH:ATTN_WIDTH + 2 * KV_WIDTH]
    q = z[:, :ATTN_WIDTH] * (HEAD_DIM ** -0.5)

    shape = (Q_PER_KV * WINDOW, 2 * WINDOW)
    row = jnp.bitwise_and(lax.broadcasted_iota(jnp.int32, shape, 0), WINDOW - 1)
    col = lax.broadcasted_iota(jnp.int32, shape, 1)
    ninf = jnp.float32(-jnp.inf)
    bias = jnp.where(col > row, jnp.where(col <= row + WINDOW, 0.0, ninf), ninf)
    bias0 = jnp.where(jnp.logical_and(col < WINDOW, i == 0), ninf, bias)
    sinks = [sink_ref[hh] for hh in range(N_Q_HEADS)]
    for j in range(tile // WINDOW):
        kb = kv_scr[j * WINDOW:(j + 2) * WINDOW, 0:KV_WIDTH]
        vb = kv_scr[j * WINDOW:(j + 2) * WINDOW, KV_WIDTH:2 * KV_WIDTH]
        o = _band_attention(q[j * WINDOW:(j + 1) * WINDOW], kb, vb, sinks, bias0 if j == 0 else bias)
        a_ref[j * WINDOW:(j + 1) * WINDOW, :] = o.astype(a_ref.dtype)
    kk_ref[...] = kv_scr[tile:tile + WINDOW, 0:KV_WIDTH]
    vk_ref[...] = kv_scr[tile:tile + WINDOW, KV_WIDTH:2 * KV_WIDTH]
    kv_scr[0:WINDOW, :] = kv_scr[tile:tile + WINDOW, :]


def _prompt_proj_attn(x, g_pre, w_in, sinks, *, tile):
    nb, seq, _ = x.shape
    zw = w_in.shape[1]
    grid = (nb, seq // tile)
    return pl.pallas_call(
        functools.partial(_proj_attn_kernel, tile=tile),
        grid=grid,
        in_specs=[
            pl.BlockSpec(memory_space=pltpu.SMEM),
            pl.BlockSpec((None, tile, D_MODEL), lambda b, i: (b, i, 0)),
            _const_spec((1, D_MODEL)),
            _const_spec((D_MODEL, zw)),
        ],
        out_specs=[
            pl.BlockSpec((tile, ATTN_WIDTH), lambda b, i: (i, b)),
            pl.BlockSpec((tile, SSM_WIDTH), lambda b, i: (i, b)),
            pl.BlockSpec((None, WINDOW, KV_WIDTH), lambda b, i: (b, 0, 0)),
            pl.BlockSpec((None, WINDOW, KV_WIDTH), lambda b, i: (b, 0, 0)),
        ],
        out_shape=[
            jax.ShapeDtypeStruct((seq, nb * ATTN_WIDTH), BF16),
            jax.ShapeDtypeStruct((seq, nb * SSM_WIDTH), F32),
            jax.ShapeDtypeStruct((nb, WINDOW, KV_WIDTH), F32),
            jax.ShapeDtypeStruct((nb, WINDOW, KV_WIDTH), F32),
        ],
        scratch_shapes=[pltpu.VMEM((tile + WINDOW, 2 * KV_WIDTH), F32)],
        compiler_params=_params(2),
        name="prompt_proj_attn",
    )(sinks, x, g_pre, w_in)


def _s5_prompt_kernel(u_ref, a_ref, bb_ref, cc_ref, are_ref, aim_ref, d_ref, wglu_ref, wout_ref,
                      m_ref, hfin_ref, hbuf, *, steps, nb):
    i = pl.program_id(0)

    @pl.when(i == 0)
    def _():
        hfin_ref[...] = jnp.zeros(hfin_ref.shape, F32)

    ys = []
    for j in range(N_STRIP):
        uj = u_ref[:, j * LANE:(j + 1) * LANE]
        hbuf[...] = _dot(uj.astype(BF16), bb_ref[j])
        are = jnp.broadcast_to(are_ref[j:j + 1, :], (nb, STRIP_STATES))
        aim = jnp.broadcast_to(aim_ref[j:j + 1, :], (nb, STRIP_STATES))

        def step(t, carry):
            hr, hi = carry
            r = pl.multiple_of(t * nb, nb)
            br = hbuf[pl.ds(r, nb), 0:STRIP_STATES]
            bi = hbuf[pl.ds(r, nb), STRIP_STATES:2 * STRIP_STATES]
            nr = are * hr - aim * hi + br
            ni = are * hi + aim * hr + bi
            hbuf[pl.ds(r, nb), 0:STRIP_STATES] = nr
            hbuf[pl.ds(r, nb), STRIP_STATES:2 * STRIP_STATES] = ni
            return nr, ni

        h0 = (hfin_ref[j, :, 0:STRIP_STATES], hfin_ref[j, :, STRIP_STATES:2 * STRIP_STATES])
        hr, hi = lax.fori_loop(0, steps, step, h0, unroll=8)
        hfin_ref[j, :, 0:STRIP_STATES] = hr
        hfin_ref[j, :, STRIP_STATES:2 * STRIP_STATES] = hi
        ys.append(_dot(hbuf[...].astype(BF16), cc_ref[j]) + d_ref[:, j * LANE:(j + 1) * LANE] * uj)
    s = _gelu_glu(jnp.concatenate(ys, axis=1), wglu_ref)
    m = _dot(a_ref[...], wout_ref[0:ATTN_WIDTH, :]) + _dot(s.astype(BF16), wout_ref[ATTN_WIDTH:, :])
    m_ref[...] = m.astype(m_ref.dtype)


def _prompt_s5(u, a, tabs, d_skip, w_glu, w_out, *, nb, steps):
    a_re, a_im, bb, cc = tabs
    rows = u.shape[0]
    blk = steps * nb
    return pl.pallas_call(
        functools.partial(_s5_prompt_kernel, steps=steps, nb=nb),
        grid=(rows // blk,),
        in_specs=[
            pl.BlockSpec((blk, SSM_WIDTH), lambda i: (i, 0)),
            pl.BlockSpec((blk, ATTN_WIDTH), lambda i: (i, 0)),
            _const_spec(bb.shape), _const_spec(cc.shape),
            _const_spec(a_re.shape), _const_spec(a_im.shape),
            _const_spec((1, SSM_WIDTH)),
            _const_spec(w_glu.shape), _const_spec(w_out.shape),
        ],
        out_specs=[
            pl.BlockSpec((blk, D_MODEL), lambda i: (i, 0)),
            pl.BlockSpec((N_STRIP, nb, 2 * STRIP_STATES), lambda i: (0, 0, 0)),
        ],
        out_shape=[
            jax.ShapeDtypeStruct((rows, D_MODEL), BF16),
            jax.ShapeDtypeStruct((N_STRIP, nb, 2 * STRIP_STATES), F32),
        ],
        scratch_shapes=[pltpu.VMEM((blk, 2 * STRIP_STATES), F32)],
        compiler_params=_params(1),
        name="prompt_s5",
    )(u, a, bb, cc, a_re, a_im, d_skip, w_glu, w_out)


def _post_ffn_kernel(x_ref, m_ref, gpm_ref, gpf_ref, gof_ref, wg_ref, wu_ref, wd_ref, o_ref):
    x1 = x_ref[...] + _rms(m_ref[...].astype(F32), gpm_ref[...])
    h = _rms(x1, gpf_ref[...]).astype(BF16)
    gate = _dot(h, wg_ref[...])
    up = _dot(h, wu_ref[...])
    act = (gate * jax.nn.sigmoid(gate) * up).astype(BF16)
    f = _dot(act, wd_ref[...])
    o_ref[...] = x1 + _rms(f, gof_ref[...])


def _post_ffn(x, m, gains, ffn_w, *, grid, x_spec, m_spec, o_spec, o_shape, name):
    gpm, gpf, gof = gains
    wg, wu, wd = ffn_w
    return pl.pallas_call(
        _post_ffn_kernel,
        grid=grid,
        in_specs=[x_spec, m_spec,
                  _const_spec((1, D_MODEL)), _const_spec((1, D_MODEL)), _const_spec((1, D_MODEL)),
                  _const_spec(wg.shape), _const_spec(wu.shape), _const_spec(wd.shape)],
        out_specs=o_spec,
        out_shape=jax.ShapeDtypeStruct(o_shape, F32),
        compiler_params=_params(len(grid)),
        name=name,
    )(x, m, gpm, gpf, gof, wg, wu, wd)


CONV_HALO = 32
CONV_ROWS = 32


def _layer_norm_silu(c, lng_ref, lnb_ref):
    mu = jnp.mean(c, axis=-1, keepdims=True)
    cc = c - mu
    var = jnp.mean(cc * cc, axis=-1, keepdims=True)
    cn = cc * lax.rsqrt(var + LN_EPS) * lng_ref[...] + lnb_ref[...]
    return cn * jax.nn.sigmoid(cn)


def _conv_prompt_kernel(x_ref, g_ref, w1_ref, b1_ref, wdw_ref, bdw_ref, lng_ref, lnb_ref, w2_ref, b2_ref,
                        m_ref, tail_ref, gp_scr, c_scr, *, blk, nb):
    i = pl.program_id(0)
    halo = CONV_HALO * nb
    tail = CONV_TAIL * nb

    @pl.when(i == 0)
    def _():
        gp_scr[0:halo, :] = jnp.zeros((halo, D_MODEL), F32)

    h = _rms(x_ref[...], g_ref[...]).astype(BF16)
    a = _dot(h, w1_ref[...]) + b1_ref[...]
    gp_scr[halo:halo + blk, :] = a[:, :D_MODEL] * jax.nn.sigmoid(a[:, D_MODEL:])

    def chunk(ci, carry):
        base = pl.multiple_of(ci * CONV_ROWS, CONV_ROWS)
        acc = jnp.broadcast_to(bdw_ref[...], (CONV_ROWS, D_MODEL))
        for j in range(CONV_WIDTH):
            wj = wdw_ref[j * SUBLANE:(j + 1) * SUBLANE, :]
            wj = jnp.concatenate([wj] * (CONV_ROWS // SUBLANE), axis=0)
            acc = acc + wj * gp_scr[pl.ds(base + halo - tail + j * nb, CONV_ROWS), :]
        c_scr[pl.ds(base, CONV_ROWS), :] = acc
        return carry

    lax.fori_loop(0, blk // CONV_ROWS, chunk, 0)
    act = _layer_norm_silu(c_scr[...], lng_ref, lnb_ref).astype(BF16)
    m_ref[...] = (_dot(act, w2_ref[...]) + b2_ref[...]).astype(m_ref.dtype)
    tail_ref[...] = gp_scr[halo + blk - tail:halo + blk, :]
    gp_scr[0:halo, :] = gp_scr[blk:blk + halo, :]


def _prompt_conv(x, g_pre, cw, *, nb, steps):
    w1, b1, wdw8, bdw, lng, lnb, w2, b2 = cw
    rows = x.shape[0]
    blk = steps * nb
    assert blk >= CONV_HALO * nb and blk % CONV_ROWS == 0
    vec = lambda n: _const_spec((1, n))
    return pl.pallas_call(
        functools.partial(_conv_prompt_kernel, blk=blk, nb=nb),
        grid=(rows // blk,),
        in_specs=[
            pl.BlockSpec((blk, D_MODEL), lambda i: (i, 0)),
            vec(D_MODEL), _const_spec(w1.shape), vec(2 * D_MODEL),
            _const_spec(wdw8.shape), vec(D_MODEL), vec(D_MODEL), vec(D_MODEL),
            _const_spec(w2.shape), vec(D_MODEL),
        ],
        out_specs=[
            pl.BlockSpec((blk, D_MODEL), lambda i: (i, 0)),
            pl.BlockSpec((CONV_TAIL * nb, D_MODEL), lambda i: (0, 0)),
        ],
        out_shape=[
            jax.ShapeDtypeStruct((rows, D_MODEL), BF16),
            jax.ShapeDtypeStruct((CONV_TAIL * nb, D_MODEL), F32),
        ],
        scratch_shapes=[pltpu.VMEM((blk + CONV_HALO * nb, D_MODEL), F32),
                        pltpu.VMEM((blk, D_MODEL), F32)],
        compiler_params=_params(1),
        name="prompt_conv",
    )(x, g_pre, w1, b1, wdw8, bdw, lng, lnb, w2, b2)


def _sample_l0_kernel(sink_ref, x_ref, g_ref, w_ref, ck_ref, cv_ref, sre_ref, sim_ref,
                      bb_ref, cc_ref, are_ref, aim_ref, d_ref, wglu_ref, wout_ref,
                      m_ref, kn_ref, vn_ref, ore_ref, oim_ref,
                      q_scr, k_scr, v_scr, a_scr, u_scr, y_scr, *, nseq, ntok):
    rows = nseq * SAMPLE_T
    qw = N_Q_HEADS * LANE
    h = _rms(x_ref[...], g_ref[...]).astype(BF16)
    z = _dot(h, w_ref[...])
    q_scr[...] = z[:, :qw] * (HEAD_DIM ** -0.5)
    kn = z[:, qw:qw + KV_WIDTH]
    vn = z[:, qw + KV_WIDTH:qw + 2 * KV_WIDTH]
    k_scr[...] = kn
    v_scr[...] = vn
    kn_ref[...] = kn
    vn_ref[...] = vn
    for j in range(N_STRIP):
        u_scr[j] = z[:, qw + 2 * KV_WIDTH + j * LANE:qw + 2 * KV_WIDTH + (j + 1) * LANE]
    y_scr[...] = jnp.zeros(y_scr.shape, F32)

    for j in range(N_STRIP):
        hr = sre_ref[:, j * STRIP_STATES:(j + 1) * STRIP_STATES]
        hi = sim_ref[:, j * STRIP_STATES:(j + 1) * STRIP_STATES]
        are = are_ref[j:j + 1, :]
        aim = aim_ref[j:j + 1, :]
        dj = d_ref[:, j * LANE:(j + 1) * LANE]
        for t in range(ntok):
            ut = u_scr[j, pl.ds(t, nseq, stride=SAMPLE_T), :]
            bu = _dot(ut.astype(BF16), bb_ref[j])
            hr, hi = (are * hr - aim * hi + bu[:, :STRIP_STATES],
                      are * hi + aim * hr + bu[:, STRIP_STATES:])
            hcat = jnp.concatenate([hr, hi], axis=1).astype(BF16)
            y_scr[j, pl.ds(t, nseq, stride=SAMPLE_T), :] = _dot(hcat, cc_ref[j]) + dj * ut
        ore_ref[:, j * STRIP_STATES:(j + 1) * STRIP_STATES] = hr
        oim_ref[:, j * STRIP_STATES:(j + 1) * STRIP_STATES] = hi
    s = _gelu_glu(jnp.concatenate([y_scr[j] for j in range(N_STRIP)], axis=1), wglu_ref)

    nkeys = WINDOW + SAMPLE_T
    shape = (N_Q_HEADS * SAMPLE_T, nkeys)
    tq = jnp.bitwise_and(lax.broadcasted_iota(jnp.int32, shape, 0), SAMPLE_T - 1)
    col = lax.broadcasted_iota(jnp.int32, shape, 1)
    ninf = jnp.float32(-jnp.inf)
    bias = jnp.where(col < WINDOW, jnp.where(col > tq, 0.0, ninf),
                     jnp.where(col - WINDOW <= tq, 0.0, ninf))
    sink = jnp.concatenate(
        [jnp.full((SAMPLE_T, 1), sink_ref[hh], F32) for hh in range(N_Q_HEADS)], axis=0)
    left = lax.broadcasted_iota(jnp.int32, (SAMPLE_T, LANE), 1) < HEAD_DIM

    def seq_body(n, carry):
        r = pl.multiple_of(n * SAMPLE_T, SAMPLE_T)
        qs = jnp.concatenate(
            [q_scr[pl.ds(r, SAMPLE_T), hh * LANE:(hh + 1) * LANE] for hh in range(N_Q_HEADS)], axis=0)
        kall = jnp.concatenate([ck_ref[n], k_scr[pl.ds(r, SAMPLE_T), :]], axis=0)
        vall = jnp.concatenate([cv_ref[n], v_scr[pl.ds(r, SAMPLE_T), :]], axis=0)
        sc = _dot_nt(qs.astype(BF16), kall.astype(BF16)) + bias
        mx = jnp.maximum(jnp.max(sc, axis=1, keepdims=True), sink)
        p = jnp.exp(sc - mx)
        den = jnp.sum(p, axis=1, keepdims=True) + jnp.exp(sink - mx)
        p = (p / den).astype(BF16)
        o1 = _dot(p, vall.astype(BF16))
        o2 = _dot(p, pltpu.roll(vall, HEAD_DIM, axis=1).astype(BF16))
        slabs = []
        for hp in range(N_Q_HEADS // 2):
            lo, ro = (o1, o2) if hp < Q_PER_KV // 2 else (o2, o1)
            lrow = 2 * hp * SAMPLE_T
            slabs.append(jnp.where(left, lo[lrow:lrow + SAMPLE_T], ro[lrow + SAMPLE_T:lrow + 2 * SAMPLE_T]))
        a_scr[pl.ds(r, SAMPLE_T), :] = jnp.concatenate(slabs, axis=1)
        return carry

    lax.fori_loop(0, nseq, seq_body, 0)
    m = (_dot(a_scr[...].astype(BF16), wout_ref[0:ATTN_WIDTH, :])
         + _dot(s.astype(BF16), wout_ref[ATTN_WIDTH:, :]))
    m_ref[...] = m.astype(m_ref.dtype)


def _sample_l0(x, g_pre, w_in_s, sinks, ck, cv, s_re, s_im, tabs, d_skip, w_glu, w_out, *, nseq, ntok):
    a_re, a_im, bb, cc = tabs
    rows_all = x.shape[0]
    rows = nseq * SAMPLE_T
    nstate = SSM_GROUPS * SSM_STATE
    row_spec = lambda w: pl.BlockSpec((rows, w), lambda i: (i, 0))
    return pl.pallas_call(
        functools.partial(_sample_l0_kernel, nseq=nseq, ntok=ntok),
        grid=(rows_all // rows,),
        in_specs=[
            pl.BlockSpec(memory_space=pltpu.SMEM),
            row_spec(D_MODEL), _const_spec((1, D_MODEL)), _const_spec(w_in_s.shape),
            pl.BlockSpec((nseq, WINDOW, KV_WIDTH), lambda i: (i, 0, 0)),
            pl.BlockSpec((nseq, WINDOW, KV_WIDTH), lambda i: (i, 0, 0)),
            pl.BlockSpec((nseq, nstate), lambda i: (i, 0)),
            pl.BlockSpec((nseq, nstate), lambda i: (i, 0)),
            _const_spec(bb.shape), _const_spec(cc.shape),
            _const_spec(a_re.shape), _const_spec(a_im.shape),
            _const_spec((1, SSM_WIDTH)),
            _const_spec(w_glu.shape), _const_spec(w_out.shape),
        ],
        out_specs=[
            row_spec(D_MODEL), row_spec(KV_WIDTH), row_spec(KV_WIDTH),
            pl.BlockSpec((nseq, nstate), lambda i: (i, 0)),
            pl.BlockSpec((nseq, nstate), lambda i: (i, 0)),
        ],
        out_shape=[
            jax.ShapeDtypeStruct((rows_all, D_MODEL), BF16),
            jax.ShapeDtypeStruct((rows_all, KV_WIDTH), F32),
            jax.ShapeDtypeStruct((rows_all, KV_WIDTH), F32),
            jax.ShapeDtypeStruct((rows_all // SAMPLE_T, nstate), F32),
            jax.ShapeDtypeStruct((rows_all // SAMPLE_T, nstate), F32),
        ],
        scratch_shapes=[
            pltpu.VMEM((rows, N_Q_HEADS * LANE), F32),
            pltpu.VMEM((rows, KV_WIDTH), F32),
            pltpu.VMEM((rows, KV_WIDTH), F32),
            pltpu.VMEM((rows, ATTN_WIDTH), F32),
            pltpu.VMEM((N_STRIP, rows, LANE), F32),
            pltpu.VMEM((N_STRIP, rows, LANE), F32),
        ],
        compiler_params=_params(1),
        name="sample_l0",
    )(sinks, x, g_pre, w_in_s, ck, cv, s_re, s_im, bb, cc, a_re, a_im, d_skip, w_glu, w_out)


def _sample_l1_kernel(x_ref, g_ref, w1_ref, b1_ref, st_ref, wst_ref, wnew_ref, bdw_ref, lng_ref, lnb_ref,
                      w2_ref, b2_ref, m_ref, gout_ref, c_scr, *, nseq, ntok):
    h = _rms(x_ref[...], g_ref[...]).astype(BF16)
    a = _dot(h, w1_ref[...]) + b1_ref[...]
    gg = a[:, :D_MODEL] * jax.nn.sigmoid(a[:, D_MODEL:])
    gout_ref[...] = gg
    g3 = gg.reshape(nseq, SAMPLE_T, D_MODEL)
    st = st_ref[...]
    c_scr[...] = jnp.zeros(c_scr.shape, F32)
    nslab = D_MODEL // LANE
    for t in range(ntok):
        ct = (jnp.sum(st * wst_ref[t][None], axis=1) + jnp.sum(g3 * wnew_ref[t][None], axis=1)
              + bdw_ref[...])
        for j in range(nslab):
            c_scr[j, pl.ds(t, nseq, stride=SAMPLE_T), :] = ct[:, j * LANE:(j + 1) * LANE]
    c = jnp.concatenate([c_scr[j] for j in range(nslab)], axis=1)
    act = _layer_norm_silu(c, lng_ref, lnb_ref).astype(BF16)
    m_ref[...] = (_dot(act, w2_ref[...]) + b2_ref[...]).astype(m_ref.dtype)


def _sample_l1(x, g_pre, cw, st, wst, wnew, *, nseq, ntok):
    w1, b1, _, bdw, lng, lnb, w2, b2 = cw
    rows_all = x.shape[0]
    rows = nseq * SAMPLE_T
    vec = lambda n: _const_spec((1, n))
    row_spec = pl.BlockSpec((rows, D_MODEL), lambda i: (i, 0))
    return pl.pallas_call(
        functools.partial(_sample_l1_kernel, nseq=nseq, ntok=ntok),
        grid=(rows_all // rows,),
        in_specs=[
            row_spec, vec(D_MODEL), _const_spec(w1.shape), vec(2 * D_MODEL),
            pl.BlockSpec((nseq, CONV_TAIL, D_MODEL), lambda i: (i, 0, 0)),
            _const_spec(wst.shape), _const_spec(wnew.shape),
            vec(D_MODEL), vec(D_MODEL), vec(D_MODEL), _const_spec(w2.shape), vec(D_MODEL),
        ],
        out_specs=[row_spec, row_spec],
        out_shape=[
            jax.ShapeDtypeStruct((rows_all, D_MODEL), BF16),
            jax.ShapeDtypeStruct((rows_all, D_MODEL), F32),
        ],
        scratch_shapes=[pltpu.VMEM((D_MODEL // LANE, rows, LANE), F32)],
        compiler_params=_params(1),
        name="sample_l1",
    )(x, g_pre, w1, b1, st, wst, wnew, bdw, lng, lnb, w2, b2)


PROMPT_TILE = 512
PROMPT_STEPS = 64
SAMPLE_SEQS = 32
SAMPLE_FFN_ROWS = 512


def kernel(x_prompt, x_sample, cache_k, cache_v, state_s5_re, state_s5_im, state_conv, norm_pre_mix, norm_post_mix, norm_pre_ffn, norm_post_ffn, w_in_ab, attn_sinks, s5_lambda_re, s5_lambda_im, s5_log_step, s5_b_re, s5_b_im, s5_c_re, s5_c_im, s5_d, w_glu, w_out_ab, w_pw1, b_pw1, w_dw, b_dw, conv_ln_g, conv_ln_b, w_pw2, b_pw2, w_ffn_gate, w_ffn_up, w_ffn_down):
    nb, seq, _ = x_prompt.shape
    ns, ntok, _ = x_sample.shape
    assert nb == SUBLANE and ntok <= SAMPLE_T and seq % PROMPT_TILE == 0 and seq % PROMPT_STEPS == 0
    row = lambda v: v.reshape(1, -1)
    gains = lambda layer, *names: tuple(row(n[layer]) for n in names)

    w_in = w_in_ab[0].astype(BF16)
    wq = w_in_ab[0][:, :ATTN_WIDTH].reshape(D_MODEL, N_Q_HEADS, HEAD_DIM)
    lane_kv = (jnp.arange(LANE) // HEAD_DIM)[None, :]
    head_kv = (jnp.arange(N_Q_HEADS) // Q_PER_KV)[:, None]
    wq_slab = jnp.where((lane_kv == head_kv)[None], jnp.concatenate([wq, wq], axis=2), 0.0)
    w_in_s = jnp.concatenate(
        [wq_slab.reshape(D_MODEL, N_Q_HEADS * LANE), w_in_ab[0][:, ATTN_WIDTH:]], axis=1).astype(BF16)
    w_glu_b = w_glu[0].astype(BF16)
    w_out = w_out_ab[0].astype(BF16)
    ffn_w = [(w_ffn_gate[l].astype(BF16), w_ffn_up[l].astype(BF16), w_ffn_down[l].astype(BF16))
             for l in range(2)]
    sinks = attn_sinks[0]
    tabs = _s5_tables(s5_lambda_re[0], s5_lambda_im[0], s5_log_step[0], s5_b_re[0], s5_b_im[0],
                      s5_c_re[0], s5_c_im[0])
    d_skip = row(s5_d[0])
    wdw = w_dw[0]
    conv_w = (w_pw1[0].astype(BF16), row(b_pw1[0]), jnp.repeat(wdw, SUBLANE, axis=0), row(b_dw[0]),
              row(conv_ln_g[0]), row(conv_ln_b[0]), w_pw2[0].astype(BF16), row(b_pw2[0]))
    tt = jnp.arange(ntok)[:, None]
    tap_st = jnp.arange(CONV_TAIL)[None, :] - tt
    wst = jnp.where((tap_st >= 0)[..., None], wdw[jnp.clip(tap_st, 0, CONV_WIDTH - 1)], 0.0)
    tap_new = CONV_TAIL + jnp.arange(SAMPLE_T)[None, :] - tt
    wnew = jnp.where((tap_new <= CONV_TAIL)[..., None], wdw[jnp.clip(tap_new, 0, CONV_WIDTH - 1)], 0.0)

    a_p, u_p, kk_p, vk_p = _prompt_proj_attn(
        x_prompt, row(norm_pre_mix[0]), w_in, sinks, tile=PROMPT_TILE)
    m0_p, hfin_p = _prompt_s5(
        u_p.reshape(seq * nb, SSM_WIDTH), a_p.reshape(seq * nb, ATTN_WIDTH), tabs, d_skip, w_glu_b, w_out,
        nb=nb, steps=PROMPT_STEPS)
    tgrid = (nb, seq // PROMPT_TILE)
    bmaj = pl.BlockSpec((None, PROMPT_TILE, D_MODEL), lambda b, i: (b, i, 0))
    tmaj = pl.BlockSpec((PROMPT_TILE, D_MODEL), lambda b, i: (i, b))
    x1_p = _post_ffn(
        x_prompt, m0_p.reshape(seq, nb * D_MODEL),
        gains(0, norm_post_mix, norm_pre_ffn, norm_post_ffn), ffn_w[0],
        grid=tgrid, x_spec=bmaj, m_spec=tmaj, o_spec=tmaj, o_shape=(seq, nb * D_MODEL), name="prompt_ffn0")

    m1_p, tail_p = _prompt_conv(
        x1_p.reshape(seq * nb, D_MODEL), row(norm_pre_mix[1]), conv_w, nb=nb, steps=PROMPT_STEPS)
    y_prompt = _post_ffn(
        x1_p, m1_p.reshape(seq, nb * D_MODEL),
        gains(1, norm_post_mix, norm_pre_ffn, norm_post_ffn), ffn_w[1],
        grid=tgrid, x_spec=tmaj, m_spec=tmaj, o_spec=bmaj, o_shape=(nb, seq, D_MODEL), name="prompt_ffn1")

    xs = jnp.pad(x_sample, ((0, 0), (0, SAMPLE_T - ntok), (0, 0))).reshape(ns * SAMPLE_T, D_MODEL)
    nstate = SSM_GROUPS * SSM_STATE
    m0_s, kn_s, vn_s, sre_s, sim_s = _sample_l0(
        xs, row(norm_pre_mix[0]), w_in_s, sinks,
        cache_k[0].reshape(ns, WINDOW, KV_WIDTH), cache_v[0].reshape(ns, WINDOW, KV_WIDTH),
        state_s5_re[0].reshape(ns, nstate), state_s5_im[0].reshape(ns, nstate),
        tabs, d_skip, w_glu_b, w_out, nseq=SAMPLE_SEQS, ntok=ntok)
    srow = pl.BlockSpec((SAMPLE_FFN_ROWS, D_MODEL), lambda i: (i, 0))
    sgrid = (ns * SAMPLE_T // SAMPLE_FFN_ROWS,)
    x1_s = _post_ffn(
        xs, m0_s, gains(0, norm_post_mix, norm_pre_ffn, norm_post_ffn), ffn_w[0],
        grid=sgrid, x_spec=srow, m_spec=srow, o_spec=srow, o_shape=xs.shape, name="sample_ffn0")
    m1_s, g_s = _sample_l1(x1_s, row(norm_pre_mix[1]), conv_w, state_conv[0], wst, wnew,
                           nseq=SAMPLE_SEQS, ntok=ntok)
    y_s = _post_ffn(
        x1_s, m1_s, gains(1, norm_post_mix, norm_pre_ffn, norm_post_ffn), ffn_w[1],
        grid=sgrid, x_spec=srow, m_spec=srow, o_spec=srow, o_shape=xs.shape, name="sample_ffn1")

    y_sample = y_s.reshape(ns, SAMPLE_T, D_MODEL)[:, :ntok]
    k_prompt = kk_p.reshape(1, nb, WINDOW, N_KV_HEADS, HEAD_DIM)
    v_prompt = vk_p.reshape(1, nb, WINDOW, N_KV_HEADS, HEAD_DIM)
    hfin = jnp.swapaxes(hfin_p, 0, 1)
    s5_re_prompt = hfin[:, :, :STRIP_STATES].reshape(1, nb, SSM_GROUPS, SSM_STATE)
    s5_im_prompt = hfin[:, :, STRIP_STATES:].reshape(1, nb, SSM_GROUPS, SSM_STATE)
    conv_prompt = jnp.swapaxes(tail_p.reshape(CONV_TAIL, nb, D_MODEL), 0, 1)[None]
    k_new = kn_s.reshape(ns, SAMPLE_T, N_KV_HEADS, HEAD_DIM)[:, :ntok]
    v_new = vn_s.reshape(ns, SAMPLE_T, N_KV_HEADS, HEAD_DIM)[:, :ntok]
    k_sample = jnp.concatenate([cache_k[0][:, ntok:], k_new], axis=1)[None]
    v_sample = jnp.concatenate([cache_v[0][:, ntok:], v_new], axis=1)[None]
    s5_re_sample = sre_s.reshape(1, ns, SSM_GROUPS, SSM_STATE)
    s5_im_sample = sim_s.reshape(1, ns, SSM_GROUPS, SSM_STATE)
    g_new = g_s.reshape(ns, SAMPLE_T, D_MODEL)[:, :ntok]
    conv_sample = jnp.concatenate([state_conv[0][:, ntok:], g_new], axis=1)[None]
    return (y_prompt, y_sample, k_prompt, v_prompt, s5_re_prompt, s5_im_prompt, conv_prompt,
            k_sample, v_sample, s5_re_sample, s5_im_sample, conv_sample)
```

```python
import functools

import jax
import jax.numpy as jnp
from jax import lax
from jax.experimental import pallas as pl
from jax.experimental.pallas import tpu as pltpu

F32 = jnp.float32
BF16 = jnp.bfloat16

LANE = 128
SUBLANE = 8
VMEM_LIMIT_BYTES = 56 * 1024 * 1024

D_MODEL = 1024
HEAD_DIM = 64
N_Q_HEADS = 8
N_KV_HEADS = 2
Q_PER_KV = N_Q_HEADS // N_KV_HEADS
WINDOW = 128
ATTN_WIDTH = N_Q_HEADS * HEAD_DIM
KV_WIDTH = N_KV_HEADS * HEAD_DIM
SSM_WIDTH = 512
SSM_GROUP = 16
SSM_GROUPS = 32
SSM_STATE = 64
D_FF = 2816
CONV_WIDTH = 31
CONV_TAIL = CONV_WIDTH - 1
RMS_EPS = 1e-6
LN_EPS = 1e-5

N_STRIP = SSM_WIDTH // LANE
STRIP_STATES = (SSM_GROUPS // N_STRIP) * SSM_STATE
D_SLABS = D_MODEL // LANE
ATTN_SLABS = ATTN_WIDTH // LANE

SAMPLE_T = SUBLANE


def _rms(x, g):
    return x * lax.rsqrt(jnp.mean(x * x, axis=-1, keepdims=True) + RMS_EPS) * g


def _dot(a, b):
    return jnp.dot(a, b, preferred_element_type=F32)


def _dot_nt(a, b):
    return lax.dot_general(a, b, (((1,), (1,)), ((), ())), preferred_element_type=F32)


def _const_spec(shape):
    nd = len(shape)
    return pl.BlockSpec(shape, lambda *_: (0,) * nd, pipeline_mode=pl.Buffered(1))


def _params(n_axes):
    return pltpu.CompilerParams(
        dimension_semantics=("arbitrary",) * n_axes, vmem_limit_bytes=VMEM_LIMIT_BYTES)


def _load_time_major(x_ref, slab_scr, nb):
    steps = x_ref.shape[1]
    nslab = x_ref.shape[2] // LANE
    for b in range(nb):
        for j in range(nslab):
            slab_scr[j, pl.ds(b, steps, stride=nb), :] = x_ref[b, :, j * LANE:(j + 1) * LANE]
    return jnp.concatenate([slab_scr[j] for j in range(nslab)], axis=1)


def _store_sequence_major(val, slab_scr, out_ref, nb):
    steps = out_ref.shape[1]
    nslab = out_ref.shape[2] // LANE
    for j in range(nslab):
        slab_scr[j] = val[:, j * LANE:(j + 1) * LANE]
    for b in range(nb):
        for j in range(nslab):
            out_ref[b, :, j * LANE:(j + 1) * LANE] = (
                slab_scr[j, pl.ds(b, steps, stride=nb), :].astype(out_ref.dtype))


def _s5_disc_kernel(lre_ref, lim_ref, ls_ref, bre_ref, bim_ref, are_ref, aim_ref, bbre_ref, bbim_ref):
    lam_re = lre_ref[...]
    lam_im = lim_ref[...]
    dt = jnp.exp(ls_ref[...])
    mag = jnp.exp(lam_re * dt)
    lb_re = mag * jnp.cos(lam_im * dt)
    lb_im = mag * jnp.sin(lam_im * dt)
    den = lam_re * lam_re + lam_im * lam_im
    coef_re = ((lb_re - 1.0) * lam_re + lb_im * lam_im) / den
    coef_im = (lb_im * lam_re - (lb_re - 1.0) * lam_im) / den
    b_re = bre_ref[...]
    b_im = bim_ref[...]
    are_ref[...] = lb_re
    aim_ref[...] = lb_im
    bbre_ref[...] = coef_re * b_re - coef_im * b_im
    bbim_ref[...] = coef_re * b_im + coef_im * b_re


def _s5_tables(lam_re, lam_im, log_step, b_re, b_im, c_re, c_im):
    rep = lambda v: jnp.repeat(v, SSM_GROUP, axis=0)
    rows = SSM_GROUPS * SSM_GROUP
    bt = lambda b: jnp.swapaxes(b, 1, 2).reshape(rows, SSM_STATE)
    ls = jnp.broadcast_to(log_step[:, None], (SSM_GROUPS, SSM_STATE))
    shp = jax.ShapeDtypeStruct((rows, SSM_STATE), F32)
    a_re, a_im, bb_re, bb_im = pl.pallas_call(
        _s5_disc_kernel, out_shape=(shp, shp, shp, shp), name="s5_disc",
    )(rep(lam_re), rep(lam_im), rep(ls), bt(b_re), bt(b_im))
    a_re = a_re[::SSM_GROUP].reshape(N_STRIP, STRIP_STATES)
    a_im = a_im[::SSM_GROUP].reshape(N_STRIP, STRIP_STATES)
    gl = SSM_GROUPS // N_STRIP
    eye = jnp.eye(gl, dtype=F32)

    def in_blockdiag(m):
        m = m.reshape(N_STRIP, gl, SSM_GROUP, SSM_STATE)
        return jnp.einsum("jgcp,gh->jgchp", m, eye).reshape(N_STRIP, LANE, STRIP_STATES)

    def out_blockdiag(m):
        m = m.reshape(N_STRIP, gl, SSM_GROUP, SSM_STATE)
        return jnp.einsum("jgcp,gh->jgphc", m, eye).reshape(N_STRIP, STRIP_STATES, LANE)

    bb = jnp.concatenate([in_blockdiag(bb_re), in_blockdiag(bb_im)], axis=2).astype(BF16)
    cc = jnp.concatenate([out_blockdiag(c_re), -out_blockdiag(c_im)], axis=1).astype(BF16)
    return a_re, a_im, bb, cc


def _gelu_glu(y, wglu_ref):
    g = jax.nn.gelu(y)
    return g * jax.nn.sigmoid(_dot(g.astype(BF16), wglu_ref[...]))


def _band_attention(q, kv_scr, sinks, bias, nb):
    gw = Q_PER_KV * HEAD_DIM
    left = lax.broadcasted_iota(jnp.int32, (2 * WINDOW, KV_WIDTH), 1) < HEAD_DIM
    head_of_lane = lax.shift_right_logical(
        lax.broadcasted_iota(jnp.int32, (WINDOW, gw), 1), HEAD_DIM.bit_length() - 1)
    scores, values = [], []
    for b in range(nb):
        kb = kv_scr[b, :, 0:KV_WIDTH]
        vb = kv_scr[b, :, KV_WIDTH:2 * KV_WIDTH]
        kr = pltpu.roll(kb, HEAD_DIM, axis=1)
        vr = pltpu.roll(vb, HEAD_DIM, axis=1)
        for g in range(N_KV_HEADS):
            kk = jnp.where(left, kb, kr) if g == 0 else jnp.where(left, kr, kb)
            vv = jnp.where(left, vb, vr) if g == 0 else jnp.where(left, vr, vb)
            k4 = jnp.concatenate([kk, kk], axis=1).astype(BF16)
            values.append(jnp.concatenate([vv, vv], axis=1).astype(BF16))
            qg = q[b * WINDOW:(b + 1) * WINDOW, g * gw:(g + 1) * gw]
            qst = jnp.concatenate(
                [jnp.where(head_of_lane == r, qg, 0.0) for r in range(Q_PER_KV)], axis=0).astype(BF16)
            scores.append(_dot_nt(qst, k4))
    nblk = len(scores)
    s = jnp.concatenate(scores, axis=0) + jnp.concatenate([bias] * nblk, axis=0)
    sink = jnp.concatenate(
        [jnp.full((WINDOW, 1), sinks[(n % N_KV_HEADS) * Q_PER_KV + r], F32)
         for n in range(nblk) for r in range(Q_PER_KV)], axis=0)
    m = jnp.maximum(jnp.max(s, axis=1, keepdims=True), sink)
    p = jnp.exp(s - m)
    den = jnp.sum(p, axis=1, keepdims=True) + jnp.exp(sink - m)
    p = (p / den).astype(BF16)
    rows = Q_PER_KV * WINDOW
    outs = []
    for b in range(nb):
        og = []
        for g in range(N_KV_HEADS):
            n = b * N_KV_HEADS + g
            o4 = _dot(p[n * rows:(n + 1) * rows], values[n])
            o = jnp.where(head_of_lane == 0, o4[0:WINDOW], 0.0)
            for r in range(1, Q_PER_KV):
                o = jnp.where(head_of_lane == r, o4[r * WINDOW:(r + 1) * WINDOW], o)
            og.append(o)
        outs.append(jnp.concatenate(og, axis=1))
    return outs


def _proj_attn_kernel(sink_ref, x_ref, g_ref, w_ref, au_ref, kk_ref, vk_ref, kv_scr, *, nb):
    i = pl.program_id(0)
    x = x_ref[...].reshape(nb * WINDOW, D_MODEL)
    h = _rms(x, g_ref[...]).astype(BF16)
    z = _dot(h, w_ref[...])

    @pl.when(i == 0)
    def _():
        kv_scr[:, 0:WINDOW, :] = jnp.zeros((nb, WINDOW, 2 * KV_WIDTH), F32)

    for b in range(nb):
        kv_scr[b, WINDOW:2 * WINDOW, :] = z[b * WINDOW:(b + 1) * WINDOW, ATTN_WIDTH:ATTN_WIDTH + 2 * KV_WIDTH]
    q = z[:, :ATTN_WIDTH] * (HEAD_DIM ** -0.5)

    shape = (Q_PER_KV * WINDOW, 2 * WINDOW)
    row = jnp.bitwise_and(lax.broadcasted_iota(jnp.int32, shape, 0), WINDOW - 1)
    col = lax.broadcasted_iota(jnp.int32, shape, 1)
    ninf = jnp.float32(-jnp.inf)
    bias = jnp.where(col > row, jnp.where(col <= row + WINDOW, 0.0, ninf), ninf)
    bias = jnp.where(jnp.logical_and(col < WINDOW, i == 0), ninf, bias)
    sinks = [sink_ref[hh] for hh in range(N_Q_HEADS)]
    outs = _band_attention(q, kv_scr, sinks, bias, nb)
    ucol = ATTN_WIDTH + 2 * KV_WIDTH
    for b in range(nb):
        zb = z[b * WINDOW:(b + 1) * WINDOW]
        for j in range(ATTN_SLABS):
            au_ref[j, pl.ds(b, WINDOW, stride=nb), :] = outs[b][:, j * LANE:(j + 1) * LANE]
        for j in range(N_STRIP):
            au_ref[ATTN_SLABS + j, pl.ds(b, WINDOW, stride=nb), :] = zb[:, ucol + j * LANE:ucol + (j + 1) * LANE]
        kk_ref[b] = zb[:, ATTN_WIDTH:ATTN_WIDTH + KV_WIDTH]
        vk_ref[b] = zb[:, ATTN_WIDTH + KV_WIDTH:ATTN_WIDTH + 2 * KV_WIDTH]
    kv_scr[:, 0:WINDOW, :] = kv_scr[:, WINDOW:2 * WINDOW, :]


def _prompt_proj_attn(x, g_pre, w_in, sinks):
    nb, seq, _ = x.shape
    nslab = ATTN_SLABS + N_STRIP
    return pl.pallas_call(
        functools.partial(_proj_attn_kernel, nb=nb),
        grid=(seq // WINDOW,),
        in_specs=[
            pl.BlockSpec(memory_space=pltpu.SMEM),
            pl.BlockSpec((nb, WINDOW, D_MODEL), lambda i: (0, i, 0)),
            _const_spec((1, D_MODEL)),
            _const_spec(w_in.shape),
        ],
        out_specs=[
            pl.BlockSpec((nslab, WINDOW * nb, LANE), lambda i: (0, i, 0)),
            pl.BlockSpec((nb, WINDOW, KV_WIDTH), lambda i: (0, 0, 0)),
            pl.BlockSpec((nb, WINDOW, KV_WIDTH), lambda i: (0, 0, 0)),
        ],
        out_shape=[
            jax.ShapeDtypeStruct((nslab, seq * nb, LANE), F32),
            jax.ShapeDtypeStruct((nb, WINDOW, KV_WIDTH), F32),
            jax.ShapeDtypeStruct((nb, WINDOW, KV_WIDTH), F32),
        ],
        scratch_shapes=[pltpu.VMEM((nb, 2 * WINDOW, 2 * KV_WIDTH), F32)],
        compiler_params=_params(1),
        name="prompt_proj_attn",
    )(sinks, x, g_pre, w_in)


def _s5_prompt_kernel(au_ref, bb_ref, cc_ref, are_ref, aim_ref, d_ref, wglu_ref, wout_ref,
                      m_ref, hfin_ref, hbuf, slab_scr, *, steps, nb):
    i = pl.program_id(0)

    @pl.when(i == 0)
    def _():
        hfin_ref[...] = jnp.zeros(hfin_ref.shape, F32)

    ys = []
    for j in range(N_STRIP):
        uj = au_ref[ATTN_SLABS + j]
        hbuf[...] = _dot(uj.astype(BF16), bb_ref[j])
        are = jnp.broadcast_to(are_ref[j:j + 1, :], (nb, STRIP_STATES))
        aim = jnp.broadcast_to(aim_ref[j:j + 1, :], (nb, STRIP_STATES))

        def step(t, carry):
            hr, hi = carry
            r = pl.multiple_of(t * nb, nb)
            br = hbuf[pl.ds(r, nb), 0:STRIP_STATES]
            bi = hbuf[pl.ds(r, nb), STRIP_STATES:2 * STRIP_STATES]
            nr = are * hr - aim * hi + br
            ni = are * hi + aim * hr + bi
            hbuf[pl.ds(r, nb), 0:STRIP_STATES] = nr
            hbuf[pl.ds(r, nb), STRIP_STATES:2 * STRIP_STATES] = ni
            return nr, ni

        h0 = (hfin_ref[j, :, 0:STRIP_STATES], hfin_ref[j, :, STRIP_STATES:2 * STRIP_STATES])
        hr, hi = lax.fori_loop(0, steps, step, h0, unroll=8)
        hfin_ref[j, :, 0:STRIP_STATES] = hr
        hfin_ref[j, :, STRIP_STATES:2 * STRIP_STATES] = hi
        ys.append(_dot(hbuf[...].astype(BF16), cc_ref[j]) + d_ref[:, j * LANE:(j + 1) * LANE] * uj)
    s = _gelu_glu(jnp.concatenate(ys, axis=1), wglu_ref)
    a = jnp.concatenate([au_ref[j] for j in range(ATTN_SLABS)], axis=1).astype(BF16)
    m = _dot(a, wout_ref[0:ATTN_WIDTH, :]) + _dot(s.astype(BF16), wout_ref[ATTN_WIDTH:, :])
    _store_sequence_major(m, slab_scr, m_ref, nb)


def _prompt_s5(au, tabs, d_skip, w_glu, w_out, *, nb, steps):
    a_re, a_im, bb, cc = tabs
    nslab, rows, _ = au.shape
    blk = steps * nb
    return pl.pallas_call(
        functools.partial(_s5_prompt_kernel, steps=steps, nb=nb),
        grid=(rows // blk,),
        in_specs=[
            pl.BlockSpec((nslab, blk, LANE), lambda i: (0, i, 0)),
            _const_spec(bb.shape), _const_spec(cc.shape),
            _const_spec(a_re.shape), _const_spec(a_im.shape),
            _const_spec((1, SSM_WIDTH)),
            _const_spec(w_glu.shape), _const_spec(w_out.shape),
        ],
        out_specs=[
            pl.BlockSpec((nb, steps, D_MODEL), lambda i: (0, i, 0)),
            pl.BlockSpec((N_STRIP, nb, 2 * STRIP_STATES), lambda i: (0, 0, 0)),
        ],
        out_shape=[
            jax.ShapeDtypeStruct((nb, rows // nb, D_MODEL), BF16),
            jax.ShapeDtypeStruct((N_STRIP, nb, 2 * STRIP_STATES), F32),
        ],
        scratch_shapes=[pltpu.VMEM((blk, 2 * STRIP_STATES), F32),
                        pltpu.VMEM((D_SLABS, blk, LANE), F32)],
        compiler_params=_params(1),
        name="prompt_s5",
    )(au, bb, cc, a_re, a_im, d_skip, w_glu, w_out)


def _post_ffn_kernel(x_ref, m_ref, gpm_ref, gpf_ref, gof_ref, wg_ref, wu_ref, wd_ref, o_ref):
    x1 = x_ref[...] + _rms(m_ref[...].astype(F32), gpm_ref[...])
    h = _rms(x1, gpf_ref[...]).astype(BF16)
    gate = _dot(h, wg_ref[...])
    up = _dot(h, wu_ref[...])
    act = (gate * jax.nn.sigmoid(gate) * up).astype(BF16)
    f = _dot(act, wd_ref[...])
    o_ref[...] = x1 + _rms(f, gof_ref[...])


def _post_ffn(x, m, gains, ffn_w, *, rows, name):
    gpm, gpf, gof = gains
    wg, wu, wd = ffn_w
    row_spec = pl.BlockSpec((rows, D_MODEL), lambda i: (i, 0))
    return pl.pallas_call(
        _post_ffn_kernel,
        grid=(x.shape[0] // rows,),
        in_specs=[row_spec, row_spec,
                  _const_spec((1, D_MODEL)), _const_spec((1, D_MODEL)), _const_spec((1, D_MODEL)),
                  _const_spec(wg.shape), _const_spec(wu.shape), _const_spec(wd.shape)],
        out_specs=row_spec,
        out_shape=jax.ShapeDtypeStruct(x.shape, F32),
        compiler_params=_params(1),
        name=name,
    )(x, m, gpm, gpf, gof, wg, wu, wd)


CONV_HALO = 32
CONV_ROWS = 64


def _layer_norm_silu(c, lng_ref, lnb_ref):
    mu = jnp.mean(c, axis=-1, keepdims=True)
    cc = c - mu
    var = jnp.mean(cc * cc, axis=-1, keepdims=True)
    cn = cc * lax.rsqrt(var + LN_EPS) * lng_ref[...] + lnb_ref[...]
    return cn * jax.nn.sigmoid(cn)


def _conv_prompt_kernel(x_ref, g_ref, w1_ref, b1_ref, wdw_ref, bdw_ref, lng_ref, lnb_ref, w2_ref, b2_ref,
                        m_ref, tail_ref, gp_scr, c_scr, slab_scr, *, blk, nb):
    i = pl.program_id(0)
    halo = CONV_HALO * nb
    tail = CONV_TAIL * nb

    @pl.when(i == 0)
    def _():
        gp_scr[0:halo, :] = jnp.zeros((halo, D_MODEL), F32)

    x = _load_time_major(x_ref, slab_scr, nb)
    h = _rms(x, g_ref[...]).astype(BF16)
    a = _dot(h, w1_ref[...]) + b1_ref[...]
    gp_scr[halo:halo + blk, :] = a[:, :D_MODEL] * jax.nn.sigmoid(a[:, D_MODEL:])

    def chunk(ci, carry):
        base = pl.multiple_of(ci * CONV_ROWS, CONV_ROWS)
        for l in range(D_SLABS):
            lanes = slice(l * LANE, (l + 1) * LANE)
            xs = gp_scr[pl.ds(base + halo - tail, CONV_ROWS + tail), lanes]
            acc = jnp.broadcast_to(bdw_ref[:, lanes], (CONV_ROWS, LANE))
            for j in range(CONV_WIDTH):
                wj = wdw_ref[j * SUBLANE:(j + 1) * SUBLANE, lanes]
                wj = jnp.concatenate([wj] * (CONV_ROWS // SUBLANE), axis=0)
                acc = acc + wj * xs[j * nb:j * nb + CONV_ROWS]
            c_scr[pl.ds(base, CONV_ROWS), lanes] = acc
        return carry

    lax.fori_loop(0, blk // CONV_ROWS, chunk, 0)
    act = _layer_norm_silu(c_scr[...], lng_ref, lnb_ref).astype(BF16)
    _store_sequence_major(_dot(act, w2_ref[...]) + b2_ref[...], slab_scr, m_ref, nb)
    tail_ref[...] = gp_scr[halo + blk - tail:halo + blk, :]
    gp_scr[0:halo, :] = gp_scr[blk:blk + halo, :]


def _prompt_conv(x, g_pre, cw, *, steps):
    w1, b1, wdw8, bdw, lng, lnb, w2, b2 = cw
    nb, seq, _ = x.shape
    blk = steps * nb
    assert steps >= CONV_HALO and blk % CONV_ROWS == 0
    vec = lambda n: _const_spec((1, n))
    seq_spec = pl.BlockSpec((nb, steps, D_MODEL), lambda i: (0, i, 0))
    return pl.pallas_call(
        functools.partial(_conv_prompt_kernel, blk=blk, nb=nb),
        grid=(seq // steps,),
        in_specs=[
            seq_spec,
            vec(D_MODEL), _const_spec(w1.shape), vec(2 * D_MODEL),
            _const_spec(wdw8.shape), vec(D_MODEL), vec(D_MODEL), vec(D_MODEL),
            _const_spec(w2.shape), vec(D_MODEL),
        ],
        out_specs=[
            seq_spec,
            pl.BlockSpec((CONV_TAIL * nb, D_MODEL), lambda i: (0, 0)),
        ],
        out_shape=[
            jax.ShapeDtypeStruct((nb, seq, D_MODEL), BF16),
            jax.ShapeDtypeStruct((CONV_TAIL * nb, D_MODEL), F32),
        ],
        scratch_shapes=[pltpu.VMEM((blk + CONV_HALO * nb, D_MODEL), F32),
                        pltpu.VMEM((blk, D_MODEL), F32),
                        pltpu.VMEM((D_SLABS, blk, LANE), F32)],
        compiler_params=_params(1),
        name="prompt_conv",
    )(x, g_pre, w1, b1, wdw8, bdw, lng, lnb, w2, b2)


def _sample_l0_kernel(sink_ref, x_ref, g_ref, w_ref, ck_ref, cv_ref, sre_ref, sim_ref,
                      bb_ref, cc_ref, are_ref, aim_ref, d_ref, wglu_ref, wout_ref,
                      m_ref, kn_ref, vn_ref, ore_ref, oim_ref,
                      q_scr, k_scr, v_scr, a_scr, u_scr, y_scr, *, nseq, ntok):
    qw = N_Q_HEADS * LANE
    h = _rms(x_ref[...], g_ref[...]).astype(BF16)
    z = _dot(h, w_ref[...])
    q_scr[...] = z[:, :qw] * (HEAD_DIM ** -0.5)
    kn = z[:, qw:qw + KV_WIDTH]
    vn = z[:, qw + KV_WIDTH:qw + 2 * KV_WIDTH]
    k_scr[...] = kn
    v_scr[...] = vn
    kn_ref[...] = kn
    vn_ref[...] = vn
    for j in range(N_STRIP):
        u_scr[j] = z[:, qw + 2 * KV_WIDTH + j * LANE:qw + 2 * KV_WIDTH + (j + 1) * LANE]
    y_scr[...] = jnp.zeros(y_scr.shape, F32)

    for j in range(N_STRIP):
        hr = sre_ref[:, j * STRIP_STATES:(j + 1) * STRIP_STATES]
        hi = sim_ref[:, j * STRIP_STATES:(j + 1) * STRIP_STATES]
        are = are_ref[j:j + 1, :]
        aim = aim_ref[j:j + 1, :]
        dj = d_ref[:, j * LANE:(j + 1) * LANE]
        for t in range(ntok):
            ut = u_scr[j, pl.ds(t, nseq, stride=SAMPLE_T), :]
            bu = _dot(ut.astype(BF16), bb_ref[j])
            hr, hi = (are * hr - aim * hi + bu[:, :STRIP_STATES],
                      are * hi + aim * hr + bu[:, STRIP_STATES:])
            hcat = jnp.concatenate([hr, hi], axis=1).astype(BF16)
            y_scr[j, pl.ds(t, nseq, stride=SAMPLE_T), :] = _dot(hcat, cc_ref[j]) + dj * ut
        ore_ref[:, j * STRIP_STATES:(j + 1) * STRIP_STATES] = hr
        oim_ref[:, j * STRIP_STATES:(j + 1) * STRIP_STATES] = hi
    s = _gelu_glu(jnp.concatenate([y_scr[j] for j in range(N_STRIP)], axis=1), wglu_ref)

    nkeys = WINDOW + SAMPLE_T
    shape = (N_Q_HEADS * SAMPLE_T, nkeys)
    tq = jnp.bitwise_and(lax.broadcasted_iota(jnp.int32, shape, 0), SAMPLE_T - 1)
    col = lax.broadcasted_iota(jnp.int32, shape, 1)
    ninf = jnp.float32(-jnp.inf)
    bias = jnp.where(col < WINDOW, jnp.where(col > tq, 0.0, ninf),
                     jnp.where(col - WINDOW <= tq, 0.0, ninf))
    sink = jnp.concatenate(
        [jnp.full((SAMPLE_T, 1), sink_ref[hh], F32) for hh in range(N_Q_HEADS)], axis=0)
    left = lax.broadcasted_iota(jnp.int32, (SAMPLE_T, LANE), 1) < HEAD_DIM

    def seq_body(n, carry):
        r = pl.multiple_of(n * SAMPLE_T, SAMPLE_T)
        qs = jnp.concatenate(
            [q_scr[pl.ds(r, SAMPLE_T), hh * LANE:(hh + 1) * LANE] for hh in range(N_Q_HEADS)], axis=0)
        kall = jnp.concatenate([ck_ref[n], k_scr[pl.ds(r, SAMPLE_T), :]], axis=0)
        vall = jnp.concatenate([cv_ref[n], v_scr[pl.ds(r, SAMPLE_T), :]], axis=0)
        sc = _dot_nt(qs.astype(BF16), kall.astype(BF16)) + bias
        mx = jnp.maximum(jnp.max(sc, axis=1, keepdims=True), sink)
        p = jnp.exp(sc - mx)
        den = jnp.sum(p, axis=1, keepdims=True) + jnp.exp(sink - mx)
        p = (p / den).astype(BF16)
        o1 = _dot(p, vall.astype(BF16))
        o2 = _dot(p, pltpu.roll(vall, HEAD_DIM, axis=1).astype(BF16))
        slabs = []
        for hp in range(N_Q_HEADS // 2):
            lo, ro = (o1, o2) if hp < Q_PER_KV // 2 else (o2, o1)
            lrow = 2 * hp * SAMPLE_T
            slabs.append(jnp.where(left, lo[lrow:lrow + SAMPLE_T], ro[lrow + SAMPLE_T:lrow + 2 * SAMPLE_T]))
        a_scr[pl.ds(r, SAMPLE_T), :] = jnp.concatenate(slabs, axis=1)
        return carry

    lax.fori_loop(0, nseq, seq_body, 0)
    m = (_dot(a_scr[...].astype(BF16), wout_ref[0:ATTN_WIDTH, :])
         + _dot(s.astype(BF16), wout_ref[ATTN_WIDTH:, :]))
    m_ref[...] = m.astype(m_ref.dtype)


def _sample_l0(x, g_pre, w_in_s, sinks, ck, cv, s_re, s_im, tabs, d_skip, w_glu, w_out, *, nseq, ntok):
    a_re, a_im, bb, cc = tabs
    rows_all = x.shape[0]
    rows = nseq * SAMPLE_T
    nstate = SSM_GROUPS * SSM_STATE
    row_spec = lambda w: pl.BlockSpec((rows, w), lambda i: (i, 0))
    return pl.pallas_call(
        functools.partial(_sample_l0_kernel, nseq=nseq, ntok=ntok),
        grid=(rows_all // rows,),
        in_specs=[
            pl.BlockSpec(memory_space=pltpu.SMEM),
            row_spec(D_MODEL), _const_spec((1, D_MODEL)), _const_spec(w_in_s.shape),
            pl.BlockSpec((nseq, WINDOW, KV_WIDTH), lambda i: (i, 0, 0)),
            pl.BlockSpec((nseq, WINDOW, KV_WIDTH), lambda i: (i, 0, 0)),
            pl.BlockSpec((nseq, nstate), lambda i: (i, 0)),
            pl.BlockSpec((nseq, nstate), lambda i: (i, 0)),
            _const_spec(bb.shape), _const_spec(cc.shape),
            _const_spec(a_re.shape), _const_spec(a_im.shape),
            _const_spec((1, SSM_WIDTH)),
            _const_spec(w_glu.shape), _const_spec(w_out.shape),
        ],
        out_specs=[
            row_spec(D_MODEL), row_spec(KV_WIDTH), row_spec(KV_WIDTH),
            pl.BlockSpec((nseq, nstate), lambda i: (i, 0)),
            pl.BlockSpec((nseq, nstate), lambda i: (i, 0)),
        ],
        out_shape=[
            jax.ShapeDtypeStruct((rows_all, D_MODEL), BF16),
            jax.ShapeDtypeStruct((rows_all, KV_WIDTH), F32),
            jax.ShapeDtypeStruct((rows_all, KV_WIDTH), F32),
            jax.ShapeDtypeStruct((rows_all // SAMPLE_T, nstate), F32),
            jax.ShapeDtypeStruct((rows_all // SAMPLE_T, nstate), F32),
        ],
        scratch_shapes=[
            pltpu.VMEM((rows, N_Q_HEADS * LANE), F32),
            pltpu.VMEM((rows, KV_WIDTH), F32),
            pltpu.VMEM((rows, KV_WIDTH), F32),
            pltpu.VMEM((rows, ATTN_WIDTH), F32),
            pltpu.VMEM((N_STRIP, rows, LANE), F32),
            pltpu.VMEM((N_STRIP, rows, LANE), F32),
        ],
        compiler_params=_params(1),
        name="sample_l0",
    )(sinks, x, g_pre, w_in_s, ck, cv, s_re, s_im, bb, cc, a_re, a_im, d_skip, w_glu, w_out)


def _sample_l1_kernel(x_ref, g_ref, w1_ref, b1_ref, st_ref, wst_ref, wnew_ref, bdw_ref, lng_ref, lnb_ref,
                      w2_ref, b2_ref, m_ref, gout_ref, c_scr, *, nseq, ntok):
    h = _rms(x_ref[...], g_ref[...]).astype(BF16)
    a = _dot(h, w1_ref[...]) + b1_ref[...]
    gg = a[:, :D_MODEL] * jax.nn.sigmoid(a[:, D_MODEL:])
    gout_ref[...] = gg
    g3 = gg.reshape(nseq, SAMPLE_T, D_MODEL)
    st = st_ref[...]
    c_scr[...] = jnp.zeros(c_scr.shape, F32)
    for t in range(ntok):
        ct = (jnp.sum(st * wst_ref[t][None], axis=1) + jnp.sum(g3 * wnew_ref[t][None], axis=1)
              + bdw_ref[...])
        for j in range(D_SLABS):
            c_scr[j, pl.ds(t, nseq, stride=SAMPLE_T), :] = ct[:, j * LANE:(j + 1) * LANE]
    c = jnp.concatenate([c_scr[j] for j in range(D_SLABS)], axis=1)
    act = _layer_norm_silu(c, lng_ref, lnb_ref).astype(BF16)
    m_ref[...] = (_dot(act, w2_ref[...]) + b2_ref[...]).astype(m_ref.dtype)


def _sample_l1(x, g_pre, cw, st, wst, wnew, *, nseq, ntok):
    w1, b1, _, bdw, lng, lnb, w2, b2 = cw
    rows_all = x.shape[0]
    rows = nseq * SAMPLE_T
    vec = lambda n: _const_spec((1, n))
    row_spec = pl.BlockSpec((rows, D_MODEL), lambda i: (i, 0))
    return pl.pallas_call(
        functools.partial(_sample_l1_kernel, nseq=nseq, ntok=ntok),
        grid=(rows_all // rows,),
        in_specs=[
            row_spec, vec(D_MODEL), _const_spec(w1.shape), vec(2 * D_MODEL),
            pl.BlockSpec((nseq, CONV_TAIL, D_MODEL), lambda i: (i, 0, 0)),
            _const_spec(wst.shape), _const_spec(wnew.shape),
            vec(D_MODEL), vec(D_MODEL), vec(D_MODEL), _const_spec(w2.shape), vec(D_MODEL),
        ],
        out_specs=[row_spec, row_spec],
        out_shape=[
            jax.ShapeDtypeStruct((rows_all, D_MODEL), BF16),
            jax.ShapeDtypeStruct((rows_all, D_MODEL), F32),
        ],
        scratch_shapes=[pltpu.VMEM((D_SLABS, rows, LANE), F32)],
        compiler_params=_params(1),
        name="sample_l1",
    )(x, g_pre, w1, b1, st, wst, wnew, bdw, lng, lnb, w2, b2)


PROMPT_FFN_ROWS = 512
PROMPT_STEPS = 64
SAMPLE_SEQS = 32
SAMPLE_FFN_ROWS = 512


def kernel(x_prompt, x_sample, cache_k, cache_v, state_s5_re, state_s5_im, state_conv, norm_pre_mix, norm_post_mix, norm_pre_ffn, norm_post_ffn, w_in_ab, attn_sinks, s5_lambda_re, s5_lambda_im, s5_log_step, s5_b_re, s5_b_im, s5_c_re, s5_c_im, s5_d, w_glu, w_out_ab, w_pw1, b_pw1, w_dw, b_dw, conv_ln_g, conv_ln_b, w_pw2, b_pw2, w_ffn_gate, w_ffn_up, w_ffn_down):
    nb, seq, _ = x_prompt.shape
    ns, ntok, _ = x_sample.shape
    assert nb == SUBLANE and ntok <= SAMPLE_T and seq % WINDOW == 0 and seq % PROMPT_STEPS == 0
    row = lambda v: v.reshape(1, -1)
    gains = lambda layer: tuple(row(n[layer]) for n in (norm_post_mix, norm_pre_ffn, norm_post_ffn))

    w_in = w_in_ab[0].astype(BF16)
    wq = w_in_ab[0][:, :ATTN_WIDTH].reshape(D_MODEL, N_Q_HEADS, HEAD_DIM)
    lane_kv = (jnp.arange(LANE) // HEAD_DIM)[None, :]
    head_kv = (jnp.arange(N_Q_HEADS) // Q_PER_KV)[:, None]
    wq_slab = jnp.where((lane_kv == head_kv)[None], jnp.concatenate([wq, wq], axis=2), 0.0)
    w_in_s = jnp.concatenate(
        [wq_slab.reshape(D_MODEL, N_Q_HEADS * LANE), w_in_ab[0][:, ATTN_WIDTH:]], axis=1).astype(BF16)
    w_glu_b = w_glu[0].astype(BF16)
    w_out = w_out_ab[0].astype(BF16)
    ffn_w = [(w_ffn_gate[l].astype(BF16), w_ffn_up[l].astype(BF16), w_ffn_down[l].astype(BF16))
             for l in range(2)]
    sinks = attn_sinks[0]
    tabs = _s5_tables(s5_lambda_re[0], s5_lambda_im[0], s5_log_step[0], s5_b_re[0], s5_b_im[0],
                      s5_c_re[0], s5_c_im[0])
    d_skip = row(s5_d[0])
    wdw = w_dw[0]
    conv_w = (w_pw1[0].astype(BF16), row(b_pw1[0]), jnp.repeat(wdw, SUBLANE, axis=0), row(b_dw[0]),
              row(conv_ln_g[0]), row(conv_ln_b[0]), w_pw2[0].astype(BF16), row(b_pw2[0]))
    tt = jnp.arange(ntok)[:, None]
    tap_st = jnp.arange(CONV_TAIL)[None, :] - tt
    wst = jnp.where((tap_st >= 0)[..., None], wdw[jnp.clip(tap_st, 0, CONV_WIDTH - 1)], 0.0)
    tap_new = CONV_TAIL + jnp.arange(SAMPLE_T)[None, :] - tt
    wnew = jnp.where((tap_new <= CONV_TAIL)[..., None], wdw[jnp.clip(tap_new, 0, CONV_WIDTH - 1)], 0.0)

    flat = lambda v: v.reshape(nb * seq, D_MODEL)
    au_p, kk_p, vk_p = _prompt_proj_attn(x_prompt, row(norm_pre_mix[0]), w_in, sinks)
    m0_p, hfin_p = _prompt_s5(au_p, tabs, d_skip, w_glu_b, w_out, nb=nb, steps=PROMPT_STEPS)
    x1_p = _post_ffn(flat(x_prompt), flat(m0_p), gains(0), ffn_w[0], rows=PROMPT_FFN_ROWS, name="prompt_ffn0")
    m1_p, tail_p = _prompt_conv(x1_p.reshape(nb, seq, D_MODEL), row(norm_pre_mix[1]), conv_w, steps=PROMPT_STEPS)
    y_prompt = _post_ffn(x1_p, flat(m1_p), gains(1), ffn_w[1], rows=PROMPT_FFN_ROWS, name="prompt_ffn1")
    y_prompt = y_prompt.reshape(nb, seq, D_MODEL)

    xs = jnp.pad(x_sample, ((0, 0), (0, SAMPLE_T - ntok), (0, 0))).reshape(ns * SAMPLE_T, D_MODEL)
    nstate = SSM_GROUPS * SSM_STATE
    m0_s, kn_s, vn_s, sre_s, sim_s = _sample_l0(
        xs, row(norm_pre_mix[0]), w_in_s, sinks,
        cache_k[0].reshape(ns, WINDOW, KV_WIDTH), cache_v[0].reshape(ns, WINDOW, KV_WIDTH),
        state_s5_re[0].reshape(ns, nstate), state_s5_im[0].reshape(ns, nstate),
        tabs, d_skip, w_glu_b, w_out, nseq=SAMPLE_SEQS, ntok=ntok)
    x1_s = _post_ffn(xs, m0_s, gains(0), ffn_w[0], rows=SAMPLE_FFN_ROWS, name="sample_ffn0")
    m1_s, g_s = _sample_l1(x1_s, row(norm_pre_mix[1]), conv_w, state_conv[0], wst, wnew,
                           nseq=SAMPLE_SEQS, ntok=ntok)
    y_s = _post_ffn(x1_s, m1_s, gains(1), ffn_w[1], rows=SAMPLE_FFN_ROWS, name="sample_ffn1")

    y_sample = y_s.reshape(ns, SAMPLE_T, D_MODEL)[:, :ntok]
    k_prompt = kk_p.reshape(1, nb, WINDOW, N_KV_HEADS, HEAD_DIM)
    v_prompt = vk_p.reshape(1, nb, WINDOW, N_KV_HEADS, HEAD_DIM)
    hfin = jnp.swapaxes(hfin_p, 0, 1)
    s5_re_prompt = hfin[:, :, :STRIP_STATES].reshape(1, nb, SSM_GROUPS, SSM_STATE)
    s5_im_prompt = hfin[:, :, STRIP_STATES:].reshape(1, nb, SSM_GROUPS, SSM_STATE)
    conv_prompt = jnp.swapaxes(tail_p.reshape(CONV_TAIL, nb, D_MODEL), 0, 1)[None]
    k_new = kn_s.reshape(ns, SAMPLE_T, N_KV_HEADS, HEAD_DIM)[:, :ntok]
    v_new = vn_s.reshape(ns, SAMPLE_T, N_KV_HEADS, HEAD_DIM)[:, :ntok]
    k_sample = jnp.concatenate([cache_k[0][:, ntok:], k_new], axis=1)[None]
    v_sample = jnp.concatenate([cache_v[0][:, ntok:], v_new], axis=1)[None]
    s5_re_sample = sre_s.reshape(1, ns, SSM_GROUPS, SSM_STATE)
    s5_im_sample = sim_s.reshape(1, ns, SSM_GROUPS, SSM_STATE)
    g_new = g_s.reshape(ns, SAMPLE_T, D_MODEL)[:, :ntok]
    conv_sample = jnp.concatenate([state_conv[0][:, ntok:], g_new], axis=1)[None]
    return (y_prompt, y_sample, k_prompt, v_prompt, s5_re_prompt, s5_im_prompt, conv_prompt,
            k_sample, v_sample, s5_re_sample, s5_im_sample, conv_sample)
```

```python
import functools

import jax
import jax.numpy as jnp
from jax import lax
from jax.experimental import pallas as pl
from jax.experimental.pallas import tpu as pltpu

F32 = jnp.float32
BF16 = jnp.bfloat16

LANE = 128
SUBLANE = 8
BF16_ROWS = 16
VMEM_LIMIT_BYTES = 56 * 1024 * 1024

D_MODEL = 1024
HEAD_DIM = 64
N_Q_HEADS = 8
N_KV_HEADS = 2
Q_PER_KV = N_Q_HEADS // N_KV_HEADS
WINDOW = 128
ATTN_WIDTH = N_Q_HEADS * HEAD_DIM
KV_WIDTH = N_KV_HEADS * HEAD_DIM
SSM_WIDTH = 512
SSM_GROUP = 16
SSM_GROUPS = 32
SSM_STATE = 64
D_FF = 2816
CONV_WIDTH = 31
CONV_TAIL = CONV_WIDTH - 1
RMS_EPS = 1e-6
LN_EPS = 1e-5

N_STRIP = SSM_WIDTH // LANE
STRIP_STATES = (SSM_GROUPS // N_STRIP) * SSM_STATE
D_SLABS = D_MODEL // LANE
ATTN_SLABS = ATTN_WIDTH // LANE
FFN_SPLIT = 2

SAMPLE_T = SUBLANE


def _rms(x, g):
    return x * lax.rsqrt(jnp.mean(x * x, axis=-1, keepdims=True) + RMS_EPS) * g


def _dot(a, b):
    return jnp.dot(a, b, preferred_element_type=F32)


def _dot_nt(a, b):
    return lax.dot_general(a, b, (((1,), (1,)), ((), ())), preferred_element_type=F32)


def _const_spec(shape):
    nd = len(shape)
    return pl.BlockSpec(shape, lambda *_: (0,) * nd, pipeline_mode=pl.Buffered(1))


def _params(n_axes):
    return pltpu.CompilerParams(
        dimension_semantics=("arbitrary",) * n_axes, vmem_limit_bytes=VMEM_LIMIT_BYTES)


def _load_time_major(x_ref, slab_scr, nb):
    steps = x_ref.shape[1]
    nslab = x_ref.shape[2] // LANE
    for b in range(nb):
        for j in range(nslab):
            slab_scr[j, pl.ds(b, steps, stride=nb), :] = x_ref[b, :, j * LANE:(j + 1) * LANE]
    return jnp.concatenate([slab_scr[j] for j in range(nslab)], axis=1)


def _store_sequence_major(val, slab_scr, out_ref, nb):
    steps = out_ref.shape[1]
    nslab = out_ref.shape[2] // LANE
    for j in range(nslab):
        slab_scr[j] = val[:, j * LANE:(j + 1) * LANE]
    for b in range(nb):
        for j in range(nslab):
            out_ref[b, :, j * LANE:(j + 1) * LANE] = (
                slab_scr[j, pl.ds(b, steps, stride=nb), :].astype(out_ref.dtype))


def _s5_disc_kernel(lre_ref, lim_ref, ls_ref, bre_ref, bim_ref, are_ref, aim_ref, bbre_ref, bbim_ref):
    lam_re = lre_ref[...]
    lam_im = lim_ref[...]
    dt = jnp.exp(ls_ref[...])
    mag = jnp.exp(lam_re * dt)
    lb_re = mag * jnp.cos(lam_im * dt)
    lb_im = mag * jnp.sin(lam_im * dt)
    den = lam_re * lam_re + lam_im * lam_im
    coef_re = ((lb_re - 1.0) * lam_re + lb_im * lam_im) / den
    coef_im = (lb_im * lam_re - (lb_re - 1.0) * lam_im) / den
    b_re = bre_ref[...]
    b_im = bim_ref[...]
    are_ref[...] = lb_re
    aim_ref[...] = lb_im
    bbre_ref[...] = coef_re * b_re - coef_im * b_im
    bbim_ref[...] = coef_re * b_im + coef_im * b_re


def _s5_tables(lam_re, lam_im, log_step, b_re, b_im, c_re, c_im):
    rep = lambda v: jnp.repeat(v, SSM_GROUP, axis=0)
    rows = SSM_GROUPS * SSM_GROUP
    bt = lambda b: jnp.swapaxes(b, 1, 2).reshape(rows, SSM_STATE)
    ls = jnp.broadcast_to(log_step[:, None], (SSM_GROUPS, SSM_STATE))
    shp = jax.ShapeDtypeStruct((rows, SSM_STATE), F32)
    a_re, a_im, bb_re, bb_im = pl.pallas_call(
        _s5_disc_kernel, out_shape=(shp, shp, shp, shp), name="s5_disc",
    )(rep(lam_re), rep(lam_im), rep(ls), bt(b_re), bt(b_im))
    a_re = a_re[::SSM_GROUP].reshape(N_STRIP, STRIP_STATES)
    a_im = a_im[::SSM_GROUP].reshape(N_STRIP, STRIP_STATES)
    gl = SSM_GROUPS // N_STRIP
    eye = jnp.eye(gl, dtype=F32)

    def in_blockdiag(m):
        m = m.reshape(N_STRIP, gl, SSM_GROUP, SSM_STATE)
        return jnp.einsum("jgcp,gh->jgchp", m, eye).reshape(N_STRIP, LANE, STRIP_STATES)

    def out_blockdiag(m):
        m = m.reshape(N_STRIP, gl, SSM_GROUP, SSM_STATE)
        return jnp.einsum("jgcp,gh->jgphc", m, eye).reshape(N_STRIP, STRIP_STATES, LANE)

    bb = jnp.concatenate([in_blockdiag(bb_re), in_blockdiag(bb_im)], axis=2).astype(BF16)
    cc = jnp.concatenate([out_blockdiag(c_re), -out_blockdiag(c_im)], axis=1).astype(BF16)
    return a_re, a_im, bb, cc


def _gelu_glu(y, wglu_ref):
    g = jax.nn.gelu(y)
    return g * jax.nn.sigmoid(_dot(g.astype(BF16), wglu_ref[...]))


def _band_attention(q, kv_scr, sinks, from_prev, prev_bias, seqs):
    gw = Q_PER_KV * HEAD_DIM
    left = lax.broadcasted_iota(jnp.int32, (2 * WINDOW, KV_WIDTH), 1) < HEAD_DIM
    head_of_lane = lax.shift_right_logical(
        lax.broadcasted_iota(jnp.int32, (WINDOW, gw), 1), HEAD_DIM.bit_length() - 1)
    scores, values = [], []
    for n, b in enumerate(seqs):
        kb = kv_scr[b, :, 0:KV_WIDTH]
        vb = kv_scr[b, :, KV_WIDTH:2 * KV_WIDTH]
        kr = pltpu.roll(kb, HEAD_DIM, axis=1)
        vr = pltpu.roll(vb, HEAD_DIM, axis=1)
        for g in range(N_KV_HEADS):
            kk = jnp.where(left, kb, kr) if g == 0 else jnp.where(left, kr, kb)
            vv = jnp.where(left, vb, vr) if g == 0 else jnp.where(left, vr, vb)
            k4 = jnp.concatenate([kk, kk], axis=1).astype(BF16)
            values.append(jnp.concatenate([vv, vv], axis=1).astype(BF16))
            qg = q[n * WINDOW:(n + 1) * WINDOW, g * gw:(g + 1) * gw]
            qst = jnp.concatenate(
                [jnp.where(head_of_lane == r, qg, 0.0) for r in range(Q_PER_KV)], axis=0).astype(BF16)
            scores.append(_dot_nt(qst, k4))
    nblk = len(scores)
    s = jnp.concatenate(scores, axis=0)
    prev = jnp.concatenate([from_prev] * nblk, axis=0)
    s = jnp.where(prev, s[:, :WINDOW] + prev_bias, s[:, WINDOW:])
    sink = jnp.concatenate(
        [jnp.full((WINDOW, WINDOW), sinks[(n % N_KV_HEADS) * Q_PER_KV + r], F32)
         for n in range(nblk) for r in range(Q_PER_KV)], axis=0)
    m = jnp.maximum(jnp.max(s, axis=1, keepdims=True), sink)
    p = jnp.exp(s - m)
    den = _dot(p.astype(BF16), jnp.ones((WINDOW, WINDOW), BF16)) + jnp.exp(sink - m)
    p = p / den
    p = jnp.concatenate([jnp.where(prev, p, 0.0).astype(BF16), jnp.where(prev, 0.0, p).astype(BF16)], axis=1)
    rows = Q_PER_KV * WINDOW
    outs = []
    for b in range(len(seqs)):
        og = []
        for g in range(N_KV_HEADS):
            n = b * N_KV_HEADS + g
            o4 = _dot(p[n * rows:(n + 1) * rows], values[n])
            o = jnp.where(head_of_lane == 0, o4[0:WINDOW], 0.0)
            for r in range(1, Q_PER_KV):
                o = jnp.where(head_of_lane == r, o4[r * WINDOW:(r + 1) * WINDOW], o)
            og.append(o)
        outs.append(jnp.concatenate(og, axis=1))
    return outs


def _proj_attn_kernel(sink_ref, x_ref, g_ref, w_ref, au_ref, kk_ref, vk_ref, kv_scr, *, nb):
    i = pl.program_id(0)
    z = jnp.concatenate(
        [_dot(_rms(x_ref[b:b + 2].reshape(2 * WINDOW, D_MODEL), g_ref[...]).astype(BF16), w_ref[...])
         for b in range(0, nb, 2)], axis=0)
    ucol = ATTN_WIDTH + 2 * KV_WIDTH

    @pl.when(i == 0)
    def _():
        kv_scr[:, 0:WINDOW, :] = jnp.zeros((nb, WINDOW, 2 * KV_WIDTH), F32)

    for b in range(nb):
        kv_scr[b, WINDOW:2 * WINDOW, :] = z[b * WINDOW:(b + 1) * WINDOW, ATTN_WIDTH:ucol]
    q = z[:, :ATTN_WIDTH] * (HEAD_DIM ** -0.5)

    shape = (Q_PER_KV * WINDOW, WINDOW)
    row = jnp.bitwise_and(lax.broadcasted_iota(jnp.int32, shape, 0), WINDOW - 1)
    col = lax.broadcasted_iota(jnp.int32, shape, 1)
    prev_bias = jnp.where(i == 0, -jnp.inf, 0.0).astype(F32)
    sinks = [sink_ref[hh] for hh in range(N_Q_HEADS)]
    outs = _band_attention(q, kv_scr, sinks, col > row, prev_bias, range(nb))
    for b in range(nb):
        zb = z[b * WINDOW:(b + 1) * WINDOW]
        for j in range(ATTN_SLABS):
            au_ref[j, pl.ds(b, WINDOW, stride=nb), :] = outs[b][:, j * LANE:(j + 1) * LANE]
        for j in range(N_STRIP):
            au_ref[ATTN_SLABS + j, pl.ds(b, WINDOW, stride=nb), :] = zb[:, ucol + j * LANE:ucol + (j + 1) * LANE]
        kk_ref[b] = zb[:, ATTN_WIDTH:ATTN_WIDTH + KV_WIDTH]
        vk_ref[b] = zb[:, ATTN_WIDTH + KV_WIDTH:ucol]
    kv_scr[:, 0:WINDOW, :] = kv_scr[:, WINDOW:2 * WINDOW, :]


def _prompt_proj_attn(x, g_pre, w_in, sinks):
    nb, seq, _ = x.shape
    nslab = ATTN_SLABS + N_STRIP
    return pl.pallas_call(
        functools.partial(_proj_attn_kernel, nb=nb),
        grid=(seq // WINDOW,),
        in_specs=[
            pl.BlockSpec(memory_space=pltpu.SMEM),
            pl.BlockSpec((nb, WINDOW, D_MODEL), lambda i: (0, i, 0)),
            _const_spec((1, D_MODEL)),
            _const_spec(w_in.shape),
        ],
        out_specs=[
            pl.BlockSpec((nslab, WINDOW * nb, LANE), lambda i: (0, i, 0)),
            pl.BlockSpec((nb, WINDOW, KV_WIDTH), lambda i: (0, 0, 0)),
            pl.BlockSpec((nb, WINDOW, KV_WIDTH), lambda i: (0, 0, 0)),
        ],
        out_shape=[
            jax.ShapeDtypeStruct((nslab, seq * nb, LANE), F32),
            jax.ShapeDtypeStruct((nb, WINDOW, KV_WIDTH), F32),
            jax.ShapeDtypeStruct((nb, WINDOW, KV_WIDTH), F32),
        ],
        scratch_shapes=[pltpu.VMEM((nb, 2 * WINDOW, 2 * KV_WIDTH), F32)],
        compiler_params=_params(1),
        name="prompt_proj_attn",
    )(sinks, x, g_pre, w_in)


def _s5_prompt_kernel(au_ref, bb_ref, cc_ref, are_ref, aim_ref, d_ref, wglu_ref, wout_ref,
                      m_ref, hfin_ref, hbuf, slab_scr, *, steps, nb):
    i = pl.program_id(0)

    @pl.when(i == 0)
    def _():
        hfin_ref[...] = jnp.zeros(hfin_ref.shape, F32)

    ys = []
    for j in range(N_STRIP):
        uj = au_ref[ATTN_SLABS + j]
        hbuf[...] = _dot(uj.astype(BF16), bb_ref[j])
        are = jnp.broadcast_to(are_ref[j:j + 1, :], (nb, STRIP_STATES))
        aim = jnp.broadcast_to(aim_ref[j:j + 1, :], (nb, STRIP_STATES))

        def step(t, carry):
            hr, hi = carry
            r = pl.multiple_of(t * nb, nb)
            br = hbuf[pl.ds(r, nb), 0:STRIP_STATES]
            bi = hbuf[pl.ds(r, nb), STRIP_STATES:2 * STRIP_STATES]
            nr = are * hr - aim * hi + br
            ni = are * hi + aim * hr + bi
            hbuf[pl.ds(r, nb), 0:STRIP_STATES] = nr
            hbuf[pl.ds(r, nb), STRIP_STATES:2 * STRIP_STATES] = ni
            return nr, ni

        h0 = (hfin_ref[j, :, 0:STRIP_STATES], hfin_ref[j, :, STRIP_STATES:2 * STRIP_STATES])
        hr, hi = lax.fori_loop(0, steps, step, h0, unroll=True)
        hfin_ref[j, :, 0:STRIP_STATES] = hr
        hfin_ref[j, :, STRIP_STATES:2 * STRIP_STATES] = hi
        ys.append(_dot(hbuf[...].astype(BF16), cc_ref[j]) + d_ref[:, j * LANE:(j + 1) * LANE] * uj)
    s = _gelu_glu(jnp.concatenate(ys, axis=1), wglu_ref)
    a = jnp.concatenate([au_ref[j] for j in range(ATTN_SLABS)], axis=1).astype(BF16)
    m = _dot(a, wout_ref[0:ATTN_WIDTH, :]) + _dot(s.astype(BF16), wout_ref[ATTN_WIDTH:, :])
    _store_sequence_major(m, slab_scr, m_ref, nb)


def _prompt_s5(au, tabs, d_skip, w_glu, w_out, *, nb, steps):
    a_re, a_im, bb, cc = tabs
    nslab, rows, _ = au.shape
    blk = steps * nb
    return pl.pallas_call(
        functools.partial(_s5_prompt_kernel, steps=steps, nb=nb),
        grid=(rows // blk,),
        in_specs=[
            pl.BlockSpec((nslab, blk, LANE), lambda i: (0, i, 0)),
            _const_spec(bb.shape), _const_spec(cc.shape),
            _const_spec(a_re.shape), _const_spec(a_im.shape),
            _const_spec((1, SSM_WIDTH)),
            _const_spec(w_glu.shape), _const_spec(w_out.shape),
        ],
        out_specs=[
            pl.BlockSpec((nb, steps, D_MODEL), lambda i: (0, i, 0)),
            pl.BlockSpec((N_STRIP, nb, 2 * STRIP_STATES), lambda i: (0, 0, 0)),
        ],
        out_shape=[
            jax.ShapeDtypeStruct((nb, rows // nb, D_MODEL), BF16),
            jax.ShapeDtypeStruct((N_STRIP, nb, 2 * STRIP_STATES), F32),
        ],
        scratch_shapes=[pltpu.VMEM((blk, 2 * STRIP_STATES), F32),
                        pltpu.VMEM((D_SLABS, blk, LANE), F32)],
        compiler_params=_params(1),
        name="prompt_s5",
    )(au, bb, cc, a_re, a_im, d_skip, w_glu, w_out)


def _post_ffn_kernel(x_ref, m_ref, gpm_ref, gpf_ref, gof_ref, wg_ref, wu_ref, wd_ref, o_ref):
    half = x_ref.shape[0] // FFN_SPLIT
    for r in range(FFN_SPLIT):
        rows = slice(r * half, (r + 1) * half)
        x1 = x_ref[rows, :] + _rms(m_ref[rows, :].astype(F32), gpm_ref[...])
        h = _rms(x1, gpf_ref[...]).astype(BF16)
        gate = _dot(h, wg_ref[...])
        up = _dot(h, wu_ref[...])
        act = (gate * jax.nn.sigmoid(gate) * up).astype(BF16)
        f = _dot(act, wd_ref[...])
        o_ref[rows, :] = x1 + _rms(f, gof_ref[...])


def _post_ffn(x, m, gains, ffn_w, *, rows, name):
    gpm, gpf, gof = gains
    wg, wu, wd = ffn_w
    row_spec = pl.BlockSpec((rows, D_MODEL), lambda i: (i, 0))
    return pl.pallas_call(
        _post_ffn_kernel,
        grid=(x.shape[0] // rows,),
        in_specs=[row_spec, row_spec,
                  _const_spec((1, D_MODEL)), _const_spec((1, D_MODEL)), _const_spec((1, D_MODEL)),
                  _const_spec(wg.shape), _const_spec(wu.shape), _const_spec(wd.shape)],
        out_specs=row_spec,
        out_shape=jax.ShapeDtypeStruct(x.shape, F32),
        compiler_params=_params(1),
        name=name,
    )(x, m, gpm, gpf, gof, wg, wu, wd)


CONV_HALO = 32
CONV_ROWS = 64


def _layer_norm_silu(c, lng_ref, lnb_ref):
    mu = jnp.mean(c, axis=-1, keepdims=True)
    cc = c - mu
    var = jnp.mean(cc * cc, axis=-1, keepdims=True)
    cn = cc * lax.rsqrt(var + LN_EPS) * lng_ref[...] + lnb_ref[...]
    return cn * jax.nn.sigmoid(cn)


def _conv_prompt_kernel(x_ref, g_ref, w1_ref, b1_ref, wdw_ref, bdw_ref, lng_ref, lnb_ref, w2_ref, b2_ref,
                        m_ref, tail_ref, ge_scr, go_scr, last_scr, c_scr, slab_scr, *, blk, nb):
    i = pl.program_id(0)
    halo = CONV_HALO * nb
    tail = CONV_TAIL * nb
    span = CONV_ROWS + tail

    @pl.when(i == 0)
    def _():
        ge_scr[0:halo, :] = jnp.zeros((halo, D_MODEL), BF16)
        go_scr[0:halo, :] = jnp.zeros((halo, D_MODEL), BF16)
        last_scr[...] = jnp.zeros(last_scr.shape, F32)

    x = _load_time_major(x_ref, slab_scr, nb)
    h = _rms(x, g_ref[...]).astype(BF16)
    a = _dot(h, w1_ref[...]) + b1_ref[...]
    g = a[:, :D_MODEL] * jax.nn.sigmoid(a[:, D_MODEL:])
    tail_ref[...] = g[blk - tail:, :]
    ge_scr[halo:halo + blk, :] = g.astype(BF16)
    go_scr[halo:halo + blk, :] = jnp.concatenate([last_scr[...], g[:blk - nb]], axis=0).astype(BF16)
    last_scr[...] = g[blk - nb:, :]

    def chunk(ci, carry):
        base = pl.multiple_of(ci * CONV_ROWS, CONV_ROWS)
        start = base + halo - tail
        for l in range(D_SLABS):
            lanes = slice(l * LANE, (l + 1) * LANE)
            xe = ge_scr[pl.ds(start, span), lanes]
            xo = go_scr[pl.ds(start, span), lanes]
            acc = jnp.zeros((CONV_ROWS, LANE), F32)
            for j in range(CONV_WIDTH):
                src, off = (xe, j * nb) if j % 2 == 0 else (xo, (j + 1) * nb)
                wj = wdw_ref[j * BF16_ROWS:(j + 1) * BF16_ROWS, lanes]
                wj = jnp.concatenate([wj] * (CONV_ROWS // BF16_ROWS), axis=0)
                acc = acc + src[off:off + CONV_ROWS].astype(F32) * wj.astype(F32)
            c_scr[pl.ds(base, CONV_ROWS), lanes] = acc + bdw_ref[:, lanes]
        return carry

    lax.fori_loop(0, blk // CONV_ROWS, chunk, 0)
    act = _layer_norm_silu(c_scr[...], lng_ref, lnb_ref).astype(BF16)
    _store_sequence_major(_dot(act, w2_ref[...]) + b2_ref[...], slab_scr, m_ref, nb)
    ge_scr[0:halo, :] = ge_scr[blk:blk + halo, :]
    go_scr[0:halo, :] = go_scr[blk:blk + halo, :]


def _prompt_conv(x, g_pre, cw, *, steps):
    w1, b1, wdw8, bdw, lng, lnb, w2, b2 = cw
    nb, seq, _ = x.shape
    blk = steps * nb
    assert steps >= CONV_HALO and blk % CONV_ROWS == 0 and 2 * nb == BF16_ROWS
    vec = lambda n: _const_spec((1, n))
    seq_spec = pl.BlockSpec((nb, steps, D_MODEL), lambda i: (0, i, 0))
    return pl.pallas_call(
        functools.partial(_conv_prompt_kernel, blk=blk, nb=nb),
        grid=(seq // steps,),
        in_specs=[
            seq_spec,
            vec(D_MODEL), _const_spec(w1.shape), vec(2 * D_MODEL),
            _const_spec(wdw8.shape), vec(D_MODEL), vec(D_MODEL), vec(D_MODEL),
            _const_spec(w2.shape), vec(D_MODEL),
        ],
        out_specs=[
            seq_spec,
            pl.BlockSpec((CONV_TAIL * nb, D_MODEL), lambda i: (0, 0)),
        ],
        out_shape=[
            jax.ShapeDtypeStruct((nb, seq, D_MODEL), BF16),
            jax.ShapeDtypeStruct((CONV_TAIL * nb, D_MODEL), F32),
        ],
        scratch_shapes=[pltpu.VMEM((blk + CONV_HALO * nb, D_MODEL), BF16),
                        pltpu.VMEM((blk + CONV_HALO * nb, D_MODEL), BF16),
                        pltpu.VMEM((nb, D_MODEL), F32),
                        pltpu.VMEM((blk, D_MODEL), F32),
                        pltpu.VMEM((D_SLABS, blk, LANE), F32)],
        compiler_params=_params(1),
        name="prompt_conv",
    )(x, g_pre, w1, b1, wdw8, bdw, lng, lnb, w2, b2)


def _sample_l0_kernel(sink_ref, x_ref, g_ref, w_ref, ck_ref, cv_ref, sre_ref, sim_ref,
                      bb_ref, cc_ref, are_ref, aim_ref, d_ref, wglu_ref, wout_ref,
                      m_ref, kn_ref, vn_ref, ore_ref, oim_ref,
                      q_scr, k_scr, v_scr, a_scr, u_scr, y_scr, *, nseq, ntok):
    qw = N_Q_HEADS * LANE
    h = _rms(x_ref[...], g_ref[...]).astype(BF16)
    z = _dot(h, w_ref[...])
    q_scr[...] = z[:, :qw] * (HEAD_DIM ** -0.5)
    kn = z[:, qw:qw + KV_WIDTH]
    vn = z[:, qw + KV_WIDTH:qw + 2 * KV_WIDTH]
    k_scr[...] = kn
    v_scr[...] = vn
    kn_ref[...] = kn
    vn_ref[...] = vn
    for j in range(N_STRIP):
        u_scr[j] = z[:, qw + 2 * KV_WIDTH + j * LANE:qw + 2 * KV_WIDTH + (j + 1) * LANE]
    y_scr[...] = jnp.zeros(y_scr.shape, F32)

    for j in range(N_STRIP):
        hr = sre_ref[:, j * STRIP_STATES:(j + 1) * STRIP_STATES]
        hi = sim_ref[:, j * STRIP_STATES:(j + 1) * STRIP_STATES]
        are = are_ref[j:j + 1, :]
        aim = aim_ref[j:j + 1, :]
        dj = d_ref[:, j * LANE:(j + 1) * LANE]
        for t in range(ntok):
            ut = u_scr[j, pl.ds(t, nseq, stride=SAMPLE_T), :]
            bu = _dot(ut.astype(BF16), bb_ref[j])
            hr, hi = (are * hr - aim * hi + bu[:, :STRIP_STATES],
                      are * hi + aim * hr + bu[:, STRIP_STATES:])
            hcat = jnp.concatenate([hr, hi], axis=1).astype(BF16)
            y_scr[j, pl.ds(t, nseq, stride=SAMPLE_T), :] = _dot(hcat, cc_ref[j]) + dj * ut
        ore_ref[:, j * STRIP_STATES:(j + 1) * STRIP_STATES] = hr
        oim_ref[:, j * STRIP_STATES:(j + 1) * STRIP_STATES] = hi
    s = _gelu_glu(jnp.concatenate([y_scr[j] for j in range(N_STRIP)], axis=1), wglu_ref)

    nkeys = WINDOW + SAMPLE_T
    shape = (N_Q_HEADS * SAMPLE_T, nkeys)
    tq = jnp.bitwise_and(lax.broadcasted_iota(jnp.int32, shape, 0), SAMPLE_T - 1)
    col = lax.broadcasted_iota(jnp.int32, shape, 1)
    ninf = jnp.float32(-jnp.inf)
    bias = jnp.where(col < WINDOW, jnp.where(col > tq, 0.0, ninf),
                     jnp.where(col - WINDOW <= tq, 0.0, ninf))
    sink = jnp.concatenate(
        [jnp.full((SAMPLE_T, 1), sink_ref[hh], F32) for hh in range(N_Q_HEADS)], axis=0)
    left = lax.broadcasted_iota(jnp.int32, (SAMPLE_T, LANE), 1) < HEAD_DIM

    def seq_body(n, carry):
        r = pl.multiple_of(n * SAMPLE_T, SAMPLE_T)
        qs = jnp.concatenate(
            [q_scr[pl.ds(r, SAMPLE_T), hh * LANE:(hh + 1) * LANE] for hh in range(N_Q_HEADS)], axis=0)
        kall = jnp.concatenate([ck_ref[n], k_scr[pl.ds(r, SAMPLE_T), :]], axis=0)
        vall = jnp.concatenate([cv_ref[n], v_scr[pl.ds(r, SAMPLE_T), :]], axis=0)
        sc = _dot_nt(qs.astype(BF16), kall.astype(BF16)) + bias
        mx = jnp.maximum(jnp.max(sc, axis=1, keepdims=True), sink)
        p = jnp.exp(sc - mx)
        den = jnp.sum(p, axis=1, keepdims=True) + jnp.exp(sink - mx)
        p = (p / den).astype(BF16)
        o1 = _dot(p, vall.astype(BF16))
        o2 = _dot(p, pltpu.roll(vall, HEAD_DIM, axis=1).astype(BF16))
        slabs = []
        for hp in range(N_Q_HEADS // 2):
            lo, ro = (o1, o2) if hp < Q_PER_KV // 2 else (o2, o1)
            lrow = 2 * hp * SAMPLE_T
            slabs.append(jnp.where(left, lo[lrow:lrow + SAMPLE_T], ro[lrow + SAMPLE_T:lrow + 2 * SAMPLE_T]))
        a_scr[pl.ds(r, SAMPLE_T), :] = jnp.concatenate(slabs, axis=1)
        return carry

    lax.fori_loop(0, nseq, seq_body, 0)
    m = (_dot(a_scr[...].astype(BF16), wout_ref[0:ATTN_WIDTH, :])
         + _dot(s.astype(BF16), wout_ref[ATTN_WIDTH:, :]))
    m_ref[...] = m.astype(m_ref.dtype)


def _sample_l0(x, g_pre, w_in_s, sinks, ck, cv, s_re, s_im, tabs, d_skip, w_glu, w_out, *, nseq, ntok):
    a_re, a_im, bb, cc = tabs
    rows_all = x.shape[0]
    rows = nseq * SAMPLE_T
    nstate = SSM_GROUPS * SSM_STATE
    row_spec = lambda w: pl.BlockSpec((rows, w), lambda i: (i, 0))
    return pl.pallas_call(
        functools.partial(_sample_l0_kernel, nseq=nseq, ntok=ntok),
        grid=(rows_all // rows,),
        in_specs=[
            pl.BlockSpec(memory_space=pltpu.SMEM),
            row_spec(D_MODEL), _const_spec((1, D_MODEL)), _const_spec(w_in_s.shape),
            pl.BlockSpec((nseq, WINDOW, KV_WIDTH), lambda i: (i, 0, 0)),
            pl.BlockSpec((nseq, WINDOW, KV_WIDTH), lambda i: (i, 0, 0)),
            pl.BlockSpec((nseq, nstate), lambda i: (i, 0)),
            pl.BlockSpec((nseq, nstate), lambda i: (i, 0)),
            _const_spec(bb.shape), _const_spec(cc.shape),
            _const_spec(a_re.shape), _const_spec(a_im.shape),
            _const_spec((1, SSM_WIDTH)),
            _const_spec(w_glu.shape), _const_spec(w_out.shape),
        ],
        out_specs=[
            row_spec(D_MODEL), row_spec(KV_WIDTH), row_spec(KV_WIDTH),
            pl.BlockSpec((nseq, nstate), lambda i: (i, 0)),
            pl.BlockSpec((nseq, nstate), lambda i: (i, 0)),
        ],
        out_shape=[
            jax.ShapeDtypeStruct((rows_all, D_MODEL), BF16),
            jax.ShapeDtypeStruct((rows_all, KV_WIDTH), F32),
            jax.ShapeDtypeStruct((rows_all, KV_WIDTH), F32),
            jax.ShapeDtypeStruct((rows_all // SAMPLE_T, nstate), F32),
            jax.ShapeDtypeStruct((rows_all // SAMPLE_T, nstate), F32),
        ],
        scratch_shapes=[
            pltpu.VMEM((rows, N_Q_HEADS * LANE), F32),
            pltpu.VMEM((rows, KV_WIDTH), F32),
            pltpu.VMEM((rows, KV_WIDTH), F32),
            pltpu.VMEM((rows, ATTN_WIDTH), F32),
            pltpu.VMEM((N_STRIP, rows, LANE), F32),
            pltpu.VMEM((N_STRIP, rows, LANE), F32),
        ],
        compiler_params=_params(1),
        name="sample_l0",
    )(sinks, x, g_pre, w_in_s, ck, cv, s_re, s_im, bb, cc, a_re, a_im, d_skip, w_glu, w_out)


def _sample_l1_kernel(x_ref, g_ref, w1_ref, b1_ref, st_ref, wst_ref, wnew_ref, bdw_ref, lng_ref, lnb_ref,
                      w2_ref, b2_ref, m_ref, gout_ref, c_scr, *, nseq, ntok):
    h = _rms(x_ref[...], g_ref[...]).astype(BF16)
    a = _dot(h, w1_ref[...]) + b1_ref[...]
    gg = a[:, :D_MODEL] * jax.nn.sigmoid(a[:, D_MODEL:])
    gout_ref[...] = gg
    g3 = gg.reshape(nseq, SAMPLE_T, D_MODEL)
    st = st_ref[...]
    c_scr[...] = jnp.zeros(c_scr.shape, F32)
    for t in range(ntok):
        ct = (jnp.sum(st * wst_ref[t][None], axis=1) + jnp.sum(g3 * wnew_ref[t][None], axis=1)
              + bdw_ref[...])
        for j in range(D_SLABS):
            c_scr[j, pl.ds(t, nseq, stride=SAMPLE_T), :] = ct[:, j * LANE:(j + 1) * LANE]
    c = jnp.concatenate([c_scr[j] for j in range(D_SLABS)], axis=1)
    act = _layer_norm_silu(c, lng_ref, lnb_ref).astype(BF16)
    m_ref[...] = (_dot(act, w2_ref[...]) + b2_ref[...]).astype(m_ref.dtype)


def _sample_l1(x, g_pre, cw, st, wst, wnew, *, nseq, ntok):
    w1, b1, _, bdw, lng, lnb, w2, b2 = cw
    rows_all = x.shape[0]
    rows = nseq * SAMPLE_T
    vec = lambda n: _const_spec((1, n))
    row_spec = pl.BlockSpec((rows, D_MODEL), lambda i: (i, 0))
    return pl.pallas_call(
        functools.partial(_sample_l1_kernel, nseq=nseq, ntok=ntok),
        grid=(rows_all // rows,),
        in_specs=[
            row_spec, vec(D_MODEL), _const_spec(w1.shape), vec(2 * D_MODEL),
            pl.BlockSpec((nseq, CONV_TAIL, D_MODEL), lambda i: (i, 0, 0)),
            _const_spec(wst.shape), _const_spec(wnew.shape),
            vec(D_MODEL), vec(D_MODEL), vec(D_MODEL), _const_spec(w2.shape), vec(D_MODEL),
        ],
        out_specs=[row_spec, row_spec],
        out_shape=[
            jax.ShapeDtypeStruct((rows_all, D_MODEL), BF16),
            jax.ShapeDtypeStruct((rows_all, D_MODEL), F32),
        ],
        scratch_shapes=[pltpu.VMEM((D_SLABS, rows, LANE), F32)],
        compiler_params=_params(1),
        name="sample_l1",
    )(x, g_pre, w1, b1, st, wst, wnew, bdw, lng, lnb, w2, b2)


PROMPT_FFN_ROWS = 512
PROMPT_STEPS = 64
SAMPLE_SEQS = 32
SAMPLE_FFN_ROWS = 512


def kernel(x_prompt, x_sample, cache_k, cache_v, state_s5_re, state_s5_im, state_conv, norm_pre_mix, norm_post_mix, norm_pre_ffn, norm_post_ffn, w_in_ab, attn_sinks, s5_lambda_re, s5_lambda_im, s5_log_step, s5_b_re, s5_b_im, s5_c_re, s5_c_im, s5_d, w_glu, w_out_ab, w_pw1, b_pw1, w_dw, b_dw, conv_ln_g, conv_ln_b, w_pw2, b_pw2, w_ffn_gate, w_ffn_up, w_ffn_down):
    nb, seq, _ = x_prompt.shape
    ns, ntok, _ = x_sample.shape
    assert nb == SUBLANE and ntok <= SAMPLE_T and seq % WINDOW == 0 and seq % PROMPT_STEPS == 0
    row = lambda v: v.reshape(1, -1)
    gains = lambda layer: tuple(row(n[layer]) for n in (norm_post_mix, norm_pre_ffn, norm_post_ffn))

    w_in = w_in_ab[0].astype(BF16)
    wq = w_in_ab[0][:, :ATTN_WIDTH].reshape(D_MODEL, N_Q_HEADS, HEAD_DIM)
    lane_kv = (jnp.arange(LANE) // HEAD_DIM)[None, :]
    head_kv = (jnp.arange(N_Q_HEADS) // Q_PER_KV)[:, None]
    wq_slab = jnp.where((lane_kv == head_kv)[None], jnp.concatenate([wq, wq], axis=2), 0.0)
    w_in_s = jnp.concatenate(
        [wq_slab.reshape(D_MODEL, N_Q_HEADS * LANE), w_in_ab[0][:, ATTN_WIDTH:]], axis=1).astype(BF16)
    w_glu_b = w_glu[0].astype(BF16)
    w_out = w_out_ab[0].astype(BF16)
    ffn_w = [(w_ffn_gate[l].astype(BF16), w_ffn_up[l].astype(BF16), w_ffn_down[l].astype(BF16))
             for l in range(2)]
    sinks = attn_sinks[0]
    tabs = _s5_tables(s5_lambda_re[0], s5_lambda_im[0], s5_log_step[0], s5_b_re[0], s5_b_im[0],
                      s5_c_re[0], s5_c_im[0])
    d_skip = row(s5_d[0])
    wdw = w_dw[0]
    conv_w = (w_pw1[0].astype(BF16), row(b_pw1[0]), jnp.repeat(wdw.astype(BF16), BF16_ROWS, axis=0), row(b_dw[0]),
              row(conv_ln_g[0]), row(conv_ln_b[0]), w_pw2[0].astype(BF16), row(b_pw2[0]))
    tt = jnp.arange(ntok)[:, None]
    tap_st = jnp.arange(CONV_TAIL)[None, :] - tt
    wst = jnp.where((tap_st >= 0)[..., None], wdw[jnp.clip(tap_st, 0, CONV_WIDTH - 1)], 0.0)
    tap_new = CONV_TAIL + jnp.arange(SAMPLE_T)[None, :] - tt
    wnew = jnp.where((tap_new <= CONV_TAIL)[..., None], wdw[jnp.clip(tap_new, 0, CONV_WIDTH - 1)], 0.0)

    flat = lambda v: v.reshape(nb * seq, D_MODEL)
    au_p, kk_p, vk_p = _prompt_proj_attn(x_prompt, row(norm_pre_mix[0]), w_in, sinks)
    m0_p, hfin_p = _prompt_s5(au_p, tabs, d_skip, w_glu_b, w_out, nb=nb, steps=PROMPT_STEPS)
    x1_p = _post_ffn(flat(x_prompt), flat(m0_p), gains(0), ffn_w[0], rows=PROMPT_FFN_ROWS, name="prompt_ffn0")
    m1_p, tail_p = _prompt_conv(x1_p.reshape(nb, seq, D_MODEL), row(norm_pre_mix[1]), conv_w, steps=PROMPT_STEPS)
    y_prompt = _post_ffn(x1_p, flat(m1_p), gains(1), ffn_w[1], rows=PROMPT_FFN_ROWS, name="prompt_ffn1")
    y_prompt = y_prompt.reshape(nb, seq, D_MODEL)

    xs = jnp.pad(x_sample, ((0, 0), (0, SAMPLE_T - ntok), (0, 0))).reshape(ns * SAMPLE_T, D_MODEL)
    nstate = SSM_GROUPS * SSM_STATE
    m0_s, kn_s, vn_s, sre_s, sim_s = _sample_l0(
        xs, row(norm_pre_mix[0]), w_in_s, sinks,
        cache_k[0].reshape(ns, WINDOW, KV_WIDTH), cache_v[0].reshape(ns, WINDOW, KV_WIDTH),
        state_s5_re[0].reshape(ns, nstate), state_s5_im[0].reshape(ns, nstate),
        tabs, d_skip, w_glu_b, w_out, nseq=SAMPLE_SEQS, ntok=ntok)
    x1_s = _post_ffn(xs, m0_s, gains(0), ffn_w[0], rows=SAMPLE_FFN_ROWS, name="sample_ffn0")
    m1_s, g_s = _sample_l1(x1_s, row(norm_pre_mix[1]), conv_w, state_conv[0], wst, wnew,
                           nseq=SAMPLE_SEQS, ntok=ntok)
    y_s = _post_ffn(x1_s, m1_s, gains(1), ffn_w[1], rows=SAMPLE_FFN_ROWS, name="sample_ffn1")

    y_sample = y_s.reshape(ns, SAMPLE_T, D_MODEL)[:, :ntok]
    k_prompt = kk_p.reshape(1, nb, WINDOW, N_KV_HEADS, HEAD_DIM)
    v_prompt = vk_p.reshape(1, nb, WINDOW, N_KV_HEADS, HEAD_DIM)
    hfin = jnp.swapaxes(hfin_p, 0, 1)
    s5_re_prompt = hfin[:, :, :STRIP_STATES].reshape(1, nb, SSM_GROUPS, SSM_STATE)
    s5_im_prompt = hfin[:, :, STRIP_STATES:].reshape(1, nb, SSM_GROUPS, SSM_STATE)
    conv_prompt = jnp.swapaxes(tail_p.reshape(CONV_TAIL, nb, D_MODEL), 0, 1)[None]
    k_new = kn_s.reshape(ns, SAMPLE_T, N_KV_HEADS, HEAD_DIM)[:, :ntok]
    v_new = vn_s.reshape(ns, SAMPLE_T, N_KV_HEADS, HEAD_DIM)[:, :ntok]
    k_sample = jnp.concatenate([cache_k[0][:, ntok:], k_new], axis=1)[None]
    v_sample = jnp.concatenate([cache_v[0][:, ntok:], v_new], axis=1)[None]
    s5_re_sample = sre_s.reshape(1, ns, SSM_GROUPS, SSM_STATE)
    s5_im_sample = sim_s.reshape(1, ns, SSM_GROUPS, SSM_STATE)
    g_new = g_s.reshape(ns, SAMPLE_T, D_MODEL)[:, :ntok]
    conv_sample = jnp.concatenate([state_conv[0][:, ntok:], g_new], axis=1)[None]
    return (y_prompt, y_sample, k_prompt, v_prompt, s5_re_prompt, s5_im_prompt, conv_prompt,
            k_sample, v_sample, s5_re_sample, s5_im_sample, conv_sample)
```

```python
import functools

import jax
import jax.numpy as jnp
from jax import lax
from jax.experimental import pallas as pl
from jax.experimental.pallas import tpu as pltpu

F32 = jnp.float32
BF16 = jnp.bfloat16

LANE = 128
SUBLANE = 8
VMEM_LIMIT_BYTES = 56 * 1024 * 1024

D_MODEL = 1024
HEAD_DIM = 64
N_Q_HEADS = 8
N_KV_HEADS = 2
Q_PER_KV = N_Q_HEADS // N_KV_HEADS
WINDOW = 128
ATTN_WIDTH = N_Q_HEADS * HEAD_DIM
KV_WIDTH = N_KV_HEADS * HEAD_DIM
SSM_WIDTH = 512
SSM_GROUP = 16
SSM_GROUPS = 32
SSM_STATE = 64
D_FF = 2816
CONV_WIDTH = 31
CONV_TAIL = CONV_WIDTH - 1
RMS_EPS = 1e-6
LN_EPS = 1e-5

N_STRIP = SSM_WIDTH // LANE
STRIP_STATES = (SSM_GROUPS // N_STRIP) * SSM_STATE
D_SLABS = D_MODEL // LANE
ATTN_SLABS = ATTN_WIDTH // LANE
FFN_SPLIT = 2

SAMPLE_T = SUBLANE


def _rms(x, g):
    return x * lax.rsqrt(jnp.mean(x * x, axis=-1, keepdims=True) + RMS_EPS) * g


def _dot(a, b):
    return jnp.dot(a, b, preferred_element_type=F32)


def _dot_nt(a, b):
    return lax.dot_general(a, b, (((1,), (1,)), ((), ())), preferred_element_type=F32)


def _const_spec(shape):
    nd = len(shape)
    return pl.BlockSpec(shape, lambda *_: (0,) * nd, pipeline_mode=pl.Buffered(1))


def _params(n_axes):
    return pltpu.CompilerParams(
        dimension_semantics=("arbitrary",) * n_axes, vmem_limit_bytes=VMEM_LIMIT_BYTES)


def _load_time_major(x_ref, slab_scr, nb):
    steps = x_ref.shape[1]
    nslab = x_ref.shape[2] // LANE
    for b in range(nb):
        for j in range(nslab):
            slab_scr[j, pl.ds(b, steps, stride=nb), :] = x_ref[b, :, j * LANE:(j + 1) * LANE]
    return jnp.concatenate([slab_scr[j] for j in range(nslab)], axis=1)


def _store_sequence_major(val, slab_scr, out_ref, nb):
    steps = out_ref.shape[1]
    nslab = out_ref.shape[2] // LANE
    for j in range(nslab):
        slab_scr[j] = val[:, j * LANE:(j + 1) * LANE]
    for b in range(nb):
        for j in range(nslab):
            out_ref[b, :, j * LANE:(j + 1) * LANE] = (
                slab_scr[j, pl.ds(b, steps, stride=nb), :].astype(out_ref.dtype))


def _s5_disc_kernel(lre_ref, lim_ref, ls_ref, bre_ref, bim_ref, are_ref, aim_ref, bbre_ref, bbim_ref):
    lam_re = lre_ref[...]
    lam_im = lim_ref[...]
    dt = jnp.exp(ls_ref[...])
    mag = jnp.exp(lam_re * dt)
    lb_re = mag * jnp.cos(lam_im * dt)
    lb_im = mag * jnp.sin(lam_im * dt)
    den = lam_re * lam_re + lam_im * lam_im
    coef_re = ((lb_re - 1.0) * lam_re + lb_im * lam_im) / den
    coef_im = (lb_im * lam_re - (lb_re - 1.0) * lam_im) / den
    b_re = bre_ref[...]
    b_im = bim_ref[...]
    are_ref[...] = lb_re
    aim_ref[...] = lb_im
    bbre_ref[...] = coef_re * b_re - coef_im * b_im
    bbim_ref[...] = coef_re * b_im + coef_im * b_re


def _s5_tables(lam_re, lam_im, log_step, b_re, b_im, c_re, c_im):
    rep = lambda v: jnp.repeat(v, SSM_GROUP, axis=0)
    rows = SSM_GROUPS * SSM_GROUP
    bt = lambda b: jnp.swapaxes(b, 1, 2).reshape(rows, SSM_STATE)
    ls = jnp.broadcast_to(log_step[:, None], (SSM_GROUPS, SSM_STATE))
    shp = jax.ShapeDtypeStruct((rows, SSM_STATE), F32)
    a_re, a_im, bb_re, bb_im = pl.pallas_call(
        _s5_disc_kernel, out_shape=(shp, shp, shp, shp), name="s5_disc",
    )(rep(lam_re), rep(lam_im), rep(ls), bt(b_re), bt(b_im))
    a_re = a_re[::SSM_GROUP].reshape(N_STRIP, STRIP_STATES)
    a_im = a_im[::SSM_GROUP].reshape(N_STRIP, STRIP_STATES)
    gl = SSM_GROUPS // N_STRIP
    eye = jnp.eye(gl, dtype=F32).reshape(1, gl, 1, gl, 1)

    def in_blockdiag(m):
        m = m.reshape(N_STRIP, gl, SSM_GROUP, 1, SSM_STATE)
        return (m * eye).reshape(N_STRIP, LANE, STRIP_STATES)

    def out_blockdiag(m):
        m = jnp.swapaxes(m.reshape(N_STRIP, gl, SSM_GROUP, SSM_STATE), 2, 3)
        return (m.reshape(N_STRIP, gl, SSM_STATE, 1, SSM_GROUP) * eye).reshape(N_STRIP, STRIP_STATES, LANE)

    bb = jnp.concatenate([in_blockdiag(bb_re), in_blockdiag(bb_im)], axis=2).astype(BF16)
    cc = jnp.concatenate([out_blockdiag(c_re), -out_blockdiag(c_im)], axis=1).astype(BF16)
    return a_re, a_im, bb, cc


def _gelu_glu(y, wglu_ref):
    g = jax.nn.gelu(y)
    return g * jax.nn.sigmoid(_dot(g.astype(BF16), wglu_ref[...]))


def _band_attention(q, kv_scr, sinks, from_prev, prev_bias, seqs):
    gw = Q_PER_KV * HEAD_DIM
    left = lax.broadcasted_iota(jnp.int32, (2 * WINDOW, KV_WIDTH), 1) < HEAD_DIM
    head_of_lane = lax.shift_right_logical(
        lax.broadcasted_iota(jnp.int32, (WINDOW, gw), 1), HEAD_DIM.bit_length() - 1)
    scores, values = [], []
    for n, b in enumerate(seqs):
        kb = kv_scr[b, :, 0:KV_WIDTH]
        vb = kv_scr[b, :, KV_WIDTH:2 * KV_WIDTH]
        kr = pltpu.roll(kb, HEAD_DIM, axis=1)
        vr = pltpu.roll(vb, HEAD_DIM, axis=1)
        for g in range(N_KV_HEADS):
            kk = jnp.where(left, kb, kr) if g == 0 else jnp.where(left, kr, kb)
            vv = jnp.where(left, vb, vr) if g == 0 else jnp.where(left, vr, vb)
            k4 = jnp.concatenate([kk, kk], axis=1).astype(BF16)
            values.append(jnp.concatenate([vv, vv], axis=1).astype(BF16))
            qg = q[n * WINDOW:(n + 1) * WINDOW, g * gw:(g + 1) * gw]
            qst = jnp.concatenate(
                [jnp.where(head_of_lane == r, qg, 0.0) for r in range(Q_PER_KV)], axis=0).astype(BF16)
            scores.append(_dot_nt(qst, k4))
    nblk = len(scores)
    s = jnp.concatenate(scores, axis=0)
    prev = jnp.concatenate([from_prev] * nblk, axis=0)
    s = jnp.where(prev, s[:, :WINDOW] + prev_bias, s[:, WINDOW:])
    sink = jnp.concatenate(
        [jnp.full((WINDOW, WINDOW), sinks[(n % N_KV_HEADS) * Q_PER_KV + r], F32)
         for n in range(nblk) for r in range(Q_PER_KV)], axis=0)
    m = jnp.maximum(jnp.max(s, axis=1, keepdims=True), sink)
    p = jnp.exp(s - m)
    den = _dot(p.astype(BF16), jnp.ones((WINDOW, WINDOW), BF16)) + jnp.exp(sink - m)
    p = p / den
    p = jnp.concatenate([jnp.where(prev, p, 0.0).astype(BF16), jnp.where(prev, 0.0, p).astype(BF16)], axis=1)
    rows = Q_PER_KV * WINDOW
    outs = []
    for b in range(len(seqs)):
        og = []
        for g in range(N_KV_HEADS):
            n = b * N_KV_HEADS + g
            o4 = _dot(p[n * rows:(n + 1) * rows], values[n])
            o = jnp.where(head_of_lane == 0, o4[0:WINDOW], 0.0)
            for r in range(1, Q_PER_KV):
                o = jnp.where(head_of_lane == r, o4[r * WINDOW:(r + 1) * WINDOW], o)
            og.append(o)
        outs.append(jnp.concatenate(og, axis=1))
    return outs


def _proj_attn_kernel(sink_ref, x_ref, g_ref, w_ref, au_ref, kk_ref, vk_ref, kv_scr, *, nb):
    i = pl.program_id(0)
    z = jnp.concatenate(
        [_dot(_rms(x_ref[b:b + 2].reshape(2 * WINDOW, D_MODEL), g_ref[...]).astype(BF16), w_ref[...])
         for b in range(0, nb, 2)], axis=0)
    ucol = ATTN_WIDTH + 2 * KV_WIDTH

    @pl.when(i == 0)
    def _():
        kv_scr[:, 0:WINDOW, :] = jnp.zeros((nb, WINDOW, 2 * KV_WIDTH), F32)

    for b in range(nb):
        kv_scr[b, WINDOW:2 * WINDOW, :] = z[b * WINDOW:(b + 1) * WINDOW, ATTN_WIDTH:ucol]
    q = z[:, :ATTN_WIDTH] * (HEAD_DIM ** -0.5)

    shape = (Q_PER_KV * WINDOW, WINDOW)
    row = jnp.bitwise_and(lax.broadcasted_iota(jnp.int32, shape, 0), WINDOW - 1)
    col = lax.broadcasted_iota(jnp.int32, shape, 1)
    prev_bias = jnp.where(i == 0, -jnp.inf, 0.0).astype(F32)
    sinks = [sink_ref[hh] for hh in range(N_Q_HEADS)]
    outs = _band_attention(q, kv_scr, sinks, col > row, prev_bias, range(nb))
    for b in range(nb):
        zb = z[b * WINDOW:(b + 1) * WINDOW]
        for j in range(ATTN_SLABS):
            au_ref[j, pl.ds(b, WINDOW, stride=nb), :] = outs[b][:, j * LANE:(j + 1) * LANE]
        for j in range(N_STRIP):
            au_ref[ATTN_SLABS + j, pl.ds(b, WINDOW, stride=nb), :] = zb[:, ucol + j * LANE:ucol + (j + 1) * LANE]
        kk_ref[b] = zb[:, ATTN_WIDTH:ATTN_WIDTH + KV_WIDTH]
        vk_ref[b] = zb[:, ATTN_WIDTH + KV_WIDTH:ucol]
    kv_scr[:, 0:WINDOW, :] = kv_scr[:, WINDOW:2 * WINDOW, :]


def _prompt_proj_attn(x, g_pre, w_in, sinks):
    nb, seq, _ = x.shape
    nslab = ATTN_SLABS + N_STRIP
    return pl.pallas_call(
        functools.partial(_proj_attn_kernel, nb=nb),
        grid=(seq // WINDOW,),
        in_specs=[
            pl.BlockSpec(memory_space=pltpu.SMEM),
            pl.BlockSpec((nb, WINDOW, D_MODEL), lambda i: (0, i, 0)),
            _const_spec((1, D_MODEL)),
            _const_spec(w_in.shape),
        ],
        out_specs=[
            pl.BlockSpec((nslab, WINDOW * nb, LANE), lambda i: (0, i, 0)),
            pl.BlockSpec((nb, WINDOW, KV_WIDTH), lambda i: (0, 0, 0)),
            pl.BlockSpec((nb, WINDOW, KV_WIDTH), lambda i: (0, 0, 0)),
        ],
        out_shape=[
            jax.ShapeDtypeStruct((nslab, seq * nb, LANE), F32),
            jax.ShapeDtypeStruct((nb, WINDOW, KV_WIDTH), F32),
            jax.ShapeDtypeStruct((nb, WINDOW, KV_WIDTH), F32),
        ],
        scratch_shapes=[pltpu.VMEM((nb, 2 * WINDOW, 2 * KV_WIDTH), F32)],
        compiler_params=_params(1),
        name="prompt_proj_attn",
    )(sinks, x, g_pre, w_in)


def _s5_prompt_kernel(au_ref, bb_ref, cc_ref, are_ref, aim_ref, d_ref, wglu_ref, wout_ref,
                      m_ref, hfin_ref, hbuf, slab_scr, *, steps, nb):
    i = pl.program_id(0)

    @pl.when(i == 0)
    def _():
        hfin_ref[...] = jnp.zeros(hfin_ref.shape, F32)

    ys = []
    for j in range(N_STRIP):
        uj = au_ref[ATTN_SLABS + j]
        hbuf[...] = _dot(uj.astype(BF16), bb_ref[j])
        are = jnp.broadcast_to(are_ref[j:j + 1, :], (nb, STRIP_STATES))
        aim = jnp.broadcast_to(aim_ref[j:j + 1, :], (nb, STRIP_STATES))

        def step(t, carry):
            hr, hi = carry
            r = pl.multiple_of(t * nb, nb)
            br = hbuf[pl.ds(r, nb), 0:STRIP_STATES]
            bi = hbuf[pl.ds(r, nb), STRIP_STATES:2 * STRIP_STATES]
            nr = are * hr - aim * hi + br
            ni = are * hi + aim * hr + bi
            hbuf[pl.ds(r, nb), 0:STRIP_STATES] = nr
            hbuf[pl.ds(r, nb), STRIP_STATES:2 * STRIP_STATES] = ni
            return nr, ni

        h0 = (hfin_ref[j, :, 0:STRIP_STATES], hfin_ref[j, :, STRIP_STATES:2 * STRIP_STATES])
        hr, hi = lax.fori_loop(0, steps, step, h0, unroll=True)
        hfin_ref[j, :, 0:STRIP_STATES] = hr
        hfin_ref[j, :, STRIP_STATES:2 * STRIP_STATES] = hi
        ys.append(_dot(hbuf[...].astype(BF16), cc_ref[j]) + d_ref[:, j * LANE:(j + 1) * LANE] * uj)
    s = _gelu_glu(jnp.concatenate(ys, axis=1), wglu_ref)
    a = jnp.concatenate([au_ref[j] for j in range(ATTN_SLABS)], axis=1).astype(BF16)
    m = _dot(a, wout_ref[0:ATTN_WIDTH, :]) + _dot(s.astype(BF16), wout_ref[ATTN_WIDTH:, :])
    _store_sequence_major(m, slab_scr, m_ref, nb)


def _prompt_s5(au, tabs, d_skip, w_glu, w_out, *, nb, steps):
    a_re, a_im, bb, cc = tabs
    nslab, rows, _ = au.shape
    blk = steps * nb
    return pl.pallas_call(
        functools.partial(_s5_prompt_kernel, steps=steps, nb=nb),
        grid=(rows // blk,),
        in_specs=[
            pl.BlockSpec((nslab, blk, LANE), lambda i: (0, i, 0)),
            _const_spec(bb.shape), _const_spec(cc.shape),
            _const_spec(a_re.shape), _const_spec(a_im.shape),
            _const_spec((1, SSM_WIDTH)),
            _const_spec(w_glu.shape), _const_spec(w_out.shape),
        ],
        out_specs=[
            pl.BlockSpec((nb, steps, D_MODEL), lambda i: (0, i, 0)),
            pl.BlockSpec((N_STRIP, nb, 2 * STRIP_STATES), lambda i: (0, 0, 0)),
        ],
        out_shape=[
            jax.ShapeDtypeStruct((nb, rows // nb, D_MODEL), BF16),
            jax.ShapeDtypeStruct((N_STRIP, nb, 2 * STRIP_STATES), F32),
        ],
        scratch_shapes=[pltpu.VMEM((blk, 2 * STRIP_STATES), F32),
                        pltpu.VMEM((D_SLABS, blk, LANE), F32)],
        compiler_params=_params(1),
        name="prompt_s5",
    )(au, bb, cc, a_re, a_im, d_skip, w_glu, w_out)


def _post_ffn_kernel(x_ref, m_ref, gpm_ref, gpf_ref, gof_ref, wg_ref, wu_ref, wd_ref, o_ref):
    half = x_ref.shape[0] // FFN_SPLIT
    for r in range(FFN_SPLIT):
        rows = slice(r * half, (r + 1) * half)
        x1 = x_ref[rows, :] + _rms(m_ref[rows, :].astype(F32), gpm_ref[...])
        h = _rms(x1, gpf_ref[...]).astype(BF16)
        gate = _dot(h, wg_ref[...])
        up = _dot(h, wu_ref[...])
        act = (gate * jax.nn.sigmoid(gate) * up).astype(BF16)
        f = _dot(act, wd_ref[...])
        o_ref[rows, :] = x1 + _rms(f, gof_ref[...])


def _post_ffn(x, m, gains, ffn_w, *, rows, name):
    gpm, gpf, gof = gains
    wg, wu, wd = ffn_w
    row_spec = pl.BlockSpec((rows, D_MODEL), lambda i: (i, 0))
    return pl.pallas_call(
        _post_ffn_kernel,
        grid=(x.shape[0] // rows,),
        in_specs=[row_spec, row_spec,
                  _const_spec((1, D_MODEL)), _const_spec((1, D_MODEL)), _const_spec((1, D_MODEL)),
                  _const_spec(wg.shape), _const_spec(wu.shape), _const_spec(wd.shape)],
        out_specs=row_spec,
        out_shape=jax.ShapeDtypeStruct(x.shape, F32),
        compiler_params=_params(1),
        name=name,
    )(x, m, gpm, gpf, gof, wg, wu, wd)


CONV_HALO = 32
CONV_ROWS = 64


def _layer_norm_silu(c, lng_ref, lnb_ref):
    mu = jnp.mean(c, axis=-1, keepdims=True)
    cc = c - mu
    var = jnp.mean(cc * cc, axis=-1, keepdims=True)
    cn = cc * lax.rsqrt(var + LN_EPS) * lng_ref[...] + lnb_ref[...]
    return cn * jax.nn.sigmoid(cn)


def _conv_prompt_kernel(x_ref, g_ref, w1_ref, b1_ref, wdw_ref, bdw_ref, lng_ref, lnb_ref, w2_ref, b2_ref,
                        m_ref, tail_ref, gp_scr, c_scr, slab_scr, *, blk, nb):
    i = pl.program_id(0)
    halo = CONV_HALO * nb
    tail = CONV_TAIL * nb

    @pl.when(i == 0)
    def _():
        gp_scr[0:halo, :] = jnp.zeros((halo, D_MODEL), F32)

    x = _load_time_major(x_ref, slab_scr, nb)
    h = _rms(x, g_ref[...]).astype(BF16)
    a = _dot(h, w1_ref[...]) + b1_ref[...]
    gp_scr[halo:halo + blk, :] = a[:, :D_MODEL] * jax.nn.sigmoid(a[:, D_MODEL:])

    def chunk(ci, carry):
        base = pl.multiple_of(ci * CONV_ROWS, CONV_ROWS)
        for l in range(D_SLABS):
            lanes = slice(l * LANE, (l + 1) * LANE)
            xs = gp_scr[pl.ds(base + halo - tail, CONV_ROWS + tail), lanes]
            acc = jnp.broadcast_to(bdw_ref[:, lanes], (CONV_ROWS, LANE))
            for j in range(CONV_WIDTH):
                wj = wdw_ref[j * SUBLANE:(j + 1) * SUBLANE, lanes]
                wj = jnp.concatenate([wj] * (CONV_ROWS // SUBLANE), axis=0)
                acc = acc + wj * xs[j * nb:j * nb + CONV_ROWS]
            c_scr[pl.ds(base, CONV_ROWS), lanes] = acc
        return carry

    lax.fori_loop(0, blk // CONV_ROWS, chunk, 0)
    act = _layer_norm_silu(c_scr[...], lng_ref, lnb_ref).astype(BF16)
    _store_sequence_major(_dot(act, w2_ref[...]) + b2_ref[...], slab_scr, m_ref, nb)
    tail_ref[...] = gp_scr[halo + blk - tail:halo + blk, :]
    gp_scr[0:halo, :] = gp_scr[blk:blk + halo, :]


def _prompt_conv(x, g_pre, cw, *, steps):
    w1, b1, wdw8, bdw, lng, lnb, w2, b2 = cw
    nb, seq, _ = x.shape
    blk = steps * nb
    assert steps >= CONV_HALO and blk % CONV_ROWS == 0
    vec = lambda n: _const_spec((1, n))
    seq_spec = pl.BlockSpec((nb, steps, D_MODEL), lambda i: (0, i, 0))
    return pl.pallas_call(
        functools.partial(_conv_prompt_kernel, blk=blk, nb=nb),
        grid=(seq // steps,),
        in_specs=[
            seq_spec,
            vec(D_MODEL), _const_spec(w1.shape), vec(2 * D_MODEL),
            _const_spec(wdw8.shape), vec(D_MODEL), vec(D_MODEL), vec(D_MODEL),
            _const_spec(w2.shape), vec(D_MODEL),
        ],
        out_specs=[
            seq_spec,
            pl.BlockSpec((CONV_TAIL * nb, D_MODEL), lambda i: (0, 0)),
        ],
        out_shape=[
            jax.ShapeDtypeStruct((nb, seq, D_MODEL), BF16),
            jax.ShapeDtypeStruct((CONV_TAIL * nb, D_MODEL), F32),
        ],
        scratch_shapes=[pltpu.VMEM((blk + CONV_HALO * nb, D_MODEL), F32),
                        pltpu.VMEM((blk, D_MODEL), F32),
                        pltpu.VMEM((D_SLABS, blk, LANE), F32)],
        compiler_params=_params(1),
        name="prompt_conv",
    )(x, g_pre, w1, b1, wdw8, bdw, lng, lnb, w2, b2)


def _sample_l0_kernel(sink_ref, x_ref, g_ref, w_ref, ck_ref, cv_ref, sre_ref, sim_ref,
                      bb_ref, cc_ref, are_ref, aim_ref, d_ref, wglu_ref, wout_ref,
                      m_ref, ko_ref, vo_ref, ore_ref, oim_ref,
                      q_scr, k_scr, v_scr, a_scr, u_scr, y_scr, *, nseq, ntok):
    qw = N_Q_HEADS * LANE
    h = _rms(x_ref[...], g_ref[...]).astype(BF16)
    z = _dot(h, w_ref[...])
    q_scr[...] = z[:, :qw] * (HEAD_DIM ** -0.5)
    kn = z[:, qw:qw + KV_WIDTH]
    vn = z[:, qw + KV_WIDTH:qw + 2 * KV_WIDTH]
    k_scr[...] = kn
    v_scr[...] = vn
    for j in range(N_STRIP):
        u_scr[j] = z[:, qw + 2 * KV_WIDTH + j * LANE:qw + 2 * KV_WIDTH + (j + 1) * LANE]
    y_scr[...] = jnp.zeros(y_scr.shape, F32)

    for j in range(N_STRIP):
        hr = sre_ref[:, j * STRIP_STATES:(j + 1) * STRIP_STATES]
        hi = sim_ref[:, j * STRIP_STATES:(j + 1) * STRIP_STATES]
        are = are_ref[j:j + 1, :]
        aim = aim_ref[j:j + 1, :]
        dj = d_ref[:, j * LANE:(j + 1) * LANE]
        for t in range(ntok):
            ut = u_scr[j, pl.ds(t, nseq, stride=SAMPLE_T), :]
            bu = _dot(ut.astype(BF16), bb_ref[j])
            hr, hi = (are * hr - aim * hi + bu[:, :STRIP_STATES],
                      are * hi + aim * hr + bu[:, STRIP_STATES:])
            hcat = jnp.concatenate([hr, hi], axis=1).astype(BF16)
            y_scr[j, pl.ds(t, nseq, stride=SAMPLE_T), :] = _dot(hcat, cc_ref[j]) + dj * ut
        ore_ref[:, j * STRIP_STATES:(j + 1) * STRIP_STATES] = hr
        oim_ref[:, j * STRIP_STATES:(j + 1) * STRIP_STATES] = hi
    s = _gelu_glu(jnp.concatenate([y_scr[j] for j in range(N_STRIP)], axis=1), wglu_ref)

    nkeys = WINDOW + SAMPLE_T
    shape = (N_Q_HEADS * SAMPLE_T, nkeys)
    tq = jnp.bitwise_and(lax.broadcasted_iota(jnp.int32, shape, 0), SAMPLE_T - 1)
    col = lax.broadcasted_iota(jnp.int32, shape, 1)
    ninf = jnp.float32(-jnp.inf)
    bias = jnp.where(col < WINDOW, jnp.where(col > tq, 0.0, ninf),
                     jnp.where(col - WINDOW <= tq, 0.0, ninf))
    sink = jnp.concatenate(
        [jnp.full((SAMPLE_T, 1), sink_ref[hh], F32) for hh in range(N_Q_HEADS)], axis=0)
    left = lax.broadcasted_iota(jnp.int32, (SAMPLE_T, LANE), 1) < HEAD_DIM

    def seq_body(n, carry):
        r = pl.multiple_of(n * SAMPLE_T, SAMPLE_T)
        qs = jnp.concatenate(
            [q_scr[pl.ds(r, SAMPLE_T), hh * LANE:(hh + 1) * LANE] for hh in range(N_Q_HEADS)], axis=0)
        kall = jnp.concatenate([ck_ref[n], k_scr[pl.ds(r, SAMPLE_T), :]], axis=0)
        vall = jnp.concatenate([cv_ref[n], v_scr[pl.ds(r, SAMPLE_T), :]], axis=0)
        ko_ref[n] = kall[ntok:ntok + WINDOW]
        vo_ref[n] = vall[ntok:ntok + WINDOW]
        sc = _dot_nt(qs.astype(BF16), kall.astype(BF16)) + bias
        mx = jnp.maximum(jnp.max(sc, axis=1, keepdims=True), sink)
        p = jnp.exp(sc - mx)
        den = jnp.sum(p, axis=1, keepdims=True) + jnp.exp(sink - mx)
        p = (p / den).astype(BF16)
        o1 = _dot(p, vall.astype(BF16))
        o2 = _dot(p, pltpu.roll(vall, HEAD_DIM, axis=1).astype(BF16))
        slabs = []
        for hp in range(N_Q_HEADS // 2):
            lo, ro = (o1, o2) if hp < Q_PER_KV // 2 else (o2, o1)
            lrow = 2 * hp * SAMPLE_T
            slabs.append(jnp.where(left, lo[lrow:lrow + SAMPLE_T], ro[lrow + SAMPLE_T:lrow + 2 * SAMPLE_T]))
        a_scr[pl.ds(r, SAMPLE_T), :] = jnp.concatenate(slabs, axis=1)
        return carry

    lax.fori_loop(0, nseq, seq_body, 0, unroll=SAMPLE_ATTN_UNROLL)
    m = (_dot(a_scr[...].astype(BF16), wout_ref[0:ATTN_WIDTH, :])
         + _dot(s.astype(BF16), wout_ref[ATTN_WIDTH:, :]))
    m_ref[...] = m.astype(m_ref.dtype)


def _sample_l0(x, g_pre, w_in_s, sinks, ck, cv, s_re, s_im, tabs, d_skip, w_glu, w_out, *, nseq, ntok):
    a_re, a_im, bb, cc = tabs
    rows_all = x.shape[0]
    rows = nseq * SAMPLE_T
    nstate = SSM_GROUPS * SSM_STATE
    row_spec = lambda w: pl.BlockSpec((rows, w), lambda i: (i, 0))
    return pl.pallas_call(
        functools.partial(_sample_l0_kernel, nseq=nseq, ntok=ntok),
        grid=(rows_all // rows,),
        in_specs=[
            pl.BlockSpec(memory_space=pltpu.SMEM),
            row_spec(D_MODEL), _const_spec((1, D_MODEL)), _const_spec(w_in_s.shape),
            pl.BlockSpec((nseq, WINDOW, KV_WIDTH), lambda i: (i, 0, 0)),
            pl.BlockSpec((nseq, WINDOW, KV_WIDTH), lambda i: (i, 0, 0)),
            pl.BlockSpec((nseq, nstate), lambda i: (i, 0)),
            pl.BlockSpec((nseq, nstate), lambda i: (i, 0)),
            _const_spec(bb.shape), _const_spec(cc.shape),
            _const_spec(a_re.shape), _const_spec(a_im.shape),
            _const_spec((1, SSM_WIDTH)),
            _const_spec(w_glu.shape), _const_spec(w_out.shape),
        ],
        out_specs=[
            row_spec(D_MODEL),
            pl.BlockSpec((nseq, WINDOW, KV_WIDTH), lambda i: (i, 0, 0)),
            pl.BlockSpec((nseq, WINDOW, KV_WIDTH), lambda i: (i, 0, 0)),
            pl.BlockSpec((nseq, nstate), lambda i: (i, 0)),
            pl.BlockSpec((nseq, nstate), lambda i: (i, 0)),
        ],
        out_shape=[
            jax.ShapeDtypeStruct((rows_all, D_MODEL), BF16),
            jax.ShapeDtypeStruct(ck.shape, F32),
            jax.ShapeDtypeStruct(cv.shape, F32),
            jax.ShapeDtypeStruct((rows_all // SAMPLE_T, nstate), F32),
            jax.ShapeDtypeStruct((rows_all // SAMPLE_T, nstate), F32),
        ],
        scratch_shapes=[
            pltpu.VMEM((rows, N_Q_HEADS * LANE), F32),
            pltpu.VMEM((rows, KV_WIDTH), F32),
            pltpu.VMEM((rows, KV_WIDTH), F32),
            pltpu.VMEM((rows, ATTN_WIDTH), F32),
            pltpu.VMEM((N_STRIP, rows, LANE), F32),
            pltpu.VMEM((N_STRIP, rows, LANE), F32),
        ],
        compiler_params=_params(1),
        name="sample_l0",
    )(sinks, x, g_pre, w_in_s, ck, cv, s_re, s_im, bb, cc, a_re, a_im, d_skip, w_glu, w_out)


def _sample_l1_kernel(x_ref, g_ref, w1_ref, b1_ref, st_ref, wst_ref, wnew_ref, bdw_ref, lng_ref, lnb_ref,
                      w2_ref, b2_ref, m_ref, sto_ref, c_scr, *, nseq, ntok):
    h = _rms(x_ref[...], g_ref[...]).astype(BF16)
    a = _dot(h, w1_ref[...]) + b1_ref[...]
    gg = a[:, :D_MODEL] * jax.nn.sigmoid(a[:, D_MODEL:])
    g3 = gg.reshape(nseq, SAMPLE_T, D_MODEL)
    st = st_ref[...]
    sto_ref[...] = jnp.concatenate([st[:, ntok:, :], g3[:, :ntok, :]], axis=1)
    c_scr[...] = jnp.zeros(c_scr.shape, F32)
    for t in range(ntok):
        ct = (jnp.sum(st * wst_ref[t][None], axis=1) + jnp.sum(g3 * wnew_ref[t][None], axis=1)
              + bdw_ref[...])
        for j in range(D_SLABS):
            c_scr[j, pl.ds(t, nseq, stride=SAMPLE_T), :] = ct[:, j * LANE:(j + 1) * LANE]
    c = jnp.concatenate([c_scr[j] for j in range(D_SLABS)], axis=1)
    act = _layer_norm_silu(c, lng_ref, lnb_ref).astype(BF16)
    m_ref[...] = (_dot(act, w2_ref[...]) + b2_ref[...]).astype(m_ref.dtype)


def _sample_l1(x, g_pre, cw, st, wst, wnew, *, nseq, ntok):
    w1, b1, _, bdw, lng, lnb, w2, b2 = cw
    rows_all = x.shape[0]
    rows = nseq * SAMPLE_T
    vec = lambda n: _const_spec((1, n))
    row_spec = pl.BlockSpec((rows, D_MODEL), lambda i: (i, 0))
    return pl.pallas_call(
        functools.partial(_sample_l1_kernel, nseq=nseq, ntok=ntok),
        grid=(rows_all // rows,),
        in_specs=[
            row_spec, vec(D_MODEL), _const_spec(w1.shape), vec(2 * D_MODEL),
            pl.BlockSpec((nseq, CONV_TAIL, D_MODEL), lambda i: (i, 0, 0)),
            _const_spec(wst.shape), _const_spec(wnew.shape),
            vec(D_MODEL), vec(D_MODEL), vec(D_MODEL), _const_spec(w2.shape), vec(D_MODEL),
        ],
        out_specs=[row_spec, pl.BlockSpec((nseq, CONV_TAIL, D_MODEL), lambda i: (i, 0, 0))],
        out_shape=[
            jax.ShapeDtypeStruct((rows_all, D_MODEL), BF16),
            jax.ShapeDtypeStruct(st.shape, F32),
        ],
        scratch_shapes=[pltpu.VMEM((D_SLABS, rows, LANE), F32)],
        compiler_params=_params(1),
        name="sample_l1",
    )(x, g_pre, w1, b1, st, wst, wnew, bdw, lng, lnb, w2, b2)


PROMPT_FFN_ROWS = 512
PROMPT_STEPS = 64
SAMPLE_SEQS = 32
SAMPLE_FFN_ROWS = 512
SAMPLE_ATTN_UNROLL = 8


def kernel(x_prompt, x_sample, cache_k, cache_v, state_s5_re, state_s5_im, state_conv, norm_pre_mix, norm_post_mix, norm_pre_ffn, norm_post_ffn, w_in_ab, attn_sinks, s5_lambda_re, s5_lambda_im, s5_log_step, s5_b_re, s5_b_im, s5_c_re, s5_c_im, s5_d, w_glu, w_out_ab, w_pw1, b_pw1, w_dw, b_dw, conv_ln_g, conv_ln_b, w_pw2, b_pw2, w_ffn_gate, w_ffn_up, w_ffn_down):
    nb, seq, _ = x_prompt.shape
    ns, ntok, _ = x_sample.shape
    assert nb == SUBLANE and ntok <= SAMPLE_T and seq % WINDOW == 0 and seq % PROMPT_STEPS == 0
    row = lambda v: v.reshape(1, -1)
    gains = lambda layer: tuple(row(n[layer]) for n in (norm_post_mix, norm_pre_ffn, norm_post_ffn))

    w_in = w_in_ab[0].astype(BF16)
    wq = w_in_ab[0][:, :ATTN_WIDTH].reshape(D_MODEL, N_Q_HEADS, HEAD_DIM)
    lane_kv = (jnp.arange(LANE) // HEAD_DIM)[None, :]
    head_kv = (jnp.arange(N_Q_HEADS) // Q_PER_KV)[:, None]
    wq_slab = jnp.where((lane_kv == head_kv)[None], jnp.concatenate([wq, wq], axis=2), 0.0)
    w_in_s = jnp.concatenate(
        [wq_slab.reshape(D_MODEL, N_Q_HEADS * LANE), w_in_ab[0][:, ATTN_WIDTH:]], axis=1).astype(BF16)
    w_glu_b = w_glu[0].astype(BF16)
    w_out = w_out_ab[0].astype(BF16)
    ffn_w = [(w_ffn_gate[l].astype(BF16), w_ffn_up[l].astype(BF16), w_ffn_down[l].astype(BF16))
             for l in range(2)]
    sinks = attn_sinks[0]
    tabs = _s5_tables(s5_lambda_re[0], s5_lambda_im[0], s5_log_step[0], s5_b_re[0], s5_b_im[0],
                      s5_c_re[0], s5_c_im[0])
    d_skip = row(s5_d[0])
    wdw = w_dw[0]
    conv_w = (w_pw1[0].astype(BF16), row(b_pw1[0]), jnp.repeat(wdw, SUBLANE, axis=0), row(b_dw[0]),
              row(conv_ln_g[0]), row(conv_ln_b[0]), w_pw2[0].astype(BF16), row(b_pw2[0]))
    tt = jnp.arange(ntok)[:, None]
    tap_st = jnp.arange(CONV_TAIL)[None, :] - tt
    wst = jnp.where((tap_st >= 0)[..., None], wdw[jnp.clip(tap_st, 0, CONV_WIDTH - 1)], 0.0)
    tap_new = CONV_TAIL + jnp.arange(SAMPLE_T)[None, :] - tt
    wnew = jnp.where((tap_new <= CONV_TAIL)[..., None], wdw[jnp.clip(tap_new, 0, CONV_WIDTH - 1)], 0.0)

    flat = lambda v: v.reshape(nb * seq, D_MODEL)
    au_p, kk_p, vk_p = _prompt_proj_attn(x_prompt, row(norm_pre_mix[0]), w_in, sinks)
    m0_p, hfin_p = _prompt_s5(au_p, tabs, d_skip, w_glu_b, w_out, nb=nb, steps=PROMPT_STEPS)
    x1_p = _post_ffn(flat(x_prompt), flat(m0_p), gains(0), ffn_w[0], rows=PROMPT_FFN_ROWS, name="prompt_ffn0")
    m1_p, tail_p = _prompt_conv(x1_p.reshape(nb, seq, D_MODEL), row(norm_pre_mix[1]), conv_w, steps=PROMPT_STEPS)
    y_prompt = _post_ffn(x1_p, flat(m1_p), gains(1), ffn_w[1], rows=PROMPT_FFN_ROWS, name="prompt_ffn1")
    y_prompt = y_prompt.reshape(nb, seq, D_MODEL)

    xs = jnp.pad(x_sample, ((0, 0), (0, SAMPLE_T - ntok), (0, 0))).reshape(ns * SAMPLE_T, D_MODEL)
    nstate = SSM_GROUPS * SSM_STATE
    m0_s, ko_s, vo_s, sre_s, sim_s = _sample_l0(
        xs, row(norm_pre_mix[0]), w_in_s, sinks,
        cache_k[0].reshape(ns, WINDOW, KV_WIDTH), cache_v[0].reshape(ns, WINDOW, KV_WIDTH),
        state_s5_re[0].reshape(ns, nstate), state_s5_im[0].reshape(ns, nstate),
        tabs, d_skip, w_glu_b, w_out, nseq=SAMPLE_SEQS, ntok=ntok)
    x1_s = _post_ffn(xs, m0_s, gains(0), ffn_w[0], rows=SAMPLE_FFN_ROWS, name="sample_ffn0")
    m1_s, sto_s = _sample_l1(x1_s, row(norm_pre_mix[1]), conv_w, state_conv[0], wst, wnew,
                           nseq=SAMPLE_SEQS, ntok=ntok)
    y_s = _post_ffn(x1_s, m1_s, gains(1), ffn_w[1], rows=SAMPLE_FFN_ROWS, name="sample_ffn1")

    y_sample = y_s.reshape(ns, SAMPLE_T, D_MODEL)[:, :ntok]
    k_prompt = kk_p.reshape(1, nb, WINDOW, N_KV_HEADS, HEAD_DIM)
    v_prompt = vk_p.reshape(1, nb, WINDOW, N_KV_HEADS, HEAD_DIM)
    hfin = jnp.swapaxes(hfin_p, 0, 1)
    s5_re_prompt = hfin[:, :, :STRIP_STATES].reshape(1, nb, SSM_GROUPS, SSM_STATE)
    s5_im_prompt = hfin[:, :, STRIP_STATES:].reshape(1, nb, SSM_GROUPS, SSM_STATE)
    conv_prompt = jnp.swapaxes(tail_p.reshape(CONV_TAIL, nb, D_MODEL), 0, 1)[None]
    k_sample = ko_s.reshape(1, ns, WINDOW, N_KV_HEADS, HEAD_DIM)
    v_sample = vo_s.reshape(1, ns, WINDOW, N_KV_HEADS, HEAD_DIM)
    s5_re_sample = sre_s.reshape(1, ns, SSM_GROUPS, SSM_STATE)
    s5_im_sample = sim_s.reshape(1, ns, SSM_GROUPS, SSM_STATE)
    conv_sample = sto_s[None]
    return (y_prompt, y_sample, k_prompt, v_prompt, s5_re_prompt, s5_im_prompt, conv_prompt,
            k_sample, v_sample, s5_re_sample, s5_im_sample, conv_sample)
```

```python
import functools

import jax
import jax.numpy as jnp
from jax import lax
from jax.experimental import pallas as pl
from jax.experimental.pallas import tpu as pltpu

F32 = jnp.float32
BF16 = jnp.bfloat16

LANE = 128
SUBLANE = 8
VMEM_LIMIT_BYTES = 56 * 1024 * 1024

D_MODEL = 1024
HEAD_DIM = 64
N_Q_HEADS = 8
N_KV_HEADS = 2
Q_PER_KV = N_Q_HEADS // N_KV_HEADS
WINDOW = 128
ATTN_WIDTH = N_Q_HEADS * HEAD_DIM
KV_WIDTH = N_KV_HEADS * HEAD_DIM
SSM_WIDTH = 512
SSM_GROUP = 16
SSM_GROUPS = 32
SSM_STATE = 64
D_FF = 2816
CONV_WIDTH = 31
CONV_TAIL = CONV_WIDTH - 1
RMS_EPS = 1e-6
LN_EPS = 1e-5

N_STRIP = SSM_WIDTH // LANE
STRIP_STATES = (SSM_GROUPS // N_STRIP) * SSM_STATE
D_SLABS = D_MODEL // LANE
ATTN_SLABS = ATTN_WIDTH // LANE
FFN_SPLIT = 2

SAMPLE_T = SUBLANE


def _rms(x, g):
    return x * lax.rsqrt(jnp.mean(x * x, axis=-1, keepdims=True) + RMS_EPS) * g


def _dot(a, b):
    return jnp.dot(a, b, preferred_element_type=F32)


def _dot_nt(a, b):
    return lax.dot_general(a, b, (((1,), (1,)), ((), ())), preferred_element_type=F32)


def _const_spec(shape):
    nd = len(shape)
    return pl.BlockSpec(shape, lambda *_: (0,) * nd, pipeline_mode=pl.Buffered(1))


def _params(n_axes):
    return pltpu.CompilerParams(
        dimension_semantics=("arbitrary",) * n_axes, vmem_limit_bytes=VMEM_LIMIT_BYTES)


def _load_time_major(x_ref, slab_scr, nb):
    steps = x_ref.shape[1]
    nslab = x_ref.shape[2] // LANE
    for b in range(nb):
        for j in range(nslab):
            slab_scr[j, pl.ds(b, steps, stride=nb), :] = x_ref[b, :, j * LANE:(j + 1) * LANE]
    return jnp.concatenate([slab_scr[j] for j in range(nslab)], axis=1)


def _store_sequence_major(val, slab_scr, out_ref, nb):
    steps = out_ref.shape[1]
    nslab = out_ref.shape[2] // LANE
    for j in range(nslab):
        slab_scr[j] = val[:, j * LANE:(j + 1) * LANE]
    for b in range(nb):
        for j in range(nslab):
            out_ref[b, :, j * LANE:(j + 1) * LANE] = (
                slab_scr[j, pl.ds(b, steps, stride=nb), :].astype(out_ref.dtype))


def _s5_disc_kernel(lre_ref, lim_ref, ls_ref, bre_ref, bim_ref, are_ref, aim_ref, bbre_ref, bbim_ref):
    lam_re = lre_ref[...]
    lam_im = lim_ref[...]
    dt = jnp.exp(ls_ref[...])
    mag = jnp.exp(lam_re * dt)
    lb_re = mag * jnp.cos(lam_im * dt)
    lb_im = mag * jnp.sin(lam_im * dt)
    den = lam_re * lam_re + lam_im * lam_im
    coef_re = ((lb_re - 1.0) * lam_re + lb_im * lam_im) / den
    coef_im = (lb_im * lam_re - (lb_re - 1.0) * lam_im) / den
    b_re = bre_ref[...]
    b_im = bim_ref[...]
    are_ref[...] = lb_re
    aim_ref[...] = lb_im
    bbre_ref[...] = coef_re * b_re - coef_im * b_im
    bbim_ref[...] = coef_re * b_im + coef_im * b_re


def _s5_tables(lam_re, lam_im, log_step, b_re, b_im, c_re, c_im):
    rep = lambda v: jnp.repeat(v, SSM_GROUP, axis=0)
    rows = SSM_GROUPS * SSM_GROUP
    bt = lambda b: jnp.swapaxes(b, 1, 2).reshape(rows, SSM_STATE)
    ls = jnp.broadcast_to(log_step[:, None], (SSM_GROUPS, SSM_STATE))
    shp = jax.ShapeDtypeStruct((rows, SSM_STATE), F32)
    a_re, a_im, bb_re, bb_im = pl.pallas_call(
        _s5_disc_kernel, out_shape=(shp, shp, shp, shp), name="s5_disc",
    )(rep(lam_re), rep(lam_im), rep(ls), bt(b_re), bt(b_im))
    a_re = a_re[::SSM_GROUP].reshape(N_STRIP, STRIP_STATES)
    a_im = a_im[::SSM_GROUP].reshape(N_STRIP, STRIP_STATES)
    gl = SSM_GROUPS // N_STRIP
    eye = jnp.eye(gl, dtype=F32).reshape(1, gl, 1, gl, 1)

    def in_blockdiag(m):
        m = m.reshape(N_STRIP, gl, SSM_GROUP, 1, SSM_STATE)
        return (m * eye).reshape(N_STRIP, LANE, STRIP_STATES)

    def out_blockdiag(m):
        m = jnp.swapaxes(m.reshape(N_STRIP, gl, SSM_GROUP, SSM_STATE), 2, 3)
        return (m.reshape(N_STRIP, gl, SSM_STATE, 1, SSM_GROUP) * eye).reshape(N_STRIP, STRIP_STATES, LANE)

    bb = jnp.concatenate([in_blockdiag(bb_re), in_blockdiag(bb_im)], axis=2).astype(BF16)
    cc = jnp.concatenate([out_blockdiag(c_re), -out_blockdiag(c_im)], axis=1).astype(BF16)
    return a_re, a_im, bb, cc


def _gelu_glu(y, wglu_ref):
    g = jax.nn.gelu(y)
    return g * jax.nn.sigmoid(_dot(g.astype(BF16), wglu_ref[...]))


def _band_attention(q, kv_scr, sinks, from_prev, prev_bias, seqs):
    gw = Q_PER_KV * HEAD_DIM
    left = lax.broadcasted_iota(jnp.int32, (2 * WINDOW, KV_WIDTH), 1) < HEAD_DIM
    head_of_lane = lax.shift_right_logical(
        lax.broadcasted_iota(jnp.int32, (WINDOW, gw), 1), HEAD_DIM.bit_length() - 1)
    scores, values = [], []
    for n, b in enumerate(seqs):
        kb = kv_scr[b, :, 0:KV_WIDTH]
        vb = kv_scr[b, :, KV_WIDTH:2 * KV_WIDTH]
        kr = pltpu.roll(kb, HEAD_DIM, axis=1)
        vr = pltpu.roll(vb, HEAD_DIM, axis=1)
        for g in range(N_KV_HEADS):
            kk = jnp.where(left, kb, kr) if g == 0 else jnp.where(left, kr, kb)
            vv = jnp.where(left, vb, vr) if g == 0 else jnp.where(left, vr, vb)
            k4 = jnp.concatenate([kk, kk], axis=1).astype(BF16)
            values.append(jnp.concatenate([vv, vv], axis=1).astype(BF16))
            qg = q[n * WINDOW:(n + 1) * WINDOW, g * gw:(g + 1) * gw]
            qst = jnp.concatenate(
                [jnp.where(head_of_lane == r, qg, 0.0) for r in range(Q_PER_KV)], axis=0).astype(BF16)
            scores.append(_dot_nt(qst, k4))
    nblk = len(scores)
    s = jnp.concatenate(scores, axis=0)
    prev = jnp.concatenate([from_prev] * nblk, axis=0)
    s = jnp.where(prev, s[:, :WINDOW] + prev_bias, s[:, WINDOW:])
    sink = jnp.concatenate(
        [jnp.full((WINDOW, WINDOW), sinks[(n % N_KV_HEADS) * Q_PER_KV + r], F32)
         for n in range(nblk) for r in range(Q_PER_KV)], axis=0)
    m = jnp.maximum(jnp.max(s, axis=1, keepdims=True), sink)
    p = jnp.exp(s - m)
    den = _dot(p.astype(BF16), jnp.ones((WINDOW, WINDOW), BF16)) + jnp.exp(sink - m)
    p = p / den
    p = jnp.concatenate([jnp.where(prev, p, 0.0).astype(BF16), jnp.where(prev, 0.0, p).astype(BF16)], axis=1)
    rows = Q_PER_KV * WINDOW
    outs = []
    for b in range(len(seqs)):
        og = []
        for g in range(N_KV_HEADS):
            n = b * N_KV_HEADS + g
            o4 = _dot(p[n * rows:(n + 1) * rows], values[n])
            o = jnp.where(head_of_lane == 0, o4[0:WINDOW], 0.0)
            for r in range(1, Q_PER_KV):
                o = jnp.where(head_of_lane == r, o4[r * WINDOW:(r + 1) * WINDOW], o)
            og.append(o)
        outs.append(jnp.concatenate(og, axis=1))
    return outs


def _proj_attn_kernel(sink_ref, x_ref, g_ref, w_ref, au_ref, kk_ref, vk_ref, kv_scr, *, nb):
    i = pl.program_id(0)
    z = jnp.concatenate(
        [_dot(_rms(x_ref[b:b + 2].reshape(2 * WINDOW, D_MODEL), g_ref[...]).astype(BF16), w_ref[...])
         for b in range(0, nb, 2)], axis=0)
    ucol = ATTN_WIDTH + 2 * KV_WIDTH

    @pl.when(i == 0)
    def _():
        kv_scr[:, 0:WINDOW, :] = jnp.zeros((nb, WINDOW, 2 * KV_WIDTH), F32)

    for b in range(nb):
        kv_scr[b, WINDOW:2 * WINDOW, :] = z[b * WINDOW:(b + 1) * WINDOW, ATTN_WIDTH:ucol]
    q = z[:, :ATTN_WIDTH] * (HEAD_DIM ** -0.5)

    shape = (Q_PER_KV * WINDOW, WINDOW)
    row = jnp.bitwise_and(lax.broadcasted_iota(jnp.int32, shape, 0), WINDOW - 1)
    col = lax.broadcasted_iota(jnp.int32, shape, 1)
    prev_bias = jnp.where(i == 0, -jnp.inf, 0.0).astype(F32)
    sinks = [sink_ref[hh] for hh in range(N_Q_HEADS)]
    outs = _band_attention(q, kv_scr, sinks, col > row, prev_bias, range(nb))
    for b in range(nb):
        zb = z[b * WINDOW:(b + 1) * WINDOW]
        for j in range(ATTN_SLABS):
            au_ref[j, pl.ds(b, WINDOW, stride=nb), :] = outs[b][:, j * LANE:(j + 1) * LANE]
        for j in range(N_STRIP):
            au_ref[ATTN_SLABS + j, pl.ds(b, WINDOW, stride=nb), :] = zb[:, ucol + j * LANE:ucol + (j + 1) * LANE]
        kk_ref[b] = zb[:, ATTN_WIDTH:ATTN_WIDTH + KV_WIDTH]
        vk_ref[b] = zb[:, ATTN_WIDTH + KV_WIDTH:ucol]
    kv_scr[:, 0:WINDOW, :] = kv_scr[:, WINDOW:2 * WINDOW, :]


def _prompt_proj_attn(x, g_pre, w_in, sinks):
    nb, seq, _ = x.shape
    nslab = ATTN_SLABS + N_STRIP
    return pl.pallas_call(
        functools.partial(_proj_attn_kernel, nb=nb),
        grid=(seq // WINDOW,),
        in_specs=[
            pl.BlockSpec(memory_space=pltpu.SMEM),
            pl.BlockSpec((nb, WINDOW, D_MODEL), lambda i: (0, i, 0)),
            _const_spec((1, D_MODEL)),
            _const_spec(w_in.shape),
        ],
        out_specs=[
            pl.BlockSpec((nslab, WINDOW * nb, LANE), lambda i: (0, i, 0)),
            pl.BlockSpec((nb, WINDOW, KV_WIDTH), lambda i: (0, 0, 0)),
            pl.BlockSpec((nb, WINDOW, KV_WIDTH), lambda i: (0, 0, 0)),
        ],
        out_shape=[
            jax.ShapeDtypeStruct((nslab, seq * nb, LANE), F32),
            jax.ShapeDtypeStruct((nb, WINDOW, KV_WIDTH), F32),
            jax.ShapeDtypeStruct((nb, WINDOW, KV_WIDTH), F32),
        ],
        scratch_shapes=[pltpu.VMEM((nb, 2 * WINDOW, 2 * KV_WIDTH), F32)],
        compiler_params=_params(1),
        name="prompt_proj_attn",
    )(sinks, x, g_pre, w_in)


def _s5_prompt_kernel(au_ref, bb_ref, cc_ref, are_ref, aim_ref, d_ref, wglu_ref, wout_ref,
                      m_ref, hfin_ref, hbuf, slab_scr, *, steps, nb):
    i = pl.program_id(0)

    @pl.when(i == 0)
    def _():
        hfin_ref[...] = jnp.zeros(hfin_ref.shape, F32)

    ys = []
    for j in range(N_STRIP):
        uj = au_ref[ATTN_SLABS + j]
        hbuf[...] = _dot(uj.astype(BF16), bb_ref[j])
        are = jnp.broadcast_to(are_ref[j:j + 1, :], (nb, STRIP_STATES))
        aim = jnp.broadcast_to(aim_ref[j:j + 1, :], (nb, STRIP_STATES))

        def step(t, carry):
            hr, hi = carry
            r = pl.multiple_of(t * nb, nb)
            br = hbuf[pl.ds(r, nb), 0:STRIP_STATES]
            bi = hbuf[pl.ds(r, nb), STRIP_STATES:2 * STRIP_STATES]
            nr = are * hr - aim * hi + br
            ni = are * hi + aim * hr + bi
            hbuf[pl.ds(r, nb), 0:STRIP_STATES] = nr
            hbuf[pl.ds(r, nb), STRIP_STATES:2 * STRIP_STATES] = ni
            return nr, ni

        h0 = (hfin_ref[j, :, 0:STRIP_STATES], hfin_ref[j, :, STRIP_STATES:2 * STRIP_STATES])
        hr, hi = lax.fori_loop(0, steps, step, h0, unroll=True)
        hfin_ref[j, :, 0:STRIP_STATES] = hr
        hfin_ref[j, :, STRIP_STATES:2 * STRIP_STATES] = hi
        ys.append(_dot(hbuf[...].astype(BF16), cc_ref[j]) + d_ref[:, j * LANE:(j + 1) * LANE] * uj)
    s = _gelu_glu(jnp.concatenate(ys, axis=1), wglu_ref)
    a = jnp.concatenate([au_ref[j] for j in range(ATTN_SLABS)], axis=1).astype(BF16)
    m = _dot(a, wout_ref[0:ATTN_WIDTH, :]) + _dot(s.astype(BF16), wout_ref[ATTN_WIDTH:, :])
    _store_sequence_major(m, slab_scr, m_ref, nb)


def _prompt_s5(au, tabs, d_skip, w_glu, w_out, *, nb, steps):
    a_re, a_im, bb, cc = tabs
    nslab, rows, _ = au.shape
    blk = steps * nb
    return pl.pallas_call(
        functools.partial(_s5_prompt_kernel, steps=steps, nb=nb),
        grid=(rows // blk,),
        in_specs=[
            pl.BlockSpec((nslab, blk, LANE), lambda i: (0, i, 0)),
            _const_spec(bb.shape), _const_spec(cc.shape),
            _const_spec(a_re.shape), _const_spec(a_im.shape),
            _const_spec((1, SSM_WIDTH)),
            _const_spec(w_glu.shape), _const_spec(w_out.shape),
        ],
        out_specs=[
            pl.BlockSpec((nb, steps, D_MODEL), lambda i: (0, i, 0)),
            pl.BlockSpec((N_STRIP, nb, 2 * STRIP_STATES), lambda i: (0, 0, 0)),
        ],
        out_shape=[
            jax.ShapeDtypeStruct((nb, rows // nb, D_MODEL), BF16),
            jax.ShapeDtypeStruct((N_STRIP, nb, 2 * STRIP_STATES), F32),
        ],
        scratch_shapes=[pltpu.VMEM((blk, 2 * STRIP_STATES), F32),
                        pltpu.VMEM((D_SLABS, blk, LANE), F32)],
        compiler_params=_params(1),
        name="prompt_s5",
    )(au, bb, cc, a_re, a_im, d_skip, w_glu, w_out)


def _post_ffn_kernel(x_ref, m_ref, gpm_ref, gpf_ref, gof_ref, wg_ref, wu_ref, wd_ref, o_ref):
    half = x_ref.shape[0] // FFN_SPLIT
    for r in range(FFN_SPLIT):
        rows = slice(r * half, (r + 1) * half)
        x1 = x_ref[rows, :] + _rms(m_ref[rows, :].astype(F32), gpm_ref[...])
        h = _rms(x1, gpf_ref[...]).astype(BF16)
        gate = _dot(h, wg_ref[...])
        up = _dot(h, wu_ref[...])
        act = (gate * jax.nn.sigmoid(gate) * up).astype(BF16)
        f = _dot(act, wd_ref[...])
        o_ref[rows, :] = x1 + _rms(f, gof_ref[...])


def _post_ffn(x, m, gains, ffn_w, *, rows, name):
    gpm, gpf, gof = gains
    wg, wu, wd = ffn_w
    row_spec = pl.BlockSpec((rows, D_MODEL), lambda i: (i, 0))
    return pl.pallas_call(
        _post_ffn_kernel,
        grid=(x.shape[0] // rows,),
        in_specs=[row_spec, row_spec,
                  _const_spec((1, D_MODEL)), _const_spec((1, D_MODEL)), _const_spec((1, D_MODEL)),
                  _const_spec(wg.shape), _const_spec(wu.shape), _const_spec(wd.shape)],
        out_specs=row_spec,
        out_shape=jax.ShapeDtypeStruct(x.shape, F32),
        compiler_params=_params(1),
        name=name,
    )(x, m, gpm, gpf, gof, wg, wu, wd)


CONV_HALO = 32
CONV_ROWS = 64
PW_SLABS = 4


def _layer_norm_silu(c, lng_ref, lnb_ref):
    mu = jnp.mean(c, axis=-1, keepdims=True)
    cc = c - mu
    var = jnp.mean(cc * cc, axis=-1, keepdims=True)
    cn = cc * lax.rsqrt(var + LN_EPS) * lng_ref[...] + lnb_ref[...]
    return cn * jax.nn.sigmoid(cn)


def _conv_prompt_kernel(x_ref, g_ref, w1s_ref, b1_ref, wdw_ref, bdw_ref, lng_ref, lnb_ref, w2s_ref, b2_ref,
                        m_ref, tail_ref, gp_scr, c_scr, slab_scr, h_scr, a_scr, act_scr, mo_scr, *, blk, nb):
    i = pl.program_id(0)
    halo = CONV_HALO * nb
    tail = CONV_TAIL * nb
    half = blk // 2
    chunks = half // CONV_ROWS

    @pl.when(i == 0)
    def _():
        gp_scr[0:halo, :] = jnp.zeros((halo, D_MODEL), F32)

    def glu(a):
        a = a + b1_ref[...]
        return a[:, :D_MODEL] * jax.nn.sigmoid(a[:, D_MODEL:])

    def conv_chunk(base):
        for l in range(D_SLABS):
            lanes = slice(l * LANE, (l + 1) * LANE)
            xs = gp_scr[pl.ds(base + halo - tail, CONV_ROWS + tail), lanes]
            acc = jnp.broadcast_to(bdw_ref[:, lanes], (CONV_ROWS, LANE))
            for j in range(CONV_WIDTH):
                wj = wdw_ref[j * SUBLANE:(j + 1) * SUBLANE, lanes]
                wj = jnp.concatenate([wj] * (CONV_ROWS // SUBLANE), axis=0)
                acc = acc + wj * xs[j * nb:j * nb + CONV_ROWS]
            c_scr[pl.ds(base, CONV_ROWS), lanes] = acc

    x = _load_time_major(x_ref, slab_scr, nb)
    h_scr[...] = _rms(x, g_ref[...]).astype(BF16)
    h_a = h_scr[0:half, :]
    a_a = jnp.concatenate([_dot(h_a, w1s_ref[s]) for s in range(PW_SLABS)], axis=1)
    gp_scr[halo:halo + half, :] = glu(a_a)

    def first_half(ci, carry):
        conv_chunk(pl.multiple_of(ci * CONV_ROWS, CONV_ROWS))
        a_scr[ci] = _dot(h_scr[half:blk, :], w1s_ref[ci])
        return carry

    lax.fori_loop(0, chunks, first_half, 0)
    a_b = jnp.concatenate([a_scr[s] for s in range(PW_SLABS)], axis=1)
    gp_scr[halo + half:halo + blk, :] = glu(a_b)
    act_scr[...] = _layer_norm_silu(c_scr[0:half, :], lng_ref, lnb_ref).astype(BF16)

    def second_half(ci, carry):
        conv_chunk(pl.multiple_of(half + ci * CONV_ROWS, CONV_ROWS))
        mo_scr[ci] = _dot(act_scr[...], w2s_ref[ci])
        return carry

    lax.fori_loop(0, chunks, second_half, 0)
    act_b = _layer_norm_silu(c_scr[half:blk, :], lng_ref, lnb_ref).astype(BF16)
    m_a = jnp.concatenate([mo_scr[s] for s in range(PW_SLABS)], axis=1)
    m_b = jnp.concatenate([_dot(act_b, w2s_ref[s]) for s in range(PW_SLABS)], axis=1)
    _store_sequence_major(jnp.concatenate([m_a, m_b], axis=0) + b2_ref[...], slab_scr, m_ref, nb)
    tail_ref[...] = gp_scr[halo + blk - tail:halo + blk, :]
    gp_scr[0:halo, :] = gp_scr[blk:blk + halo, :]


def _prompt_conv(x, g_pre, cw, *, steps):
    w1, b1, wdw8, bdw, lng, lnb, w2, b2 = cw
    nb, seq, _ = x.shape
    blk = steps * nb
    assert steps >= CONV_HALO and blk // 2 == PW_SLABS * CONV_ROWS
    w1s = jnp.swapaxes(w1.reshape(D_MODEL, PW_SLABS, -1), 0, 1)
    w2s = jnp.swapaxes(w2.reshape(D_MODEL, PW_SLABS, -1), 0, 1)
    vec = lambda n: _const_spec((1, n))
    seq_spec = pl.BlockSpec((nb, steps, D_MODEL), lambda i: (0, i, 0))
    return pl.pallas_call(
        functools.partial(_conv_prompt_kernel, blk=blk, nb=nb),
        grid=(seq // steps,),
        in_specs=[
            seq_spec,
            vec(D_MODEL), _const_spec(w1s.shape), vec(2 * D_MODEL),
            _const_spec(wdw8.shape), vec(D_MODEL), vec(D_MODEL), vec(D_MODEL),
            _const_spec(w2s.shape), vec(D_MODEL),
        ],
        out_specs=[
            seq_spec,
            pl.BlockSpec((CONV_TAIL * nb, D_MODEL), lambda i: (0, 0)),
        ],
        out_shape=[
            jax.ShapeDtypeStruct((nb, seq, D_MODEL), BF16),
            jax.ShapeDtypeStruct((CONV_TAIL * nb, D_MODEL), F32),
        ],
        scratch_shapes=[pltpu.VMEM((blk + CONV_HALO * nb, D_MODEL), F32),
                        pltpu.VMEM((blk, D_MODEL), F32),
                        pltpu.VMEM((D_SLABS, blk, LANE), F32),
                        pltpu.VMEM((blk, D_MODEL), BF16),
                        pltpu.VMEM((PW_SLABS, blk // 2, 2 * D_MODEL // PW_SLABS), F32),
                        pltpu.VMEM((blk // 2, D_MODEL), BF16),
                        pltpu.VMEM((PW_SLABS, blk // 2, D_MODEL // PW_SLABS), F32)],
        compiler_params=_params(1),
        name="prompt_conv",
    )(x, g_pre, w1s, b1, wdw8, bdw, lng, lnb, w2s, b2)


def _sample_l0_kernel(sink_ref, x_ref, g_ref, w_ref, ck_ref, cv_ref, sre_ref, sim_ref,
                      bb_ref, cc_ref, are_ref, aim_ref, d_ref, wglu_ref, wout_ref,
                      m_ref, ko_ref, vo_ref, ore_ref, oim_ref,
                      q_scr, k_scr, v_scr, a_scr, u_scr, y_scr, *, nseq, ntok):
    qw = N_Q_HEADS * LANE
    h = _rms(x_ref[...], g_ref[...]).astype(BF16)
    z = _dot(h, w_ref[...])
    q_scr[...] = z[:, :qw] * (HEAD_DIM ** -0.5)
    kn = z[:, qw:qw + KV_WIDTH]
    vn = z[:, qw + KV_WIDTH:qw + 2 * KV_WIDTH]
    k_scr[...] = kn
    v_scr[...] = vn
    for j in range(N_STRIP):
        u_scr[j] = z[:, qw + 2 * KV_WIDTH + j * LANE:qw + 2 * KV_WIDTH + (j + 1) * LANE]
    y_scr[...] = jnp.zeros(y_scr.shape, F32)

    for j in range(N_STRIP):
        hr = sre_ref[:, j * STRIP_STATES:(j + 1) * STRIP_STATES]
        hi = sim_ref[:, j * STRIP_STATES:(j + 1) * STRIP_STATES]
        are = are_ref[j:j + 1, :]
        aim = aim_ref[j:j + 1, :]
        dj = d_ref[:, j * LANE:(j + 1) * LANE]
        for t in range(ntok):
            ut = u_scr[j, pl.ds(t, nseq, stride=SAMPLE_T), :]
            bu = _dot(ut.astype(BF16), bb_ref[j])
            hr, hi = (are * hr - aim * hi + bu[:, :STRIP_STATES],
                      are * hi + aim * hr + bu[:, STRIP_STATES:])
            hcat = jnp.concatenate([hr, hi], axis=1).astype(BF16)
            y_scr[j, pl.ds(t, nseq, stride=SAMPLE_T), :] = _dot(hcat, cc_ref[j]) + dj * ut
        ore_ref[:, j * STRIP_STATES:(j + 1) * STRIP_STATES] = hr
        oim_ref[:, j * STRIP_STATES:(j + 1) * STRIP_STATES] = hi
    s = _gelu_glu(jnp.concatenate([y_scr[j] for j in range(N_STRIP)], axis=1), wglu_ref)

    nkeys = WINDOW + SAMPLE_T
    shape = (N_Q_HEADS * SAMPLE_T, nkeys)
    tq = jnp.bitwise_and(lax.broadcasted_iota(jnp.int32, shape, 0), SAMPLE_T - 1)
    col = lax.broadcasted_iota(jnp.int32, shape, 1)
    ninf = jnp.float32(-jnp.inf)
    bias = jnp.where(col < WINDOW, jnp.where(col > tq, 0.0, ninf),
                     jnp.where(col - WINDOW <= tq, 0.0, ninf))
    sink = jnp.concatenate(
        [jnp.full((SAMPLE_T, 1), sink_ref[hh], F32) for hh in range(N_Q_HEADS)], axis=0)
    left = lax.broadcasted_iota(jnp.int32, (SAMPLE_T, LANE), 1) < HEAD_DIM

    def seq_body(n, carry):
        r = pl.multiple_of(n * SAMPLE_T, SAMPLE_T)
        qs = jnp.concatenate(
            [q_scr[pl.ds(r, SAMPLE_T), hh * LANE:(hh + 1) * LANE] for hh in range(N_Q_HEADS)], axis=0)
        kall = jnp.concatenate([ck_ref[n], k_scr[pl.ds(r, SAMPLE_T), :]], axis=0)
        vall = jnp.concatenate([cv_ref[n], v_scr[pl.ds(r, SAMPLE_T), :]], axis=0)
        ko_ref[n] = kall[ntok:ntok + WINDOW]
        vo_ref[n] = vall[ntok:ntok + WINDOW]
        sc = _dot_nt(qs.astype(BF16), kall.astype(BF16)) + bias
        mx = jnp.maximum(jnp.max(sc, axis=1, keepdims=True), sink)
        p = jnp.exp(sc - mx)
        den = jnp.sum(p, axis=1, keepdims=True) + jnp.exp(sink - mx)
        p = (p / den).astype(BF16)
        o1 = _dot(p, vall.astype(BF16))
        o2 = _dot(p, pltpu.roll(vall, HEAD_DIM, axis=1).astype(BF16))
        slabs = []
        for hp in range(N_Q_HEADS // 2):
            lo, ro = (o1, o2) if hp < Q_PER_KV // 2 else (o2, o1)
            lrow = 2 * hp * SAMPLE_T
            slabs.append(jnp.where(left, lo[lrow:lrow + SAMPLE_T], ro[lrow + SAMPLE_T:lrow + 2 * SAMPLE_T]))
        a_scr[pl.ds(r, SAMPLE_T), :] = jnp.concatenate(slabs, axis=1)
        return carry

    lax.fori_loop(0, nseq, seq_body, 0, unroll=SAMPLE_ATTN_UNROLL)
    m = (_dot(a_scr[...].astype(BF16), wout_ref[0:ATTN_WIDTH, :])
         + _dot(s.astype(BF16), wout_ref[ATTN_WIDTH:, :]))
    m_ref[...] = m.astype(m_ref.dtype)


def _sample_l0(x, g_pre, w_in_s, sinks, ck, cv, s_re, s_im, tabs, d_skip, w_glu, w_out, *, nseq, ntok):
    a_re, a_im, bb, cc = tabs
    rows_all = x.shape[0]
    rows = nseq * SAMPLE_T
    nstate = SSM_GROUPS * SSM_STATE
    row_spec = lambda w: pl.BlockSpec((rows, w), lambda i: (i, 0))
    return pl.pallas_call(
        functools.partial(_sample_l0_kernel, nseq=nseq, ntok=ntok),
        grid=(rows_all // rows,),
        in_specs=[
            pl.BlockSpec(memory_space=pltpu.SMEM),
            row_spec(D_MODEL), _const_spec((1, D_MODEL)), _const_spec(w_in_s.shape),
            pl.BlockSpec((nseq, WINDOW, KV_WIDTH), lambda i: (i, 0, 0)),
            pl.BlockSpec((nseq, WINDOW, KV_WIDTH), lambda i: (i, 0, 0)),
            pl.BlockSpec((nseq, nstate), lambda i: (i, 0)),
            pl.BlockSpec((nseq, nstate), lambda i: (i, 0)),
            _const_spec(bb.shape), _const_spec(cc.shape),
            _const_spec(a_re.shape), _const_spec(a_im.shape),
            _const_spec((1, SSM_WIDTH)),
            _const_spec(w_glu.shape), _const_spec(w_out.shape),
        ],
        out_specs=[
            row_spec(D_MODEL),
            pl.BlockSpec((nseq, WINDOW, KV_WIDTH), lambda i: (i, 0, 0)),
            pl.BlockSpec((nseq, WINDOW, KV_WIDTH), lambda i: (i, 0, 0)),
            pl.BlockSpec((nseq, nstate), lambda i: (i, 0)),
            pl.BlockSpec((nseq, nstate), lambda i: (i, 0)),
        ],
        out_shape=[
            jax.ShapeDtypeStruct((rows_all, D_MODEL), BF16),
            jax.ShapeDtypeStruct(ck.shape, F32),
            jax.ShapeDtypeStruct(cv.shape, F32),
            jax.ShapeDtypeStruct((rows_all // SAMPLE_T, nstate), F32),
            jax.ShapeDtypeStruct((rows_all // SAMPLE_T, nstate), F32),
        ],
        scratch_shapes=[
            pltpu.VMEM((rows, N_Q_HEADS * LANE), F32),
            pltpu.VMEM((rows, KV_WIDTH), F32),
            pltpu.VMEM((rows, KV_WIDTH), F32),
            pltpu.VMEM((rows, ATTN_WIDTH), F32),
            pltpu.VMEM((N_STRIP, rows, LANE), F32),
            pltpu.VMEM((N_STRIP, rows, LANE), F32),
        ],
        compiler_params=_params(1),
        name="sample_l0",
    )(sinks, x, g_pre, w_in_s, ck, cv, s_re, s_im, bb, cc, a_re, a_im, d_skip, w_glu, w_out)


def _sample_l1_kernel(x_ref, g_ref, w1_ref, b1_ref, st_ref, wst_ref, wnew_ref, bdw_ref, lng_ref, lnb_ref,
                      w2_ref, b2_ref, m_ref, sto_ref, c_scr, *, nseq, ntok):
    h = _rms(x_ref[...], g_ref[...]).astype(BF16)
    a = _dot(h, w1_ref[...]) + b1_ref[...]
    gg = a[:, :D_MODEL] * jax.nn.sigmoid(a[:, D_MODEL:])
    g3 = gg.reshape(nseq, SAMPLE_T, D_MODEL)
    st = st_ref[...]
    sto_ref[...] = jnp.concatenate([st[:, ntok:, :], g3[:, :ntok, :]], axis=1)
    c_scr[...] = jnp.zeros(c_scr.shape, F32)
    for t in range(ntok):
        ct = (jnp.sum(st * wst_ref[t][None], axis=1) + jnp.sum(g3 * wnew_ref[t][None], axis=1)
              + bdw_ref[...])
        for j in range(D_SLABS):
            c_scr[j, pl.ds(t, nseq, stride=SAMPLE_T), :] = ct[:, j * LANE:(j + 1) * LANE]
    c = jnp.concatenate([c_scr[j] for j in range(D_SLABS)], axis=1)
    act = _layer_norm_silu(c, lng_ref, lnb_ref).astype(BF16)
    m_ref[...] = (_dot(act, w2_ref[...]) + b2_ref[...]).astype(m_ref.dtype)


def _sample_l1(x, g_pre, cw, st, wst, wnew, *, nseq, ntok):
    w1, b1, _, bdw, lng, lnb, w2, b2 = cw
    rows_all = x.shape[0]
    rows = nseq * SAMPLE_T
    vec = lambda n: _const_spec((1, n))
    row_spec = pl.BlockSpec((rows, D_MODEL), lambda i: (i, 0))
    return pl.pallas_call(
        functools.partial(_sample_l1_kernel, nseq=nseq, ntok=ntok),
        grid=(rows_all // rows,),
        in_specs=[
            row_spec, vec(D_MODEL), _const_spec(w1.shape), vec(2 * D_MODEL),
            pl.BlockSpec((nseq, CONV_TAIL, D_MODEL), lambda i: (i, 0, 0)),
            _const_spec(wst.shape), _const_spec(wnew.shape),
            vec(D_MODEL), vec(D_MODEL), vec(D_MODEL), _const_spec(w2.shape), vec(D_MODEL),
        ],
        out_specs=[row_spec, pl.BlockSpec((nseq, CONV_TAIL, D_MODEL), lambda i: (i, 0, 0))],
        out_shape=[
            jax.ShapeDtypeStruct((rows_all, D_MODEL), BF16),
            jax.ShapeDtypeStruct(st.shape, F32),
        ],
        scratch_shapes=[pltpu.VMEM((D_SLABS, rows, LANE), F32)],
        compiler_params=_params(1),
        name="sample_l1",
    )(x, g_pre, w1, b1, st, wst, wnew, bdw, lng, lnb, w2, b2)


PROMPT_FFN_ROWS = 512
PROMPT_STEPS = 64
SAMPLE_SEQS = 32
SAMPLE_FFN_ROWS = 512
SAMPLE_ATTN_UNROLL = 8


def kernel(x_prompt, x_sample, cache_k, cache_v, state_s5_re, state_s5_im, state_conv, norm_pre_mix, norm_post_mix, norm_pre_ffn, norm_post_ffn, w_in_ab, attn_sinks, s5_lambda_re, s5_lambda_im, s5_log_step, s5_b_re, s5_b_im, s5_c_re, s5_c_im, s5_d, w_glu, w_out_ab, w_pw1, b_pw1, w_dw, b_dw, conv_ln_g, conv_ln_b, w_pw2, b_pw2, w_ffn_gate, w_ffn_up, w_ffn_down):
    nb, seq, _ = x_prompt.shape
    ns, ntok, _ = x_sample.shape
    assert nb == SUBLANE and ntok <= SAMPLE_T and seq % WINDOW == 0 and seq % PROMPT_STEPS == 0
    row = lambda v: v.reshape(1, -1)
    gains = lambda layer: tuple(row(n[layer]) for n in (norm_post_mix, norm_pre_ffn, norm_post_ffn))

    w_in = w_in_ab[0].astype(BF16)
    wq = w_in_ab[0][:, :ATTN_WIDTH].reshape(D_MODEL, N_Q_HEADS, HEAD_DIM)
    lane_kv = (jnp.arange(LANE) // HEAD_DIM)[None, :]
    head_kv = (jnp.arange(N_Q_HEADS) // Q_PER_KV)[:, None]
    wq_slab = jnp.where((lane_kv == head_kv)[None], jnp.concatenate([wq, wq], axis=2), 0.0)
    w_in_s = jnp.concatenate(
        [wq_slab.reshape(D_MODEL, N_Q_HEADS * LANE), w_in_ab[0][:, ATTN_WIDTH:]], axis=1).astype(BF16)
    w_glu_b = w_glu[0].astype(BF16)
    w_out = w_out_ab[0].astype(BF16)
    ffn_w = [(w_ffn_gate[l].astype(BF16), w_ffn_up[l].astype(BF16), w_ffn_down[l].astype(BF16))
             for l in range(2)]
    sinks = attn_sinks[0]
    tabs = _s5_tables(s5_lambda_re[0], s5_lambda_im[0], s5_log_step[0], s5_b_re[0], s5_b_im[0],
                      s5_c_re[0], s5_c_im[0])
    d_skip = row(s5_d[0])
    wdw = w_dw[0]
    conv_w = (w_pw1[0].astype(BF16), row(b_pw1[0]), jnp.repeat(wdw, SUBLANE, axis=0), row(b_dw[0]),
              row(conv_ln_g[0]), row(conv_ln_b[0]), w_pw2[0].astype(BF16), row(b_pw2[0]))
    tt = jnp.arange(ntok)[:, None]
    tap_st = jnp.arange(CONV_TAIL)[None, :] - tt
    wst = jnp.where((tap_st >= 0)[..., None], wdw[jnp.clip(tap_st, 0, CONV_WIDTH - 1)], 0.0)
    tap_new = CONV_TAIL + jnp.arange(SAMPLE_T)[None, :] - tt
    wnew = jnp.where((tap_new <= CONV_TAIL)[..., None], wdw[jnp.clip(tap_new, 0, CONV_WIDTH - 1)], 0.0)

    flat = lambda v: v.reshape(nb * seq, D_MODEL)
    au_p, kk_p, vk_p = _prompt_proj_attn(x_prompt, row(norm_pre_mix[0]), w_in, sinks)
    m0_p, hfin_p = _prompt_s5(au_p, tabs, d_skip, w_glu_b, w_out, nb=nb, steps=PROMPT_STEPS)
    x1_p = _post_ffn(flat(x_prompt), flat(m0_p), gains(0), ffn_w[0], rows=PROMPT_FFN_ROWS, name="prompt_ffn0")
    m1_p, tail_p = _prompt_conv(x1_p.reshape(nb, seq, D_MODEL), row(norm_pre_mix[1]), conv_w, steps=PROMPT_STEPS)
    y_prompt = _post_ffn(x1_p, flat(m1_p), gains(1), ffn_w[1], rows=PROMPT_FFN_ROWS, name="prompt_ffn1")
    y_prompt = y_prompt.reshape(nb, seq, D_MODEL)

    xs = jnp.pad(x_sample, ((0, 0), (0, SAMPLE_T - ntok), (0, 0))).reshape(ns * SAMPLE_T, D_MODEL)
    nstate = SSM_GROUPS * SSM_STATE
    m0_s, ko_s, vo_s, sre_s, sim_s = _sample_l0(
        xs, row(norm_pre_mix[0]), w_in_s, sinks,
        cache_k.reshape(ns, WINDOW, KV_WIDTH), cache_v.reshape(ns, WINDOW, KV_WIDTH),
        state_s5_re.reshape(ns, nstate), state_s5_im.reshape(ns, nstate),
        tabs, d_skip, w_glu_b, w_out, nseq=SAMPLE_SEQS, ntok=ntok)
    x1_s = _post_ffn(xs, m0_s, gains(0), ffn_w[0], rows=SAMPLE_FFN_ROWS, name="sample_ffn0")
    m1_s, sto_s = _sample_l1(x1_s, row(norm_pre_mix[1]), conv_w, state_conv.reshape(ns, CONV_TAIL, D_MODEL), wst, wnew,
                           nseq=SAMPLE_SEQS, ntok=ntok)
    y_s = _post_ffn(x1_s, m1_s, gains(1), ffn_w[1], rows=SAMPLE_FFN_ROWS, name="sample_ffn1")

    y_sample = y_s.reshape(ns, SAMPLE_T, D_MODEL)[:, :ntok]
    k_prompt = kk_p.reshape(1, nb, WINDOW, N_KV_HEADS, HEAD_DIM)
    v_prompt = vk_p.reshape(1, nb, WINDOW, N_KV_HEADS, HEAD_DIM)
    hfin = jnp.swapaxes(hfin_p, 0, 1)
    s5_re_prompt = hfin[:, :, :STRIP_STATES].reshape(1, nb, SSM_GROUPS, SSM_STATE)
    s5_im_prompt = hfin[:, :, STRIP_STATES:].reshape(1, nb, SSM_GROUPS, SSM_STATE)
    conv_prompt = jnp.swapaxes(tail_p.reshape(CONV_TAIL, nb, D_MODEL), 0, 1)[None]
    k_sample = ko_s.reshape(1, ns, WINDOW, N_KV_HEADS, HEAD_DIM)
    v_sample = vo_s.reshape(1, ns, WINDOW, N_KV_HEADS, HEAD_DIM)
    s5_re_sample = sre_s.reshape(1, ns, SSM_GROUPS, SSM_STATE)
    s5_im_sample = sim_s.reshape(1, ns, SSM_GROUPS, SSM_STATE)
    conv_sample = sto_s.reshape(1, ns, CONV_TAIL, D_MODEL)
    return (y_prompt, y_sample, k_prompt, v_prompt, s5_re_prompt, s5_im_prompt, conv_prompt,
            k_sample, v_sample, s5_re_sample, s5_im_sample, conv_sample)
```

```python
import functools

import jax
import jax.numpy as jnp
from jax import lax
from jax.experimental import pallas as pl
from jax.experimental.pallas import tpu as pltpu

F32 = jnp.float32
BF16 = jnp.bfloat16

LANE = 128
SUBLANE = 8
VMEM_LIMIT_BYTES = 56 * 1024 * 1024

D_MODEL = 1024
HEAD_DIM = 64
N_Q_HEADS = 8
N_KV_HEADS = 2
Q_PER_KV = N_Q_HEADS // N_KV_HEADS
WINDOW = 128
ATTN_WIDTH = N_Q_HEADS * HEAD_DIM
KV_WIDTH = N_KV_HEADS * HEAD_DIM
SSM_WIDTH = 512
SSM_GROUP = 16
SSM_GROUPS = 32
SSM_STATE = 64
D_FF = 2816
CONV_WIDTH = 31
CONV_TAIL = CONV_WIDTH - 1
RMS_EPS = 1e-6
LN_EPS = 1e-5

N_STRIP = SSM_WIDTH // LANE
STRIP_STATES = (SSM_GROUPS // N_STRIP) * SSM_STATE
D_SLABS = D_MODEL // LANE
ATTN_SLABS = ATTN_WIDTH // LANE
FFN_SPLIT = 4

SAMPLE_T = SUBLANE


def _rms(x, g):
    return x * lax.rsqrt(jnp.mean(x * x, axis=-1, keepdims=True) + RMS_EPS) * g


def _dot(a, b):
    return jnp.dot(a, b, preferred_element_type=F32)


def _dot_nt(a, b):
    return lax.dot_general(a, b, (((1,), (1,)), ((), ())), preferred_element_type=F32)


def _const_spec(shape):
    nd = len(shape)
    return pl.BlockSpec(shape, lambda *_: (0,) * nd, pipeline_mode=pl.Buffered(1))


def _params(n_axes):
    return pltpu.CompilerParams(
        dimension_semantics=("arbitrary",) * n_axes, vmem_limit_bytes=VMEM_LIMIT_BYTES)


def _load_time_major(x_ref, slab_scr, nb):
    steps = x_ref.shape[1]
    nslab = x_ref.shape[2] // LANE
    for b in range(nb):
        for j in range(nslab):
            slab_scr[j, pl.ds(b, steps, stride=nb), :] = x_ref[b, :, j * LANE:(j + 1) * LANE]
    return jnp.concatenate([slab_scr[j] for j in range(nslab)], axis=1)


def _store_sequence_major(val, slab_scr, out_ref, nb):
    steps = out_ref.shape[1]
    nslab = out_ref.shape[2] // LANE
    for j in range(nslab):
        slab_scr[j] = val[:, j * LANE:(j + 1) * LANE]
    for b in range(nb):
        for j in range(nslab):
            out_ref[b, :, j * LANE:(j + 1) * LANE] = (
                slab_scr[j, pl.ds(b, steps, stride=nb), :].astype(out_ref.dtype))


def _s5_disc_kernel(lre_ref, lim_ref, ls_ref, bre_ref, bim_ref, are_ref, aim_ref, bbre_ref, bbim_ref):
    lam_re = lre_ref[...]
    lam_im = lim_ref[...]
    dt = jnp.exp(ls_ref[...])
    mag = jnp.exp(lam_re * dt)
    lb_re = mag * jnp.cos(lam_im * dt)
    lb_im = mag * jnp.sin(lam_im * dt)
    den = lam_re * lam_re + lam_im * lam_im
    coef_re = ((lb_re - 1.0) * lam_re + lb_im * lam_im) / den
    coef_im = (lb_im * lam_re - (lb_re - 1.0) * lam_im) / den
    b_re = bre_ref[...]
    b_im = bim_ref[...]
    are_ref[...] = lb_re
    aim_ref[...] = lb_im
    bbre_ref[...] = coef_re * b_re - coef_im * b_im
    bbim_ref[...] = coef_re * b_im + coef_im * b_re


def _s5_tables(lam_re, lam_im, log_step, b_re, b_im, c_re, c_im):
    rep = lambda v: jnp.repeat(v, SSM_GROUP, axis=0)
    rows = SSM_GROUPS * SSM_GROUP
    bt = lambda b: jnp.swapaxes(b, 1, 2).reshape(rows, SSM_STATE)
    ls = jnp.broadcast_to(log_step[:, None], (SSM_GROUPS, SSM_STATE))
    shp = jax.ShapeDtypeStruct((rows, SSM_STATE), F32)
    a_re, a_im, bb_re, bb_im = pl.pallas_call(
        _s5_disc_kernel, out_shape=(shp, shp, shp, shp), name="s5_disc",
    )(rep(lam_re), rep(lam_im), rep(ls), bt(b_re), bt(b_im))
    a_re = a_re[::SSM_GROUP].reshape(N_STRIP, STRIP_STATES)
    a_im = a_im[::SSM_GROUP].reshape(N_STRIP, STRIP_STATES)
    gl = SSM_GROUPS // N_STRIP
    eye = jnp.eye(gl, dtype=F32).reshape(1, gl, 1, gl, 1)

    def in_blockdiag(m):
        m = m.reshape(N_STRIP, gl, SSM_GROUP, 1, SSM_STATE)
        return (m * eye).reshape(N_STRIP, LANE, STRIP_STATES)

    def out_blockdiag(m):
        m = jnp.swapaxes(m.reshape(N_STRIP, gl, SSM_GROUP, SSM_STATE), 2, 3)
        return (m.reshape(N_STRIP, gl, SSM_STATE, 1, SSM_GROUP) * eye).reshape(N_STRIP, STRIP_STATES, LANE)

    bb = jnp.concatenate([in_blockdiag(bb_re), in_blockdiag(bb_im)], axis=2).astype(BF16)
    cc = jnp.concatenate([out_blockdiag(c_re), -out_blockdiag(c_im)], axis=1).astype(BF16)
    return a_re, a_im, bb, cc


def _gelu_glu(y, wglu_ref):
    g = jax.nn.gelu(y)
    return g * jax.nn.sigmoid(_dot(g.astype(BF16), wglu_ref[...]))


def _band_attention(q, kv_scr, sinks, from_prev, prev_bias, seqs):
    gw = Q_PER_KV * HEAD_DIM
    left = lax.broadcasted_iota(jnp.int32, (2 * WINDOW, KV_WIDTH), 1) < HEAD_DIM
    head_of_lane = lax.shift_right_logical(
        lax.broadcasted_iota(jnp.int32, (WINDOW, gw), 1), HEAD_DIM.bit_length() - 1)
    scores, values = [], []
    for n, b in enumerate(seqs):
        kb = kv_scr[b, :, 0:KV_WIDTH]
        vb = kv_scr[b, :, KV_WIDTH:2 * KV_WIDTH]
        kr = pltpu.roll(kb, HEAD_DIM, axis=1)
        vr = pltpu.roll(vb, HEAD_DIM, axis=1)
        for g in range(N_KV_HEADS):
            kk = jnp.where(left, kb, kr) if g == 0 else jnp.where(left, kr, kb)
            vv = jnp.where(left, vb, vr) if g == 0 else jnp.where(left, vr, vb)
            k4 = jnp.concatenate([kk, kk], axis=1).astype(BF16)
            values.append(jnp.concatenate([vv, vv], axis=1).astype(BF16))
            qg = q[n * WINDOW:(n + 1) * WINDOW, g * gw:(g + 1) * gw]
            qst = jnp.concatenate(
                [jnp.where(head_of_lane == r, qg, 0.0) for r in range(Q_PER_KV)], axis=0).astype(BF16)
            scores.append(_dot_nt(qst, k4))
    nblk = len(scores)
    s = jnp.concatenate(scores, axis=0)
    prev = jnp.concatenate([from_prev] * nblk, axis=0)
    s = jnp.where(prev, s[:, :WINDOW] + prev_bias, s[:, WINDOW:])
    sink = jnp.concatenate(
        [jnp.full((WINDOW, WINDOW), sinks[(n % N_KV_HEADS) * Q_PER_KV + r], F32)
         for n in range(nblk) for r in range(Q_PER_KV)], axis=0)
    m = jnp.maximum(jnp.max(s, axis=1, keepdims=True), sink)
    p = jnp.exp(s - m)
    pb = p.astype(BF16)
    hrows = pb.shape[0] // 2
    ri = lax.broadcasted_iota(jnp.int32, (2 * WINDOW, 2 * WINDOW), 0) < WINDOW
    ci = lax.broadcasted_iota(jnp.int32, (2 * WINDOW, 2 * WINDOW), 1) < WINDOW
    ones2 = jnp.where(ri == ci, 1.0, 0.0).astype(BF16)
    sums = _dot(jnp.concatenate([pb[:hrows], pb[hrows:]], axis=1), ones2)
    den = jnp.concatenate([sums[:, :WINDOW], sums[:, WINDOW:]], axis=0) + jnp.exp(sink - m)
    p = p / den
    p = jnp.concatenate([jnp.where(prev, p, 0.0).astype(BF16), jnp.where(prev, 0.0, p).astype(BF16)], axis=1)
    rows = Q_PER_KV * WINDOW
    outs = []
    for b in range(len(seqs)):
        og = []
        for g in range(N_KV_HEADS):
            n = b * N_KV_HEADS + g
            o4 = _dot(p[n * rows:(n + 1) * rows], values[n])
            o = jnp.where(head_of_lane == 0, o4[0:WINDOW], 0.0)
            for r in range(1, Q_PER_KV):
                o = jnp.where(head_of_lane == r, o4[r * WINDOW:(r + 1) * WINDOW], o)
            og.append(o)
        outs.append(jnp.concatenate(og, axis=1))
    return outs


def _proj_attn_kernel(sink_ref, x_ref, g_ref, w_ref, au_ref, kk_ref, vk_ref, kv_scr, *, nb):
    i = pl.program_id(0)
    z = jnp.concatenate(
        [_dot(_rms(x_ref[b:b + 2].reshape(2 * WINDOW, D_MODEL), g_ref[...]).astype(BF16), w_ref[...])
         for b in range(0, nb, 2)], axis=0)
    ucol = ATTN_WIDTH + 2 * KV_WIDTH

    @pl.when(i == 0)
    def _():
        kv_scr[:, 0:WINDOW, :] = jnp.zeros((nb, WINDOW, 2 * KV_WIDTH), F32)

    for b in range(nb):
        kv_scr[b, WINDOW:2 * WINDOW, :] = z[b * WINDOW:(b + 1) * WINDOW, ATTN_WIDTH:ucol]
    q = z[:, :ATTN_WIDTH] * (HEAD_DIM ** -0.5)

    shape = (Q_PER_KV * WINDOW, WINDOW)
    row = jnp.bitwise_and(lax.broadcasted_iota(jnp.int32, shape, 0), WINDOW - 1)
    col = lax.broadcasted_iota(jnp.int32, shape, 1)
    prev_bias = jnp.where(i == 0, -jnp.inf, 0.0).astype(F32)
    sinks = [sink_ref[hh] for hh in range(N_Q_HEADS)]
    outs = _band_attention(q, kv_scr, sinks, col > row, prev_bias, range(nb))
    for b in range(nb):
        zb = z[b * WINDOW:(b + 1) * WINDOW]
        for j in range(ATTN_SLABS):
            au_ref[j, pl.ds(b, WINDOW, stride=nb), :] = outs[b][:, j * LANE:(j + 1) * LANE]
        for j in range(N_STRIP):
            au_ref[ATTN_SLABS + j, pl.ds(b, WINDOW, stride=nb), :] = zb[:, ucol + j * LANE:ucol + (j + 1) * LANE]
        kk_ref[b] = zb[:, ATTN_WIDTH:ATTN_WIDTH + KV_WIDTH]
        vk_ref[b] = zb[:, ATTN_WIDTH + KV_WIDTH:ucol]
    kv_scr[:, 0:WINDOW, :] = kv_scr[:, WINDOW:2 * WINDOW, :]


def _prompt_proj_attn(x, g_pre, w_in, sinks):
    nb, seq, _ = x.shape
    nslab = ATTN_SLABS + N_STRIP
    return pl.pallas_call(
        functools.partial(_proj_attn_kernel, nb=nb),
        grid=(seq // WINDOW,),
        in_specs=[
            pl.BlockSpec(memory_space=pltpu.SMEM),
            pl.BlockSpec((nb, WINDOW, D_MODEL), lambda i: (0, i, 0)),
            _const_spec((1, D_MODEL)),
            _const_spec(w_in.shape),
        ],
        out_specs=[
            pl.BlockSpec((nslab, WINDOW * nb, LANE), lambda i: (0, i, 0)),
            pl.BlockSpec((nb, WINDOW, KV_WIDTH), lambda i: (0, 0, 0)),
            pl.BlockSpec((nb, WINDOW, KV_WIDTH), lambda i: (0, 0, 0)),
        ],
        out_shape=[
            jax.ShapeDtypeStruct((nslab, seq * nb, LANE), F32),
            jax.ShapeDtypeStruct((nb, WINDOW, KV_WIDTH), F32),
            jax.ShapeDtypeStruct((nb, WINDOW, KV_WIDTH), F32),
        ],
        scratch_shapes=[pltpu.VMEM((nb, 2 * WINDOW, 2 * KV_WIDTH), F32)],
        compiler_params=_params(1),
        name="prompt_proj_attn",
    )(sinks, x, g_pre, w_in)


def _s5_prompt_kernel(au_ref, bb_ref, cc_ref, are_ref, aim_ref, d_ref, wglu_ref, wout_ref,
                      m_ref, hfin_ref, hbuf, slab_scr, *, steps, nb):
    i = pl.program_id(0)

    @pl.when(i == 0)
    def _():
        hfin_ref[...] = jnp.zeros(hfin_ref.shape, F32)

    ys = []
    for j in range(N_STRIP):
        uj = au_ref[ATTN_SLABS + j]
        hbuf[...] = _dot(uj.astype(BF16), bb_ref[j])
        are = jnp.broadcast_to(are_ref[j:j + 1, :], (nb, STRIP_STATES))
        aim = jnp.broadcast_to(aim_ref[j:j + 1, :], (nb, STRIP_STATES))

        def step(t, carry):
            hr, hi = carry
            r = pl.multiple_of(t * nb, nb)
            br = hbuf[pl.ds(r, nb), 0:STRIP_STATES]
            bi = hbuf[pl.ds(r, nb), STRIP_STATES:2 * STRIP_STATES]
            nr = are * hr - aim * hi + br
            ni = are * hi + aim * hr + bi
            hbuf[pl.ds(r, nb), 0:STRIP_STATES] = nr
            hbuf[pl.ds(r, nb), STRIP_STATES:2 * STRIP_STATES] = ni
            return nr, ni

        h0 = (hfin_ref[j, :, 0:STRIP_STATES], hfin_ref[j, :, STRIP_STATES:2 * STRIP_STATES])
        hr, hi = lax.fori_loop(0, steps, step, h0, unroll=True)
        hfin_ref[j, :, 0:STRIP_STATES] = hr
        hfin_ref[j, :, STRIP_STATES:2 * STRIP_STATES] = hi
        ys.append(_dot(hbuf[...].astype(BF16), cc_ref[j]) + d_ref[:, j * LANE:(j + 1) * LANE] * uj)
    s = _gelu_glu(jnp.concatenate(ys, axis=1), wglu_ref)
    a = jnp.concatenate([au_ref[j] for j in range(ATTN_SLABS)], axis=1).astype(BF16)
    m = _dot(a, wout_ref[0:ATTN_WIDTH, :]) + _dot(s.astype(BF16), wout_ref[ATTN_WIDTH:, :])
    _store_sequence_major(m, slab_scr, m_ref, nb)


def _prompt_s5(au, tabs, d_skip, w_glu, w_out, *, nb, steps):
    a_re, a_im, bb, cc = tabs
    nslab, rows, _ = au.shape
    blk = steps * nb
    return pl.pallas_call(
        functools.partial(_s5_prompt_kernel, steps=steps, nb=nb),
        grid=(rows // blk,),
        in_specs=[
            pl.BlockSpec((nslab, blk, LANE), lambda i: (0, i, 0)),
            _const_spec(bb.shape), _const_spec(cc.shape),
            _const_spec(a_re.shape), _const_spec(a_im.shape),
            _const_spec((1, SSM_WIDTH)),
            _const_spec(w_glu.shape), _const_spec(w_out.shape),
        ],
        out_specs=[
            pl.BlockSpec((nb, steps, D_MODEL), lambda i: (0, i, 0)),
            pl.BlockSpec((N_STRIP, nb, 2 * STRIP_STATES), lambda i: (0, 0, 0)),
        ],
        out_shape=[
            jax.ShapeDtypeStruct((nb, rows // nb, D_MODEL), BF16),
            jax.ShapeDtypeStruct((N_STRIP, nb, 2 * STRIP_STATES), F32),
        ],
        scratch_shapes=[pltpu.VMEM((blk, 2 * STRIP_STATES), F32),
                        pltpu.VMEM((D_SLABS, blk, LANE), F32)],
        compiler_params=_params(1),
        name="prompt_s5",
    )(au, bb, cc, a_re, a_im, d_skip, w_glu, w_out)


def _post_ffn_kernel(x_ref, m_ref, gpm_ref, gpf_ref, gof_ref, wg_ref, wu_ref, wd_ref, o_ref):
    group = x_ref.shape[0] // FFN_SPLIT
    for r in range(FFN_SPLIT):
        rows = slice(r * group, (r + 1) * group)
        x1 = x_ref[rows, :] + _rms(m_ref[rows, :].astype(F32), gpm_ref[...])
        h = _rms(x1, gpf_ref[...]).astype(BF16)
        gate = _dot(h, wg_ref[...])
        up = _dot(h, wu_ref[...])
        act = (gate * jax.nn.sigmoid(gate) * up).astype(BF16)
        f = _dot(act, wd_ref[...])
        o_ref[rows, :] = x1 + _rms(f, gof_ref[...])


def _post_ffn(x, m, gains, ffn_w, *, rows, name):
    gpm, gpf, gof = gains
    wg, wu, wd = ffn_w
    row_spec = pl.BlockSpec((rows, D_MODEL), lambda i: (i, 0))
    return pl.pallas_call(
        _post_ffn_kernel,
        grid=(x.shape[0] // rows,),
        in_specs=[row_spec, row_spec,
                  _const_spec((1, D_MODEL)), _const_spec((1, D_MODEL)), _const_spec((1, D_MODEL)),
                  _const_spec(wg.shape), _const_spec(wu.shape), _const_spec(wd.shape)],
        out_specs=row_spec,
        out_shape=jax.ShapeDtypeStruct(x.shape, F32),
        compiler_params=_params(1),
        name=name,
    )(x, m, gpm, gpf, gof, wg, wu, wd)


CONV_HALO = 32
CONV_ROWS = 64


def _layer_norm_silu(c, lng_ref, lnb_ref):
    mu = jnp.mean(c, axis=-1, keepdims=True)
    cc = c - mu
    var = jnp.mean(cc * cc, axis=-1, keepdims=True)
    cn = cc * lax.rsqrt(var + LN_EPS) * lng_ref[...] + lnb_ref[...]
    return cn * jax.nn.sigmoid(cn)


def _conv_prompt_kernel(x_ref, g_ref, w1_ref, b1_ref, wdw_ref, bdw_ref, lng_ref, lnb_ref, w2_ref, b2_ref,
                        m_ref, tail_ref, gp_scr, c_scr, slab_scr, *, blk, nb):
    i = pl.program_id(0)
    halo = CONV_HALO * nb
    tail = CONV_TAIL * nb

    @pl.when(i == 0)
    def _():
        gp_scr[0:halo, :] = jnp.zeros((halo, D_MODEL), F32)

    x = _load_time_major(x_ref, slab_scr, nb)
    h = _rms(x, g_ref[...]).astype(BF16)
    a = _dot(h, w1_ref[...]) + b1_ref[...]
    gp_scr[halo:halo + blk, :] = a[:, :D_MODEL] * jax.nn.sigmoid(a[:, D_MODEL:])

    def chunk(ci, carry):
        base = pl.multiple_of(ci * CONV_ROWS, CONV_ROWS)
        for l in range(D_SLABS):
            lanes = slice(l * LANE, (l + 1) * LANE)
            xs = gp_scr[pl.ds(base + halo - tail, CONV_ROWS + tail), lanes]
            acc = jnp.broadcast_to(bdw_ref[:, lanes], (CONV_ROWS, LANE))
            for j in range(CONV_WIDTH):
                wj = wdw_ref[j * SUBLANE:(j + 1) * SUBLANE, lanes]
                wj = jnp.concatenate([wj] * (CONV_ROWS // SUBLANE), axis=0)
                acc = acc + wj * xs[j * nb:j * nb + CONV_ROWS]
            c_scr[pl.ds(base, CONV_ROWS), lanes] = acc
        return carry

    lax.fori_loop(0, blk // CONV_ROWS, chunk, 0)
    act = _layer_norm_silu(c_scr[...], lng_ref, lnb_ref).astype(BF16)
    _store_sequence_major(_dot(act, w2_ref[...]) + b2_ref[...], slab_scr, m_ref, nb)
    tail_ref[...] = gp_scr[halo + blk - tail:halo + blk, :]
    gp_scr[0:halo, :] = gp_scr[blk:blk + halo, :]


def _prompt_conv(x, g_pre, cw, *, steps):
    w1, b1, wdw8, bdw, lng, lnb, w2, b2 = cw
    nb, seq, _ = x.shape
    blk = steps * nb
    assert steps >= CONV_HALO and blk % CONV_ROWS == 0
    vec = lambda n: _const_spec((1, n))
    seq_spec = pl.BlockSpec((nb, steps, D_MODEL), lambda i: (0, i, 0))
    return pl.pallas_call(
        functools.partial(_conv_prompt_kernel, blk=blk, nb=nb),
        grid=(seq // steps,),
        in_specs=[
            seq_spec,
            vec(D_MODEL), _const_spec(w1.shape), vec(2 * D_MODEL),
            _const_spec(wdw8.shape), vec(D_MODEL), vec(D_MODEL), vec(D_MODEL),
            _const_spec(w2.shape), vec(D_MODEL),
        ],
        out_specs=[
            seq_spec,
            pl.BlockSpec((CONV_TAIL * nb, D_MODEL), lambda i: (0, 0)),
        ],
        out_shape=[
            jax.ShapeDtypeStruct((nb, seq, D_MODEL), BF16),
            jax.ShapeDtypeStruct((CONV_TAIL * nb, D_MODEL), F32),
        ],
        scratch_shapes=[pltpu.VMEM((blk + CONV_HALO * nb, D_MODEL), F32),
                        pltpu.VMEM((blk, D_MODEL), F32),
                        pltpu.VMEM((D_SLABS, blk, LANE), F32)],
        compiler_params=_params(1),
        name="prompt_conv",
    )(x, g_pre, w1, b1, wdw8, bdw, lng, lnb, w2, b2)


def _sample_l0_kernel(sink_ref, x_ref, g_ref, w_ref, ck_ref, cv_ref, sre_ref, sim_ref,
                      bb_ref, cc_ref, are_ref, aim_ref, d_ref, wglu_ref, wout_ref,
                      m_ref, ko_ref, vo_ref, ore_ref, oim_ref,
                      q_scr, k_scr, v_scr, a_scr, u_scr, y_scr, *, nseq, ntok):
    qw = N_Q_HEADS * LANE
    h = _rms(x_ref[...], g_ref[...]).astype(BF16)
    z = _dot(h, w_ref[...])
    q_scr[...] = z[:, :qw] * (HEAD_DIM ** -0.5)
    kn = z[:, qw:qw + KV_WIDTH]
    vn = z[:, qw + KV_WIDTH:qw + 2 * KV_WIDTH]
    k_scr[...] = kn
    v_scr[...] = vn
    for j in range(N_STRIP):
        u_scr[j] = z[:, qw + 2 * KV_WIDTH + j * LANE:qw + 2 * KV_WIDTH + (j + 1) * LANE]
    y_scr[...] = jnp.zeros(y_scr.shape, F32)

    for j in range(N_STRIP):
        hr = sre_ref[:, j * STRIP_STATES:(j + 1) * STRIP_STATES]
        hi = sim_ref[:, j * STRIP_STATES:(j + 1) * STRIP_STATES]
        are = are_ref[j:j + 1, :]
        aim = aim_ref[j:j + 1, :]
        dj = d_ref[:, j * LANE:(j + 1) * LANE]
        for t in range(ntok):
            ut = u_scr[j, pl.ds(t, nseq, stride=SAMPLE_T), :]
            bu = _dot(ut.astype(BF16), bb_ref[j])
            hr, hi = (are * hr - aim * hi + bu[:, :STRIP_STATES],
                      are * hi + aim * hr + bu[:, STRIP_STATES:])
            hcat = jnp.concatenate([hr, hi], axis=1).astype(BF16)
            y_scr[j, pl.ds(t, nseq, stride=SAMPLE_T), :] = _dot(hcat, cc_ref[j]) + dj * ut
        ore_ref[:, j * STRIP_STATES:(j + 1) * STRIP_STATES] = hr
        oim_ref[:, j * STRIP_STATES:(j + 1) * STRIP_STATES] = hi
    s = _gelu_glu(jnp.concatenate([y_scr[j] for j in range(N_STRIP)], axis=1), wglu_ref)

    nkeys = WINDOW + SAMPLE_T
    shape = (N_Q_HEADS * SAMPLE_T, nkeys)
    tq = jnp.bitwise_and(lax.broadcasted_iota(jnp.int32, shape, 0), SAMPLE_T - 1)
    col = lax.broadcasted_iota(jnp.int32, shape, 1)
    ninf = jnp.float32(-jnp.inf)
    bias = jnp.where(col < WINDOW, jnp.where(col > tq, 0.0, ninf),
                     jnp.where(col - WINDOW <= tq, 0.0, ninf))
    sink = jnp.concatenate(
        [jnp.full((SAMPLE_T, 1), sink_ref[hh], F32) for hh in range(N_Q_HEADS)], axis=0)
    left = lax.broadcasted_iota(jnp.int32, (SAMPLE_T, LANE), 1) < HEAD_DIM

    def seq_body(n, carry):
        r = pl.multiple_of(n * SAMPLE_T, SAMPLE_T)
        qs = jnp.concatenate(
            [q_scr[pl.ds(r, SAMPLE_T), hh * LANE:(hh + 1) * LANE] for hh in range(N_Q_HEADS)], axis=0)
        kall = jnp.concatenate([ck_ref[n], k_scr[pl.ds(r, SAMPLE_T), :]], axis=0)
        vall = jnp.concatenate([cv_ref[n], v_scr[pl.ds(r, SAMPLE_T), :]], axis=0)
        ko_ref[n] = kall[ntok:ntok + WINDOW]
        vo_ref[n] = vall[ntok:ntok + WINDOW]
        sc = _dot_nt(qs.astype(BF16), kall.astype(BF16)) + bias
        mx = jnp.maximum(jnp.max(sc, axis=1, keepdims=True), sink)
        p = jnp.exp(sc - mx)
        den = jnp.sum(p, axis=1, keepdims=True) + jnp.exp(sink - mx)
        p = (p / den).astype(BF16)
        o1 = _dot(p, vall.astype(BF16))
        o2 = _dot(p, pltpu.roll(vall, HEAD_DIM, axis=1).astype(BF16))
        slabs = []
        for hp in range(N_Q_HEADS // 2):
            lo, ro = (o1, o2) if hp < Q_PER_KV // 2 else (o2, o1)
            lrow = 2 * hp * SAMPLE_T
            slabs.append(jnp.where(left, lo[lrow:lrow + SAMPLE_T], ro[lrow + SAMPLE_T:lrow + 2 * SAMPLE_T]))
        a_scr[pl.ds(r, SAMPLE_T), :] = jnp.concatenate(slabs, axis=1)
        return carry

    lax.fori_loop(0, nseq, seq_body, 0, unroll=SAMPLE_ATTN_UNROLL)
    m = (_dot(a_scr[...].astype(BF16), wout_ref[0:ATTN_WIDTH, :])
         + _dot(s.astype(BF16), wout_ref[ATTN_WIDTH:, :]))
    m_ref[...] = m.astype(m_ref.dtype)


def _sample_l0(x, g_pre, w_in_s, sinks, ck, cv, s_re, s_im, tabs, d_skip, w_glu, w_out, *, nseq, ntok):
    a_re, a_im, bb, cc = tabs
    rows_all = x.shape[0]
    rows = nseq * SAMPLE_T
    nstate = SSM_GROUPS * SSM_STATE
    row_spec = lambda w: pl.BlockSpec((rows, w), lambda i: (i, 0))
    return pl.pallas_call(
        functools.partial(_sample_l0_kernel, nseq=nseq, ntok=ntok),
        grid=(rows_all // rows,),
        in_specs=[
            pl.BlockSpec(memory_space=pltpu.SMEM),
            row_spec(D_MODEL), _const_spec((1, D_MODEL)), _const_spec(w_in_s.shape),
            pl.BlockSpec((nseq, WINDOW, KV_WIDTH), lambda i: (i, 0, 0)),
            pl.BlockSpec((nseq, WINDOW, KV_WIDTH), lambda i: (i, 0, 0)),
            pl.BlockSpec((nseq, nstate), lambda i: (i, 0)),
            pl.BlockSpec((nseq, nstate), lambda i: (i, 0)),
            _const_spec(bb.shape), _const_spec(cc.shape),
            _const_spec(a_re.shape), _const_spec(a_im.shape),
            _const_spec((1, SSM_WIDTH)),
            _const_spec(w_glu.shape), _const_spec(w_out.shape),
        ],
        out_specs=[
            row_spec(D_MODEL),
            pl.BlockSpec((nseq, WINDOW, KV_WIDTH), lambda i: (i, 0, 0)),
            pl.BlockSpec((nseq, WINDOW, KV_WIDTH), lambda i: (i, 0, 0)),
            pl.BlockSpec((nseq, nstate), lambda i: (i, 0)),
            pl.BlockSpec((nseq, nstate), lambda i: (i, 0)),
        ],
        out_shape=[
            jax.ShapeDtypeStruct((rows_all, D_MODEL), BF16),
            jax.ShapeDtypeStruct(ck.shape, F32),
            jax.ShapeDtypeStruct(cv.shape, F32),
            jax.ShapeDtypeStruct((rows_all // SAMPLE_T, nstate), F32),
            jax.ShapeDtypeStruct((rows_all // SAMPLE_T, nstate), F32),
        ],
        scratch_shapes=[
            pltpu.VMEM((rows, N_Q_HEADS * LANE), F32),
            pltpu.VMEM((rows, KV_WIDTH), F32),
            pltpu.VMEM((rows, KV_WIDTH), F32),
            pltpu.VMEM((rows, ATTN_WIDTH), F32),
            pltpu.VMEM((N_STRIP, rows, LANE), F32),
            pltpu.VMEM((N_STRIP, rows, LANE), F32),
        ],
        compiler_params=_params(1),
        name="sample_l0",
    )(sinks, x, g_pre, w_in_s, ck, cv, s_re, s_im, bb, cc, a_re, a_im, d_skip, w_glu, w_out)


def _sample_l1_kernel(x_ref, g_ref, w1_ref, b1_ref, st_ref, wst_ref, wnew_ref, bdw_ref, lng_ref, lnb_ref,
                      w2_ref, b2_ref, m_ref, sto_ref, c_scr, *, nseq, ntok):
    h = _rms(x_ref[...], g_ref[...]).astype(BF16)
    a = _dot(h, w1_ref[...]) + b1_ref[...]
    gg = a[:, :D_MODEL] * jax.nn.sigmoid(a[:, D_MODEL:])
    g3 = gg.reshape(nseq, SAMPLE_T, D_MODEL)
    st = st_ref[...]
    sto_ref[...] = jnp.concatenate([st[:, ntok:, :], g3[:, :ntok, :]], axis=1)
    c_scr[...] = jnp.zeros(c_scr.shape, F32)
    for t in range(ntok):
        ct = (jnp.sum(st * wst_ref[t][None], axis=1) + jnp.sum(g3 * wnew_ref[t][None], axis=1)
              + bdw_ref[...])
        for j in range(D_SLABS):
            c_scr[j, pl.ds(t, nseq, stride=SAMPLE_T), :] = ct[:, j * LANE:(j + 1) * LANE]
    c = jnp.concatenate([c_scr[j] for j in range(D_SLABS)], axis=1)
    act = _layer_norm_silu(c, lng_ref, lnb_ref).astype(BF16)
    m_ref[...] = (_dot(act, w2_ref[...]) + b2_ref[...]).astype(m_ref.dtype)


def _sample_l1(x, g_pre, cw, st, wst, wnew, *, nseq, ntok):
    w1, b1, _, bdw, lng, lnb, w2, b2 = cw
    rows_all = x.shape[0]
    rows = nseq * SAMPLE_T
    vec = lambda n: _const_spec((1, n))
    row_spec = pl.BlockSpec((rows, D_MODEL), lambda i: (i, 0))
    return pl.pallas_call(
        functools.partial(_sample_l1_kernel, nseq=nseq, ntok=ntok),
        grid=(rows_all // rows,),
        in_specs=[
            row_spec, vec(D_MODEL), _const_spec(w1.shape), vec(2 * D_MODEL),
            pl.BlockSpec((nseq, CONV_TAIL, D_MODEL), lambda i: (i, 0, 0)),
            _const_spec(wst.shape), _const_spec(wnew.shape),
            vec(D_MODEL), vec(D_MODEL), vec(D_MODEL), _const_spec(w2.shape), vec(D_MODEL),
        ],
        out_specs=[row_spec, pl.BlockSpec((nseq, CONV_TAIL, D_MODEL), lambda i: (i, 0, 0))],
        out_shape=[
            jax.ShapeDtypeStruct((rows_all, D_MODEL), BF16),
            jax.ShapeDtypeStruct(st.shape, F32),
        ],
        scratch_shapes=[pltpu.VMEM((D_SLABS, rows, LANE), F32)],
        compiler_params=_params(1),
        name="sample_l1",
    )(x, g_pre, w1, b1, st, wst, wnew, bdw, lng, lnb, w2, b2)


PROMPT_FFN_ROWS = 1024
PROMPT_STEPS = 64
SAMPLE_SEQS = 32
SAMPLE_FFN_ROWS = 1024
SAMPLE_ATTN_UNROLL = 8


def kernel(x_prompt, x_sample, cache_k, cache_v, state_s5_re, state_s5_im, state_conv, norm_pre_mix, norm_post_mix, norm_pre_ffn, norm_post_ffn, w_in_ab, attn_sinks, s5_lambda_re, s5_lambda_im, s5_log_step, s5_b_re, s5_b_im, s5_c_re, s5_c_im, s5_d, w_glu, w_out_ab, w_pw1, b_pw1, w_dw, b_dw, conv_ln_g, conv_ln_b, w_pw2, b_pw2, w_ffn_gate, w_ffn_up, w_ffn_down):
    nb, seq, _ = x_prompt.shape
    ns, ntok, _ = x_sample.shape
    assert nb == SUBLANE and ntok <= SAMPLE_T and seq % WINDOW == 0 and seq % PROMPT_STEPS == 0
    row = lambda v: v.reshape(1, -1)
    gains = lambda layer: tuple(row(n[layer]) for n in (norm_post_mix, norm_pre_ffn, norm_post_ffn))

    w_in = w_in_ab[0].astype(BF16)
    wq = w_in_ab[0][:, :ATTN_WIDTH].reshape(D_MODEL, N_Q_HEADS, HEAD_DIM)
    lane_kv = (jnp.arange(LANE) // HEAD_DIM)[None, :]
    head_kv = (jnp.arange(N_Q_HEADS) // Q_PER_KV)[:, None]
    wq_slab = jnp.where((lane_kv == head_kv)[None], jnp.concatenate([wq, wq], axis=2), 0.0)
    w_in_s = jnp.concatenate(
        [wq_slab.reshape(D_MODEL, N_Q_HEADS * LANE), w_in_ab[0][:, ATTN_WIDTH:]], axis=1).astype(BF16)
    w_glu_b = w_glu[0].astype(BF16)
    w_out = w_out_ab[0].astype(BF16)
    ffn_w = [(w_ffn_gate[l].astype(BF16), w_ffn_up[l].astype(BF16), w_ffn_down[l].astype(BF16))
             for l in range(2)]
    sinks = attn_sinks[0]
    tabs = _s5_tables(s5_lambda_re[0], s5_lambda_im[0], s5_log_step[0], s5_b_re[0], s5_b_im[0],
                      s5_c_re[0], s5_c_im[0])
    d_skip = row(s5_d)
    wdw = w_dw[0]
    conv_w = (w_pw1[0].astype(BF16), row(b_pw1), jnp.repeat(wdw, SUBLANE, axis=0), row(b_dw),
              row(conv_ln_g), row(conv_ln_b), w_pw2[0].astype(BF16), row(b_pw2))
    tt = jnp.arange(ntok)[:, None]
    tap_st = jnp.arange(CONV_TAIL)[None, :] - tt
    wst = jnp.where((tap_st >= 0)[..., None], wdw[jnp.clip(tap_st, 0, CONV_WIDTH - 1)], 0.0)
    tap_new = CONV_TAIL + jnp.arange(SAMPLE_T)[None, :] - tt
    wnew = jnp.where((tap_new <= CONV_TAIL)[..., None], wdw[jnp.clip(tap_new, 0, CONV_WIDTH - 1)], 0.0)

    flat = lambda v: v.reshape(nb * seq, D_MODEL)
    au_p, kk_p, vk_p = _prompt_proj_attn(x_prompt, row(norm_pre_mix[0]), w_in, sinks)
    m0_p, hfin_p = _prompt_s5(au_p, tabs, d_skip, w_glu_b, w_out, nb=nb, steps=PROMPT_STEPS)
    x1_p = _post_ffn(flat(x_prompt), flat(m0_p), gains(0), ffn_w[0], rows=PROMPT_FFN_ROWS, name="prompt_ffn0")
    m1_p, tail_p = _prompt_conv(x1_p.reshape(nb, seq, D_MODEL), row(norm_pre_mix[1]), conv_w, steps=PROMPT_STEPS)
    y_prompt = _post_ffn(x1_p, flat(m1_p), gains(1), ffn_w[1], rows=PROMPT_FFN_ROWS, name="prompt_ffn1")
    y_prompt = y_prompt.reshape(nb, seq, D_MODEL)

    xs = jnp.pad(x_sample, ((0, 0), (0, SAMPLE_T - ntok), (0, 0))).reshape(ns * SAMPLE_T, D_MODEL)
    nstate = SSM_GROUPS * SSM_STATE
    m0_s, ko_s, vo_s, sre_s, sim_s = _sample_l0(
        xs, row(norm_pre_mix[0]), w_in_s, sinks,
        cache_k.reshape(ns, WINDOW, KV_WIDTH), cache_v.reshape(ns, WINDOW, KV_WIDTH),
        state_s5_re.reshape(ns, nstate), state_s5_im.reshape(ns, nstate),
        tabs, d_skip, w_glu_b, w_out, nseq=SAMPLE_SEQS, ntok=ntok)
    x1_s = _post_ffn(xs, m0_s, gains(0), ffn_w[0], rows=SAMPLE_FFN_ROWS, name="sample_ffn0")
    m1_s, sto_s = _sample_l1(x1_s, row(norm_pre_mix[1]), conv_w, state_conv.reshape(ns, CONV_TAIL, D_MODEL), wst, wnew,
                           nseq=SAMPLE_SEQS, ntok=ntok)
    y_s = _post_ffn(x1_s, m1_s, gains(1), ffn_w[1], rows=SAMPLE_FFN_ROWS, name="sample_ffn1")

    y_sample = y_s.reshape(ns, SAMPLE_T, D_MODEL)[:, :ntok]
    k_prompt = kk_p.reshape(1, nb, WINDOW, N_KV_HEADS, HEAD_DIM)
    v_prompt = vk_p.reshape(1, nb, WINDOW, N_KV_HEADS, HEAD_DIM)
    hfin = jnp.swapaxes(hfin_p, 0, 1)
    s5_re_prompt = hfin[:, :, :STRIP_STATES].reshape(1, nb, SSM_GROUPS, SSM_STATE)
    s5_im_prompt = hfin[:, :, STRIP_STATES:].reshape(1, nb, SSM_GROUPS, SSM_STATE)
    conv_prompt = jnp.swapaxes(tail_p.reshape(CONV_TAIL, nb, D_MODEL), 0, 1)[None]
    k_sample = ko_s.reshape(1, ns, WINDOW, N_KV_HEADS, HEAD_DIM)
    v_sample = vo_s.reshape(1, ns, WINDOW, N_KV_HEADS, HEAD_DIM)
    s5_re_sample = sre_s.reshape(1, ns, SSM_GROUPS, SSM_STATE)
    s5_im_sample = sim_s.reshape(1, ns, SSM_GROUPS, SSM_STATE)
    conv_sample = sto_s.reshape(1, ns, CONV_TAIL, D_MODEL)
    return (y_prompt, y_sample, k_prompt, v_prompt, s5_re_prompt, s5_im_prompt, conv_prompt,
            k_sample, v_sample, s5_re_sample, s5_im_sample, conv_sample)
```

```python
import functools

import jax
import jax.numpy as jnp
from jax import lax
from jax.experimental import pallas as pl
from jax.experimental.pallas import tpu as pltpu

F32 = jnp.float32
BF16 = jnp.bfloat16

LANE = 128
SUBLANE = 8
VMEM_LIMIT_BYTES = 56 * 1024 * 1024

D_MODEL = 1024
HEAD_DIM = 64
N_Q_HEADS = 8
N_KV_HEADS = 2
Q_PER_KV = N_Q_HEADS // N_KV_HEADS
WINDOW = 128
ATTN_WIDTH = N_Q_HEADS * HEAD_DIM
KV_WIDTH = N_KV_HEADS * HEAD_DIM
SSM_WIDTH = 512
SSM_GROUP = 16
SSM_GROUPS = 32
SSM_STATE = 64
D_FF = 2816
CONV_WIDTH = 31
CONV_TAIL = CONV_WIDTH - 1
RMS_EPS = 1e-6
LN_EPS = 1e-5

N_STRIP = SSM_WIDTH // LANE
STRIP_STATES = (SSM_GROUPS // N_STRIP) * SSM_STATE
D_SLABS = D_MODEL // LANE
ATTN_SLABS = ATTN_WIDTH // LANE
FFN_SPLIT = 4

SAMPLE_T = SUBLANE


def _rms(x, g):
    return x * lax.rsqrt(jnp.mean(x * x, axis=-1, keepdims=True) + RMS_EPS) * g


def _dot(a, b):
    return jnp.dot(a, b, preferred_element_type=F32)


def _dot_nt(a, b):
    return lax.dot_general(a, b, (((1,), (1,)), ((), ())), preferred_element_type=F32)


def _const_spec(shape):
    nd = len(shape)
    return pl.BlockSpec(shape, lambda *_: (0,) * nd, pipeline_mode=pl.Buffered(1))


def _params(n_axes):
    return pltpu.CompilerParams(
        dimension_semantics=("arbitrary",) * n_axes, vmem_limit_bytes=VMEM_LIMIT_BYTES)


def _load_time_major(x_ref, slab_scr, nb):
    steps = x_ref.shape[1]
    nslab = x_ref.shape[2] // LANE
    for b in range(nb):
        for j in range(nslab):
            slab_scr[j, pl.ds(b, steps, stride=nb), :] = x_ref[b, :, j * LANE:(j + 1) * LANE]
    return jnp.concatenate([slab_scr[j] for j in range(nslab)], axis=1)


def _store_sequence_major(val, slab_scr, out_ref, nb):
    steps = out_ref.shape[1]
    nslab = out_ref.shape[2] // LANE
    for j in range(nslab):
        slab_scr[j] = val[:, j * LANE:(j + 1) * LANE]
    for b in range(nb):
        for j in range(nslab):
            out_ref[b, :, j * LANE:(j + 1) * LANE] = (
                slab_scr[j, pl.ds(b, steps, stride=nb), :].astype(out_ref.dtype))


def _s5_disc_kernel(lre_ref, lim_ref, ls_ref, bre_ref, bim_ref, are_ref, aim_ref, bbre_ref, bbim_ref):
    lam_re = lre_ref[...]
    lam_im = lim_ref[...]
    dt = jnp.exp(ls_ref[...])
    mag = jnp.exp(lam_re * dt)
    lb_re = mag * jnp.cos(lam_im * dt)
    lb_im = mag * jnp.sin(lam_im * dt)
    den = lam_re * lam_re + lam_im * lam_im
    coef_re = ((lb_re - 1.0) * lam_re + lb_im * lam_im) / den
    coef_im = (lb_im * lam_re - (lb_re - 1.0) * lam_im) / den
    b_re = bre_ref[...]
    b_im = bim_ref[...]
    are_ref[...] = lb_re
    aim_ref[...] = lb_im
    bbre_ref[...] = coef_re * b_re - coef_im * b_im
    bbim_ref[...] = coef_re * b_im + coef_im * b_re


def _s5_tables(lam_re, lam_im, log_step, b_re, b_im, c_re, c_im):
    rep = lambda v: jnp.repeat(v, SSM_GROUP, axis=0)
    rows = SSM_GROUPS * SSM_GROUP
    bt = lambda b: jnp.swapaxes(b, 1, 2).reshape(rows, SSM_STATE)
    ls = jnp.broadcast_to(log_step[:, None], (SSM_GROUPS, SSM_STATE))
    shp = jax.ShapeDtypeStruct((rows, SSM_STATE), F32)
    a_re, a_im, bb_re, bb_im = pl.pallas_call(
        _s5_disc_kernel, out_shape=(shp, shp, shp, shp), name="s5_disc",
    )(rep(lam_re), rep(lam_im), rep(ls), bt(b_re), bt(b_im))
    a_re = a_re[::SSM_GROUP].reshape(N_STRIP, STRIP_STATES)
    a_im = a_im[::SSM_GROUP].reshape(N_STRIP, STRIP_STATES)
    gl = SSM_GROUPS // N_STRIP
    eye = jnp.eye(gl, dtype=F32).reshape(1, gl, 1, gl, 1)

    def in_blockdiag(m):
        m = m.reshape(N_STRIP, gl, SSM_GROUP, 1, SSM_STATE)
        return (m * eye).reshape(N_STRIP, LANE, STRIP_STATES)

    def out_blockdiag(m):
        m = jnp.swapaxes(m.reshape(N_STRIP, gl, SSM_GROUP, SSM_STATE), 2, 3)
        return (m.reshape(N_STRIP, gl, SSM_STATE, 1, SSM_GROUP) * eye).reshape(N_STRIP, STRIP_STATES, LANE)

    bb = jnp.concatenate([in_blockdiag(bb_re), in_blockdiag(bb_im)], axis=2).astype(BF16)
    cc = jnp.concatenate([out_blockdiag(c_re), -out_blockdiag(c_im)], axis=1).astype(BF16)
    return a_re, a_im, bb, cc


def _gelu_glu(y, wglu_ref):
    g = jax.nn.gelu(y)
    return g * jax.nn.sigmoid(_dot(g.astype(BF16), wglu_ref[...]))


def _band_attention(q, kv_scr, sinks, from_prev, prev_bias, seqs):
    gw = Q_PER_KV * HEAD_DIM
    left = lax.broadcasted_iota(jnp.int32, (2 * WINDOW, KV_WIDTH), 1) < HEAD_DIM
    head_of_lane = lax.shift_right_logical(
        lax.broadcasted_iota(jnp.int32, (WINDOW, gw), 1), HEAD_DIM.bit_length() - 1)
    scores, values = [], []
    for n, b in enumerate(seqs):
        kb = kv_scr[b, :, 0:KV_WIDTH]
        vb = kv_scr[b, :, KV_WIDTH:2 * KV_WIDTH]
        kr = pltpu.roll(kb, HEAD_DIM, axis=1)
        vr = pltpu.roll(vb, HEAD_DIM, axis=1)
        for g in range(N_KV_HEADS):
            kk = jnp.where(left, kb, kr) if g == 0 else jnp.where(left, kr, kb)
            vv = jnp.where(left, vb, vr) if g == 0 else jnp.where(left, vr, vb)
            k4 = jnp.concatenate([kk, kk], axis=1).astype(BF16)
            values.append(jnp.concatenate([vv, vv], axis=1).astype(BF16))
            qg = q[n * WINDOW:(n + 1) * WINDOW, g * gw:(g + 1) * gw]
            qst = jnp.concatenate(
                [jnp.where(head_of_lane == r, qg, 0.0) for r in range(Q_PER_KV)], axis=0).astype(BF16)
            scores.append(_dot_nt(qst, k4))
    nblk = len(scores)
    s = jnp.concatenate(scores, axis=0)
    prev = jnp.concatenate([from_prev] * nblk, axis=0)
    s = jnp.where(prev, s[:, :WINDOW] + prev_bias, s[:, WINDOW:])
    sink = jnp.concatenate(
        [jnp.full((WINDOW, WINDOW), sinks[(n % N_KV_HEADS) * Q_PER_KV + r], F32)
         for n in range(nblk) for r in range(Q_PER_KV)], axis=0)
    m = jnp.maximum(jnp.max(s, axis=1, keepdims=True), sink)
    p = jnp.exp(s - m)
    pb = p.astype(BF16)
    hrows = pb.shape[0] // 2
    ri = lax.broadcasted_iota(jnp.int32, (2 * WINDOW, 2 * WINDOW), 0) < WINDOW
    ci = lax.broadcasted_iota(jnp.int32, (2 * WINDOW, 2 * WINDOW), 1) < WINDOW
    ones2 = jnp.where(ri == ci, 1.0, 0.0).astype(BF16)
    sums = _dot(jnp.concatenate([pb[:hrows], pb[hrows:]], axis=1), ones2)
    den = jnp.concatenate([sums[:, :WINDOW], sums[:, WINDOW:]], axis=0) + jnp.exp(sink - m)
    p = p / den
    p = jnp.concatenate([jnp.where(prev, p, 0.0).astype(BF16), jnp.where(prev, 0.0, p).astype(BF16)], axis=1)
    rows = Q_PER_KV * WINDOW
    outs = []
    for b in range(len(seqs)):
        og = []
        for g in range(N_KV_HEADS):
            n = b * N_KV_HEADS + g
            o4 = _dot(p[n * rows:(n + 1) * rows], values[n])
            o = jnp.where(head_of_lane == 0, o4[0:WINDOW], 0.0)
            for r in range(1, Q_PER_KV):
                o = jnp.where(head_of_lane == r, o4[r * WINDOW:(r + 1) * WINDOW], o)
            og.append(o)
        outs.append(jnp.concatenate(og, axis=1))
    return outs


def _proj_attn_kernel(sink_ref, x_ref, g_ref, w_ref, au_ref, kk_ref, vk_ref, kv_scr, *, nb):
    i = pl.program_id(0)
    z = jnp.concatenate(
        [_dot(_rms(x_ref[b:b + 2].reshape(2 * WINDOW, D_MODEL), g_ref[...]).astype(BF16), w_ref[...])
         for b in range(0, nb, 2)], axis=0)
    ucol = ATTN_WIDTH + 2 * KV_WIDTH

    @pl.when(i == 0)
    def _():
        kv_scr[:, 0:WINDOW, :] = jnp.zeros((nb, WINDOW, 2 * KV_WIDTH), F32)

    for b in range(nb):
        kv_scr[b, WINDOW:2 * WINDOW, :] = z[b * WINDOW:(b + 1) * WINDOW, ATTN_WIDTH:ucol]
    q = z[:, :ATTN_WIDTH] * (HEAD_DIM ** -0.5)

    shape = (Q_PER_KV * WINDOW, WINDOW)
    row = jnp.bitwise_and(lax.broadcasted_iota(jnp.int32, shape, 0), WINDOW - 1)
    col = lax.broadcasted_iota(jnp.int32, shape, 1)
    prev_bias = jnp.where(i == 0, -jnp.inf, 0.0).astype(F32)
    sinks = [sink_ref[hh] for hh in range(N_Q_HEADS)]
    outs = _band_attention(q, kv_scr, sinks, col > row, prev_bias, range(nb))
    for b in range(nb):
        zb = z[b * WINDOW:(b + 1) * WINDOW]
        for j in range(ATTN_SLABS):
            au_ref[j, pl.ds(b, WINDOW, stride=nb), :] = outs[b][:, j * LANE:(j + 1) * LANE]
        for j in range(N_STRIP):
            au_ref[ATTN_SLABS + j, pl.ds(b, WINDOW, stride=nb), :] = zb[:, ucol + j * LANE:ucol + (j + 1) * LANE]
        kk_ref[b] = zb[:, ATTN_WIDTH:ATTN_WIDTH + KV_WIDTH]
        vk_ref[b] = zb[:, ATTN_WIDTH + KV_WIDTH:ucol]
    kv_scr[:, 0:WINDOW, :] = kv_scr[:, WINDOW:2 * WINDOW, :]


def _prompt_proj_attn(x, g_pre, w_in, sinks):
    nb, seq, _ = x.shape
    nslab = ATTN_SLABS + N_STRIP
    return pl.pallas_call(
        functools.partial(_proj_attn_kernel, nb=nb),
        grid=(seq // WINDOW,),
        in_specs=[
            pl.BlockSpec(memory_space=pltpu.SMEM),
            pl.BlockSpec((nb, WINDOW, D_MODEL), lambda i: (0, i, 0)),
            _const_spec((1, D_MODEL)),
            _const_spec(w_in.shape),
        ],
        out_specs=[
            pl.BlockSpec((nslab, WINDOW * nb, LANE), lambda i: (0, i, 0)),
            pl.BlockSpec((nb, WINDOW, KV_WIDTH), lambda i: (0, 0, 0)),
            pl.BlockSpec((nb, WINDOW, KV_WIDTH), lambda i: (0, 0, 0)),
        ],
        out_shape=[
            jax.ShapeDtypeStruct((nslab, seq * nb, LANE), F32),
            jax.ShapeDtypeStruct((nb, WINDOW, KV_WIDTH), F32),
            jax.ShapeDtypeStruct((nb, WINDOW, KV_WIDTH), F32),
        ],
        scratch_shapes=[pltpu.VMEM((nb, 2 * WINDOW, 2 * KV_WIDTH), F32)],
        compiler_params=_params(1),
        name="prompt_proj_attn",
    )(sinks, x, g_pre, w_in)


def _s5_prompt_kernel(au_ref, bb_ref, cc_ref, are_ref, aim_ref, d_ref, wglu_ref, wout_ref,
                      m_ref, hfin_ref, hbuf, slab_scr, *, steps, nb):
    i = pl.program_id(0)

    @pl.when(i == 0)
    def _():
        hfin_ref[...] = jnp.zeros(hfin_ref.shape, F32)

    ys = []
    for j in range(N_STRIP):
        uj = au_ref[ATTN_SLABS + j]
        hbuf[...] = _dot(uj.astype(BF16), bb_ref[j])
        are = jnp.broadcast_to(are_ref[j:j + 1, :], (nb, STRIP_STATES))
        aim = jnp.broadcast_to(aim_ref[j:j + 1, :], (nb, STRIP_STATES))

        def step(t, carry):
            hr, hi = carry
            r = pl.multiple_of(t * nb, nb)
            br = hbuf[pl.ds(r, nb), 0:STRIP_STATES]
            bi = hbuf[pl.ds(r, nb), STRIP_STATES:2 * STRIP_STATES]
            nr = are * hr - aim * hi + br
            ni = are * hi + aim * hr + bi
            hbuf[pl.ds(r, nb), 0:STRIP_STATES] = nr
            hbuf[pl.ds(r, nb), STRIP_STATES:2 * STRIP_STATES] = ni
            return nr, ni

        h0 = (hfin_ref[j, :, 0:STRIP_STATES], hfin_ref[j, :, STRIP_STATES:2 * STRIP_STATES])
        hr, hi = lax.fori_loop(0, steps, step, h0, unroll=True)
        hfin_ref[j, :, 0:STRIP_STATES] = hr
        hfin_ref[j, :, STRIP_STATES:2 * STRIP_STATES] = hi
        ys.append(_dot(hbuf[...].astype(BF16), cc_ref[j]) + d_ref[:, j * LANE:(j + 1) * LANE] * uj)
    s = _gelu_glu(jnp.concatenate(ys, axis=1), wglu_ref)
    a = jnp.concatenate([au_ref[j] for j in range(ATTN_SLABS)], axis=1).astype(BF16)
    m = _dot(a, wout_ref[0:ATTN_WIDTH, :]) + _dot(s.astype(BF16), wout_ref[ATTN_WIDTH:, :])
    _store_sequence_major(m, slab_scr, m_ref, nb)


def _prompt_s5(au, tabs, d_skip, w_glu, w_out, *, nb, steps):
    a_re, a_im, bb, cc = tabs
    nslab, rows, _ = au.shape
    blk = steps * nb
    return pl.pallas_call(
        functools.partial(_s5_prompt_kernel, steps=steps, nb=nb),
        grid=(rows // blk,),
        in_specs=[
            pl.BlockSpec((nslab, blk, LANE), lambda i: (0, i, 0)),
            _const_spec(bb.shape), _const_spec(cc.shape),
            _const_spec(a_re.shape), _const_spec(a_im.shape),
            _const_spec((1, SSM_WIDTH)),
            _const_spec(w_glu.shape), _const_spec(w_out.shape),
        ],
        out_specs=[
            pl.BlockSpec((nb, steps, D_MODEL), lambda i: (0, i, 0)),
            pl.BlockSpec((N_STRIP, nb, 2 * STRIP_STATES), lambda i: (0, 0, 0)),
        ],
        out_shape=[
            jax.ShapeDtypeStruct((nb, rows // nb, D_MODEL), BF16),
            jax.ShapeDtypeStruct((N_STRIP, nb, 2 * STRIP_STATES), F32),
        ],
        scratch_shapes=[pltpu.VMEM((blk, 2 * STRIP_STATES), F32),
                        pltpu.VMEM((D_SLABS, blk, LANE), F32)],
        compiler_params=_params(1),
        name="prompt_s5",
    )(au, bb, cc, a_re, a_im, d_skip, w_glu, w_out)


def _post_ffn_kernel(x_ref, m_ref, gpm_ref, gpf_ref, gof_ref, wg_ref, wu_ref, wd_ref, o_ref):
    group = x_ref.shape[0] // FFN_SPLIT
    for r in range(FFN_SPLIT):
        rows = slice(r * group, (r + 1) * group)
        x1 = x_ref[rows, :] + _rms(m_ref[rows, :].astype(F32), gpm_ref[...])
        h = _rms(x1, gpf_ref[...]).astype(BF16)
        gate = _dot(h, wg_ref[...])
        up = _dot(h, wu_ref[...])
        act = (gate * jax.nn.sigmoid(gate) * up).astype(BF16)
        f = _dot(act, wd_ref[...])
        o_ref[rows, :] = x1 + _rms(f, gof_ref[...])


def _post_ffn(x, m, gains, ffn_w, layer, *, rows, name):
    gpm, gpf, gof = gains
    wg, wu, wd = ffn_w
    row_spec = pl.BlockSpec((rows, D_MODEL), lambda i: (i, 0))
    layer_spec = lambda w: pl.BlockSpec(
        (None,) + w.shape[1:], lambda i: (layer, 0, 0), pipeline_mode=pl.Buffered(1))
    return pl.pallas_call(
        _post_ffn_kernel,
        grid=(x.shape[0] // rows,),
        in_specs=[row_spec, row_spec,
                  _const_spec((1, D_MODEL)), _const_spec((1, D_MODEL)), _const_spec((1, D_MODEL)),
                  layer_spec(wg), layer_spec(wu), layer_spec(wd)],
        out_specs=row_spec,
        out_shape=jax.ShapeDtypeStruct(x.shape, F32),
        compiler_params=_params(1),
        name=name,
    )(x, m, gpm, gpf, gof, wg, wu, wd)


CONV_HALO = 32
CONV_ROWS = 64


def _layer_norm_silu(c, lng_ref, lnb_ref):
    mu = jnp.mean(c, axis=-1, keepdims=True)
    cc = c - mu
    var = jnp.mean(cc * cc, axis=-1, keepdims=True)
    cn = cc * lax.rsqrt(var + LN_EPS) * lng_ref[...] + lnb_ref[...]
    return cn * jax.nn.sigmoid(cn)


def _conv_prompt_kernel(x_ref, g_ref, w1_ref, b1_ref, wdw_ref, bdw_ref, lng_ref, lnb_ref, w2_ref, b2_ref,
                        m_ref, tail_ref, gp_scr, c_scr, slab_scr, *, blk, nb):
    i = pl.program_id(0)
    halo = CONV_HALO * nb
    tail = CONV_TAIL * nb

    @pl.when(i == 0)
    def _():
        gp_scr[0:halo, :] = jnp.zeros((halo, D_MODEL), F32)

    x = _load_time_major(x_ref, slab_scr, nb)
    h = _rms(x, g_ref[...]).astype(BF16)
    a = _dot(h, w1_ref[...]) + b1_ref[...]
    gp_scr[halo:halo + blk, :] = a[:, :D_MODEL] * jax.nn.sigmoid(a[:, D_MODEL:])

    def chunk(ci, carry):
        base = pl.multiple_of(ci * CONV_ROWS, CONV_ROWS)
        for l in range(D_SLABS):
            lanes = slice(l * LANE, (l + 1) * LANE)
            xs = gp_scr[pl.ds(base + halo - tail, CONV_ROWS + tail), lanes]
            acc = jnp.broadcast_to(bdw_ref[:, lanes], (CONV_ROWS, LANE))
            for j in range(CONV_WIDTH):
                wj = wdw_ref[j * SUBLANE:(j + 1) * SUBLANE, lanes]
                wj = jnp.concatenate([wj] * (CONV_ROWS // SUBLANE), axis=0)
                acc = acc + wj * xs[j * nb:j * nb + CONV_ROWS]
            c_scr[pl.ds(base, CONV_ROWS), lanes] = acc
        return carry

    lax.fori_loop(0, blk // CONV_ROWS, chunk, 0)
    act = _layer_norm_silu(c_scr[...], lng_ref, lnb_ref).astype(BF16)
    _store_sequence_major(_dot(act, w2_ref[...]) + b2_ref[...], slab_scr, m_ref, nb)
    tail_ref[...] = gp_scr[halo + blk - tail:halo + blk, :]
    gp_scr[0:halo, :] = gp_scr[blk:blk + halo, :]


def _prompt_conv(x, g_pre, cw, *, steps):
    w1, b1, wdw8, bdw, lng, lnb, w2, b2 = cw
    nb, seq, _ = x.shape
    blk = steps * nb
    assert steps >= CONV_HALO and blk % CONV_ROWS == 0
    vec = lambda n: _const_spec((1, n))
    seq_spec = pl.BlockSpec((nb, steps, D_MODEL), lambda i: (0, i, 0))
    return pl.pallas_call(
        functools.partial(_conv_prompt_kernel, blk=blk, nb=nb),
        grid=(seq // steps,),
        in_specs=[
            seq_spec,
            vec(D_MODEL), _const_spec(w1.shape), vec(2 * D_MODEL),
            _const_spec(wdw8.shape), vec(D_MODEL), vec(D_MODEL), vec(D_MODEL),
            _const_spec(w2.shape), vec(D_MODEL),
        ],
        out_specs=[
            seq_spec,
            pl.BlockSpec((CONV_TAIL * nb, D_MODEL), lambda i: (0, 0)),
        ],
        out_shape=[
            jax.ShapeDtypeStruct((nb, seq, D_MODEL), BF16),
            jax.ShapeDtypeStruct((CONV_TAIL * nb, D_MODEL), F32),
        ],
        scratch_shapes=[pltpu.VMEM((blk + CONV_HALO * nb, D_MODEL), F32),
                        pltpu.VMEM((blk, D_MODEL), F32),
                        pltpu.VMEM((D_SLABS, blk, LANE), F32)],
        compiler_params=_params(1),
        name="prompt_conv",
    )(x, g_pre, w1, b1, wdw8, bdw, lng, lnb, w2, b2)


def _sample_l0_kernel(sink_ref, x_ref, g_ref, w_ref, ck_ref, cv_ref, sre_ref, sim_ref,
                      bb_ref, cc_ref, are_ref, aim_ref, d_ref, wglu_ref, wout_ref,
                      m_ref, ko_ref, vo_ref, ore_ref, oim_ref,
                      q_scr, k_scr, v_scr, a_scr, u_scr, y_scr, *, nseq, ntok):
    qw = N_Q_HEADS * LANE
    h = _rms(x_ref[...], g_ref[...]).astype(BF16)
    z = _dot(h, w_ref[...])
    q_scr[...] = z[:, :qw] * (HEAD_DIM ** -0.5)
    kn = z[:, qw:qw + KV_WIDTH]
    vn = z[:, qw + KV_WIDTH:qw + 2 * KV_WIDTH]
    k_scr[...] = kn
    v_scr[...] = vn
    for j in range(N_STRIP):
        u_scr[j] = z[:, qw + 2 * KV_WIDTH + j * LANE:qw + 2 * KV_WIDTH + (j + 1) * LANE]
    y_scr[...] = jnp.zeros(y_scr.shape, F32)

    for j in range(N_STRIP):
        hr = sre_ref[:, j * STRIP_STATES:(j + 1) * STRIP_STATES]
        hi = sim_ref[:, j * STRIP_STATES:(j + 1) * STRIP_STATES]
        are = are_ref[j:j + 1, :]
        aim = aim_ref[j:j + 1, :]
        dj = d_ref[:, j * LANE:(j + 1) * LANE]
        for t in range(ntok):
            ut = u_scr[j, pl.ds(t, nseq, stride=SAMPLE_T), :]
            bu = _dot(ut.astype(BF16), bb_ref[j])
            hr, hi = (are * hr - aim * hi + bu[:, :STRIP_STATES],
                      are * hi + aim * hr + bu[:, STRIP_STATES:])
            hcat = jnp.concatenate([hr, hi], axis=1).astype(BF16)
            y_scr[j, pl.ds(t, nseq, stride=SAMPLE_T), :] = _dot(hcat, cc_ref[j]) + dj * ut
        ore_ref[:, j * STRIP_STATES:(j + 1) * STRIP_STATES] = hr
        oim_ref[:, j * STRIP_STATES:(j + 1) * STRIP_STATES] = hi
    s = _gelu_glu(jnp.concatenate([y_scr[j] for j in range(N_STRIP)], axis=1), wglu_ref)

    nkeys = WINDOW + SAMPLE_T
    shape = (N_Q_HEADS * SAMPLE_T, nkeys)
    tq = jnp.bitwise_and(lax.broadcasted_iota(jnp.int32, shape, 0), SAMPLE_T - 1)
    col = lax.broadcasted_iota(jnp.int32, shape, 1)
    ninf = jnp.float32(-jnp.inf)
    bias = jnp.where(col < WINDOW, jnp.where(col > tq, 0.0, ninf),
                     jnp.where(col - WINDOW <= tq, 0.0, ninf))
    sink = jnp.concatenate(
        [jnp.full((SAMPLE_T, 1), sink_ref[hh], F32) for hh in range(N_Q_HEADS)], axis=0)
    left = lax.broadcasted_iota(jnp.int32, (SAMPLE_T, LANE), 1) < HEAD_DIM

    def seq_body(n, carry):
        r = pl.multiple_of(n * SAMPLE_T, SAMPLE_T)
        qs = jnp.concatenate(
            [q_scr[pl.ds(r, SAMPLE_T), hh * LANE:(hh + 1) * LANE] for hh in range(N_Q_HEADS)], axis=0)
        kall = jnp.concatenate([ck_ref[n], k_scr[pl.ds(r, SAMPLE_T), :]], axis=0)
        vall = jnp.concatenate([cv_ref[n], v_scr[pl.ds(r, SAMPLE_T), :]], axis=0)
        ko_ref[n] = kall[ntok:ntok + WINDOW]
        vo_ref[n] = vall[ntok:ntok + WINDOW]
        sc = _dot_nt(qs.astype(BF16), kall.astype(BF16)) + bias
        mx = jnp.maximum(jnp.max(sc, axis=1, keepdims=True), sink)
        p = jnp.exp(sc - mx)
        den = jnp.sum(p, axis=1, keepdims=True) + jnp.exp(sink - mx)
        p = (p / den).astype(BF16)
        o1 = _dot(p, vall.astype(BF16))
        o2 = _dot(p, pltpu.roll(vall, HEAD_DIM, axis=1).astype(BF16))
        slabs = []
        for hp in range(N_Q_HEADS // 2):
            lo, ro = (o1, o2) if hp < Q_PER_KV // 2 else (o2, o1)
            lrow = 2 * hp * SAMPLE_T
            slabs.append(jnp.where(left, lo[lrow:lrow + SAMPLE_T], ro[lrow + SAMPLE_T:lrow + 2 * SAMPLE_T]))
        a_scr[pl.ds(r, SAMPLE_T), :] = jnp.concatenate(slabs, axis=1)
        return carry

    lax.fori_loop(0, nseq, seq_body, 0, unroll=SAMPLE_ATTN_UNROLL)
    m = (_dot(a_scr[...].astype(BF16), wout_ref[0:ATTN_WIDTH, :])
         + _dot(s.astype(BF16), wout_ref[ATTN_WIDTH:, :]))
    m_ref[...] = m.astype(m_ref.dtype)


def _sample_l0(x, g_pre, w_in_s, sinks, ck, cv, s_re, s_im, tabs, d_skip, w_glu, w_out, *, nseq, ntok):
    a_re, a_im, bb, cc = tabs
    rows_all = x.shape[0]
    rows = nseq * SAMPLE_T
    nstate = SSM_GROUPS * SSM_STATE
    row_spec = lambda w: pl.BlockSpec((rows, w), lambda i: (i, 0))
    return pl.pallas_call(
        functools.partial(_sample_l0_kernel, nseq=nseq, ntok=ntok),
        grid=(rows_all // rows,),
        in_specs=[
            pl.BlockSpec(memory_space=pltpu.SMEM),
            row_spec(D_MODEL), _const_spec((1, D_MODEL)), _const_spec(w_in_s.shape),
            pl.BlockSpec((nseq, WINDOW, KV_WIDTH), lambda i: (i, 0, 0)),
            pl.BlockSpec((nseq, WINDOW, KV_WIDTH), lambda i: (i, 0, 0)),
            pl.BlockSpec((nseq, nstate), lambda i: (i, 0)),
            pl.BlockSpec((nseq, nstate), lambda i: (i, 0)),
            _const_spec(bb.shape), _const_spec(cc.shape),
            _const_spec(a_re.shape), _const_spec(a_im.shape),
            _const_spec((1, SSM_WIDTH)),
            _const_spec(w_glu.shape), _const_spec(w_out.shape),
        ],
        out_specs=[
            row_spec(D_MODEL),
            pl.BlockSpec((nseq, WINDOW, KV_WIDTH), lambda i: (i, 0, 0)),
            pl.BlockSpec((nseq, WINDOW, KV_WIDTH), lambda i: (i, 0, 0)),
            pl.BlockSpec((nseq, nstate), lambda i: (i, 0)),
            pl.BlockSpec((nseq, nstate), lambda i: (i, 0)),
        ],
        out_shape=[
            jax.ShapeDtypeStruct((rows_all, D_MODEL), BF16),
            jax.ShapeDtypeStruct(ck.shape, F32),
            jax.ShapeDtypeStruct(cv.shape, F32),
            jax.ShapeDtypeStruct((rows_all // SAMPLE_T, nstate), F32),
            jax.ShapeDtypeStruct((rows_all // SAMPLE_T, nstate), F32),
        ],
        scratch_shapes=[
            pltpu.VMEM((rows, N_Q_HEADS * LANE), F32),
            pltpu.VMEM((rows, KV_WIDTH), F32),
            pltpu.VMEM((rows, KV_WIDTH), F32),
            pltpu.VMEM((rows, ATTN_WIDTH), F32),
            pltpu.VMEM((N_STRIP, rows, LANE), F32),
            pltpu.VMEM((N_STRIP, rows, LANE), F32),
        ],
        compiler_params=_params(1),
        name="sample_l0",
    )(sinks, x, g_pre, w_in_s, ck, cv, s_re, s_im, bb, cc, a_re, a_im, d_skip, w_glu, w_out)


def _sample_l1_kernel(x_ref, g_ref, w1_ref, b1_ref, st_ref, wdw_ref, bdw_ref, lng_ref, lnb_ref,
                      w2_ref, b2_ref, m_ref, sto_ref, g_scr, c_scr, *, nseq, ntok):
    h = _rms(x_ref[...], g_ref[...]).astype(BF16)
    a = _dot(h, w1_ref[...]) + b1_ref[...]
    gg = a[:, :D_MODEL] * jax.nn.sigmoid(a[:, D_MODEL:])
    for j in range(D_SLABS):
        g_scr[j] = gg[:, j * LANE:(j + 1) * LANE]
    c_scr[...] = jnp.zeros(c_scr.shape, F32)
    keep = CONV_TAIL - ntok
    for j in range(D_SLABS):
        lanes = slice(j * LANE, (j + 1) * LANE)
        new = [g_scr[j, pl.ds(t, nseq, stride=SAMPLE_T), :] for t in range(ntok)]
        acc = [jnp.broadcast_to(bdw_ref[:, lanes], (nseq, LANE)) for _ in range(ntok)]
        for r in range(CONV_TAIL):
            row = st_ref[r, :, lanes]
            if r >= ntok:
                sto_ref[r - ntok, :, lanes] = row
            for t in range(min(r, ntok - 1) + 1):
                acc[t] = acc[t] + row * wdw_ref[r - t:r - t + 1, lanes]
        for t in range(ntok):
            sto_ref[keep + t, :, lanes] = new[t]
            for t2 in range(t + 1):
                tap = CONV_TAIL + t2 - t
                acc[t] = acc[t] + new[t2] * wdw_ref[tap:tap + 1, lanes]
            c_scr[j, pl.ds(t, nseq, stride=SAMPLE_T), :] = acc[t]
    c = jnp.concatenate([c_scr[j] for j in range(D_SLABS)], axis=1)
    act = _layer_norm_silu(c, lng_ref, lnb_ref).astype(BF16)
    m_ref[...] = (_dot(act, w2_ref[...]) + b2_ref[...]).astype(m_ref.dtype)


def _sample_l1(x, g_pre, cw, st, wdw, *, nseq, ntok):
    w1, b1, _, bdw, lng, lnb, w2, b2 = cw
    rows_all = x.shape[0]
    rows = nseq * SAMPLE_T
    vec = lambda n: _const_spec((1, n))
    row_spec = pl.BlockSpec((rows, D_MODEL), lambda i: (i, 0))
    st_spec = pl.BlockSpec((CONV_TAIL, nseq, D_MODEL), lambda i: (0, i, 0))
    return pl.pallas_call(
        functools.partial(_sample_l1_kernel, nseq=nseq, ntok=ntok),
        grid=(rows_all // rows,),
        in_specs=[
            row_spec, vec(D_MODEL), _const_spec(w1.shape), vec(2 * D_MODEL),
            st_spec, _const_spec(wdw.shape),
            vec(D_MODEL), vec(D_MODEL), vec(D_MODEL), _const_spec(w2.shape), vec(D_MODEL),
        ],
        out_specs=[row_spec, st_spec],
        out_shape=[
            jax.ShapeDtypeStruct((rows_all, D_MODEL), BF16),
            jax.ShapeDtypeStruct(st.shape, F32),
        ],
        scratch_shapes=[pltpu.VMEM((D_SLABS, rows, LANE), F32),
                        pltpu.VMEM((D_SLABS, rows, LANE), F32)],
        compiler_params=_params(1),
        name="sample_l1",
    )(x, g_pre, w1, b1, st, wdw, bdw, lng, lnb, w2, b2)


PROMPT_FFN_ROWS = 1024
PROMPT_STEPS = 64
SAMPLE_SEQS = 32
SAMPLE_FFN_ROWS = 1024
SAMPLE_ATTN_UNROLL = 8


def kernel(x_prompt, x_sample, cache_k, cache_v, state_s5_re, state_s5_im, state_conv, norm_pre_mix, norm_post_mix, norm_pre_ffn, norm_post_ffn, w_in_ab, attn_sinks, s5_lambda_re, s5_lambda_im, s5_log_step, s5_b_re, s5_b_im, s5_c_re, s5_c_im, s5_d, w_glu, w_out_ab, w_pw1, b_pw1, w_dw, b_dw, conv_ln_g, conv_ln_b, w_pw2, b_pw2, w_ffn_gate, w_ffn_up, w_ffn_down):
    nb, seq, _ = x_prompt.shape
    ns, ntok, _ = x_sample.shape
    assert nb == SUBLANE and ntok <= SAMPLE_T and seq % WINDOW == 0 and seq % PROMPT_STEPS == 0
    row = lambda v: v.reshape(1, -1)
    gains = lambda layer: tuple(row(n[layer]) for n in (norm_post_mix, norm_pre_ffn, norm_post_ffn))

    w_in = w_in_ab[0].astype(BF16)
    wq = w_in_ab[0][:, :ATTN_WIDTH].reshape(D_MODEL, N_Q_HEADS, HEAD_DIM)
    lane_kv = (jnp.arange(LANE) // HEAD_DIM)[None, :]
    head_kv = (jnp.arange(N_Q_HEADS) // Q_PER_KV)[:, None]
    wq_slab = jnp.where((lane_kv == head_kv)[None], jnp.concatenate([wq, wq], axis=2), 0.0)
    w_in_s = jnp.concatenate(
        [wq_slab.reshape(D_MODEL, N_Q_HEADS * LANE), w_in_ab[0][:, ATTN_WIDTH:]], axis=1).astype(BF16)
    w_glu_b = w_glu[0].astype(BF16)
    w_out = w_out_ab[0].astype(BF16)
    ffn_w = (w_ffn_gate.astype(BF16), w_ffn_up.astype(BF16), w_ffn_down.astype(BF16))
    sinks = attn_sinks[0]
    tabs = _s5_tables(s5_lambda_re[0], s5_lambda_im[0], s5_log_step[0], s5_b_re[0], s5_b_im[0],
                      s5_c_re[0], s5_c_im[0])
    d_skip = row(s5_d)
    wdw = w_dw[0]
    conv_w = (w_pw1[0].astype(BF16), row(b_pw1), jnp.repeat(wdw, SUBLANE, axis=0), row(b_dw),
              row(conv_ln_g), row(conv_ln_b), w_pw2[0].astype(BF16), row(b_pw2))

    flat = lambda v: v.reshape(nb * seq, D_MODEL)
    au_p, kk_p, vk_p = _prompt_proj_attn(x_prompt, row(norm_pre_mix[0]), w_in, sinks)
    m0_p, hfin_p = _prompt_s5(au_p, tabs, d_skip, w_glu_b, w_out, nb=nb, steps=PROMPT_STEPS)
    x1_p = _post_ffn(flat(x_prompt), flat(m0_p), gains(0), ffn_w, 0, rows=PROMPT_FFN_ROWS, name="prompt_ffn0")
    m1_p, tail_p = _prompt_conv(x1_p.reshape(nb, seq, D_MODEL), row(norm_pre_mix[1]), conv_w, steps=PROMPT_STEPS)
    y_prompt = _post_ffn(x1_p, flat(m1_p), gains(1), ffn_w, 1, rows=PROMPT_FFN_ROWS, name="prompt_ffn1")
    y_prompt = y_prompt.reshape(nb, seq, D_MODEL)

    xs = jnp.pad(x_sample, ((0, 0), (0, SAMPLE_T - ntok), (0, 0))).reshape(ns * SAMPLE_T, D_MODEL)
    nstate = SSM_GROUPS * SSM_STATE
    m0_s, ko_s, vo_s, sre_s, sim_s = _sample_l0(
        xs, row(norm_pre_mix[0]), w_in_s, sinks,
        cache_k.reshape(ns, WINDOW, KV_WIDTH), cache_v.reshape(ns, WINDOW, KV_WIDTH),
        state_s5_re.reshape(ns, nstate), state_s5_im.reshape(ns, nstate),
        tabs, d_skip, w_glu_b, w_out, nseq=SAMPLE_SEQS, ntok=ntok)
    x1_s = _post_ffn(xs, m0_s, gains(0), ffn_w, 0, rows=SAMPLE_FFN_ROWS, name="sample_ffn0")
    st_t = jnp.swapaxes(state_conv.reshape(ns, CONV_TAIL, D_MODEL), 0, 1)
    m1_s, sto_t = _sample_l1(x1_s, row(norm_pre_mix[1]), conv_w, st_t, wdw, nseq=SAMPLE_SEQS, ntok=ntok)
    y_s = _post_ffn(x1_s, m1_s, gains(1), ffn_w, 1, rows=SAMPLE_FFN_ROWS, name="sample_ffn1")

    y_sample = y_s.reshape(ns, SAMPLE_T, D_MODEL)[:, :ntok]
    k_prompt = kk_p.reshape(1, nb, WINDOW, N_KV_HEADS, HEAD_DIM)
    v_prompt = vk_p.reshape(1, nb, WINDOW, N_KV_HEADS, HEAD_DIM)
    hfin = jnp.swapaxes(hfin_p, 0, 1)
    s5_re_prompt = hfin[:, :, :STRIP_STATES].reshape(1, nb, SSM_GROUPS, SSM_STATE)
    s5_im_prompt = hfin[:, :, STRIP_STATES:].reshape(1, nb, SSM_GROUPS, SSM_STATE)
    conv_prompt = jnp.swapaxes(tail_p.reshape(CONV_TAIL, nb, D_MODEL), 0, 1)[None]
    k_sample = ko_s.reshape(1, ns, WINDOW, N_KV_HEADS, HEAD_DIM)
    v_sample = vo_s.reshape(1, ns, WINDOW, N_KV_HEADS, HEAD_DIM)
    s5_re_sample = sre_s.reshape(1, ns, SSM_GROUPS, SSM_STATE)
    s5_im_sample = sim_s.reshape(1, ns, SSM_GROUPS, SSM_STATE)
    conv_sample = jnp.swapaxes(sto_t, 0, 1).reshape(1, ns, CONV_TAIL, D_MODEL)
    return (y_prompt, y_sample, k_prompt, v_prompt, s5_re_prompt, s5_im_prompt, conv_prompt,
            k_sample, v_sample, s5_re_sample, s5_im_sample, conv_sample)
```

```python
import functools

import jax
import jax.numpy as jnp
from jax import lax
from jax.experimental import pallas as pl
from jax.experimental.pallas import tpu as pltpu

F32 = jnp.float32
BF16 = jnp.bfloat16

LANE = 128
SUBLANE = 8
VMEM_LIMIT_BYTES = 56 * 1024 * 1024

D_MODEL = 1024
HEAD_DIM = 64
N_Q_HEADS = 8
N_KV_HEADS = 2
Q_PER_KV = N_Q_HEADS // N_KV_HEADS
WINDOW = 128
ATTN_WIDTH = N_Q_HEADS * HEAD_DIM
KV_WIDTH = N_KV_HEADS * HEAD_DIM
SSM_WIDTH = 512
SSM_GROUP = 16
SSM_GROUPS = 32
SSM_STATE = 64
D_FF = 2816
CONV_WIDTH = 31
CONV_TAIL = CONV_WIDTH - 1
RMS_EPS = 1e-6
LN_EPS = 1e-5

N_STRIP = SSM_WIDTH // LANE
STRIP_STATES = (SSM_GROUPS // N_STRIP) * SSM_STATE
D_SLABS = D_MODEL // LANE
ATTN_SLABS = ATTN_WIDTH // LANE
FFN_SPLIT = 4

SAMPLE_T = SUBLANE


def _rms(x, g):
    return x * lax.rsqrt(jnp.mean(x * x, axis=-1, keepdims=True) + RMS_EPS) * g


def _dot(a, b):
    return jnp.dot(a, b, preferred_element_type=F32)


def _dot_nt(a, b):
    return lax.dot_general(a, b, (((1,), (1,)), ((), ())), preferred_element_type=F32)


def _const_spec(shape):
    nd = len(shape)
    return pl.BlockSpec(shape, lambda *_: (0,) * nd, pipeline_mode=pl.Buffered(1))


def _params(n_axes):
    return pltpu.CompilerParams(
        dimension_semantics=("arbitrary",) * n_axes, vmem_limit_bytes=VMEM_LIMIT_BYTES)


def _load_time_major(x_ref, slab_scr, nb):
    steps = x_ref.shape[1]
    nslab = x_ref.shape[2] // LANE
    for b in range(nb):
        for j in range(nslab):
            slab_scr[j, pl.ds(b, steps, stride=nb), :] = x_ref[b, :, j * LANE:(j + 1) * LANE]
    return jnp.concatenate([slab_scr[j] for j in range(nslab)], axis=1)


def _store_sequence_major(val, slab_scr, out_ref, nb):
    steps = out_ref.shape[1]
    nslab = out_ref.shape[2] // LANE
    for j in range(nslab):
        slab_scr[j] = val[:, j * LANE:(j + 1) * LANE]
    for b in range(nb):
        for j in range(nslab):
            out_ref[b, :, j * LANE:(j + 1) * LANE] = (
                slab_scr[j, pl.ds(b, steps, stride=nb), :].astype(out_ref.dtype))


def _s5_disc_kernel(lre_ref, lim_ref, ls_ref, bre_ref, bim_ref, are_ref, aim_ref, bbre_ref, bbim_ref):
    lam_re = lre_ref[...]
    lam_im = lim_ref[...]
    dt = jnp.exp(ls_ref[...])
    mag = jnp.exp(lam_re * dt)
    lb_re = mag * jnp.cos(lam_im * dt)
    lb_im = mag * jnp.sin(lam_im * dt)
    den = lam_re * lam_re + lam_im * lam_im
    coef_re = ((lb_re - 1.0) * lam_re + lb_im * lam_im) / den
    coef_im = (lb_im * lam_re - (lb_re - 1.0) * lam_im) / den
    b_re = bre_ref[...]
    b_im = bim_ref[...]
    are_ref[...] = lb_re
    aim_ref[...] = lb_im
    bbre_ref[...] = coef_re * b_re - coef_im * b_im
    bbim_ref[...] = coef_re * b_im + coef_im * b_re


def _s5_tables(lam_re, lam_im, log_step, b_re, b_im, c_re, c_im):
    rep = lambda v: jnp.repeat(v, SSM_GROUP, axis=0)
    rows = SSM_GROUPS * SSM_GROUP
    bt = lambda b: jnp.swapaxes(b, 1, 2).reshape(rows, SSM_STATE)
    ls = jnp.broadcast_to(log_step[:, None], (SSM_GROUPS, SSM_STATE))
    shp = jax.ShapeDtypeStruct((rows, SSM_STATE), F32)
    a_re, a_im, bb_re, bb_im = pl.pallas_call(
        _s5_disc_kernel, out_shape=(shp, shp, shp, shp), name="s5_disc",
    )(rep(lam_re), rep(lam_im), rep(ls), bt(b_re), bt(b_im))
    a_re = a_re[::SSM_GROUP].reshape(N_STRIP, STRIP_STATES)
    a_im = a_im[::SSM_GROUP].reshape(N_STRIP, STRIP_STATES)
    gl = SSM_GROUPS // N_STRIP
    eye = jnp.eye(gl, dtype=F32).reshape(1, gl, 1, gl, 1)

    def in_blockdiag(m):
        m = m.reshape(N_STRIP, gl, SSM_GROUP, 1, SSM_STATE)
        return (m * eye).reshape(N_STRIP, LANE, STRIP_STATES)

    def out_blockdiag(m):
        m = jnp.swapaxes(m.reshape(N_STRIP, gl, SSM_GROUP, SSM_STATE), 2, 3)
        return (m.reshape(N_STRIP, gl, SSM_STATE, 1, SSM_GROUP) * eye).reshape(N_STRIP, STRIP_STATES, LANE)

    bb = jnp.concatenate([in_blockdiag(bb_re), in_blockdiag(bb_im)], axis=2).astype(BF16)
    cc = jnp.concatenate([out_blockdiag(c_re), -out_blockdiag(c_im)], axis=1).astype(BF16)
    return a_re, a_im, bb, cc


def _gelu_glu(y, wglu_ref):
    g = jax.nn.gelu(y)
    return g * jax.nn.sigmoid(_dot(g.astype(BF16), wglu_ref[...]))


def _band_attention(q, kv_scr, sinks, from_prev, prev_bias, seqs):
    gw = Q_PER_KV * HEAD_DIM
    left = lax.broadcasted_iota(jnp.int32, (2 * WINDOW, KV_WIDTH), 1) < HEAD_DIM
    head_of_lane = lax.shift_right_logical(
        lax.broadcasted_iota(jnp.int32, (WINDOW, gw), 1), HEAD_DIM.bit_length() - 1)
    scores, values = [], []
    for n, b in enumerate(seqs):
        kb = kv_scr[b, :, 0:KV_WIDTH]
        vb = kv_scr[b, :, KV_WIDTH:2 * KV_WIDTH]
        kr = pltpu.roll(kb, HEAD_DIM, axis=1)
        vr = pltpu.roll(vb, HEAD_DIM, axis=1)
        for g in range(N_KV_HEADS):
            kk = jnp.where(left, kb, kr) if g == 0 else jnp.where(left, kr, kb)
            vv = jnp.where(left, vb, vr) if g == 0 else jnp.where(left, vr, vb)
            k4 = jnp.concatenate([kk, kk], axis=1).astype(BF16)
            values.append(jnp.concatenate([vv, vv], axis=1).astype(BF16))
            qg = q[n * WINDOW:(n + 1) * WINDOW, g * gw:(g + 1) * gw]
            qst = jnp.concatenate(
                [jnp.where(head_of_lane == r, qg, 0.0) for r in range(Q_PER_KV)], axis=0).astype(BF16)
            scores.append(_dot_nt(qst, k4))
    nblk = len(scores)
    s = jnp.concatenate(scores, axis=0)
    prev = jnp.concatenate([from_prev] * nblk, axis=0)
    s = jnp.where(prev, s[:, :WINDOW] + prev_bias, s[:, WINDOW:])
    sink = jnp.concatenate(
        [jnp.full((WINDOW, WINDOW), sinks[(n % N_KV_HEADS) * Q_PER_KV + r], F32)
         for n in range(nblk) for r in range(Q_PER_KV)], axis=0)
    m = jnp.maximum(jnp.max(s, axis=1, keepdims=True), sink)
    p = jnp.exp(s - m)
    pb = p.astype(BF16)
    hrows = pb.shape[0] // 2
    ri = lax.broadcasted_iota(jnp.int32, (2 * WINDOW, 2 * WINDOW), 0) < WINDOW
    ci = lax.broadcasted_iota(jnp.int32, (2 * WINDOW, 2 * WINDOW), 1) < WINDOW
    ones2 = jnp.where(ri == ci, 1.0, 0.0).astype(BF16)
    sums = _dot(jnp.concatenate([pb[:hrows], pb[hrows:]], axis=1), ones2)
    den = jnp.concatenate([sums[:, :WINDOW], sums[:, WINDOW:]], axis=0) + jnp.exp(sink - m)
    p = p / den
    p = jnp.concatenate([jnp.where(prev, p, 0.0).astype(BF16), jnp.where(prev, 0.0, p).astype(BF16)], axis=1)
    rows = Q_PER_KV * WINDOW
    outs = []
    for b in range(len(seqs)):
        og = []
        for g in range(N_KV_HEADS):
            n = b * N_KV_HEADS + g
            o4 = _dot(p[n * rows:(n + 1) * rows], values[n])
            o = jnp.where(head_of_lane == 0, o4[0:WINDOW], 0.0)
            for r in range(1, Q_PER_KV):
                o = jnp.where(head_of_lane == r, o4[r * WINDOW:(r + 1) * WINDOW], o)
            og.append(o)
        outs.append(jnp.concatenate(og, axis=1))
    return outs


def _proj_attn_kernel(sink_ref, x_ref, g_ref, w_ref, au_ref, kk_ref, vk_ref, kv_scr, *, nb):
    i = pl.program_id(0)
    z = jnp.concatenate(
        [_dot(_rms(x_ref[b:b + 2].reshape(2 * WINDOW, D_MODEL), g_ref[...]).astype(BF16), w_ref[...])
         for b in range(0, nb, 2)], axis=0)
    ucol = ATTN_WIDTH + 2 * KV_WIDTH

    @pl.when(i == 0)
    def _():
        kv_scr[:, 0:WINDOW, :] = jnp.zeros((nb, WINDOW, 2 * KV_WIDTH), F32)

    for b in range(nb):
        kv_scr[b, WINDOW:2 * WINDOW, :] = z[b * WINDOW:(b + 1) * WINDOW, ATTN_WIDTH:ucol]
    q = z[:, :ATTN_WIDTH] * (HEAD_DIM ** -0.5)

    shape = (Q_PER_KV * WINDOW, WINDOW)
    row = jnp.bitwise_and(lax.broadcasted_iota(jnp.int32, shape, 0), WINDOW - 1)
    col = lax.broadcasted_iota(jnp.int32, shape, 1)
    prev_bias = jnp.where(i == 0, -jnp.inf, 0.0).astype(F32)
    sinks = [sink_ref[hh] for hh in range(N_Q_HEADS)]
    outs = _band_attention(q, kv_scr, sinks, col > row, prev_bias, range(nb))
    for b in range(nb):
        zb = z[b * WINDOW:(b + 1) * WINDOW]
        for j in range(ATTN_SLABS):
            au_ref[j, pl.ds(b, WINDOW, stride=nb), :] = outs[b][:, j * LANE:(j + 1) * LANE]
        for j in range(N_STRIP):
            au_ref[ATTN_SLABS + j, pl.ds(b, WINDOW, stride=nb), :] = zb[:, ucol + j * LANE:ucol + (j + 1) * LANE]
        kk_ref[b] = zb[:, ATTN_WIDTH:ATTN_WIDTH + KV_WIDTH]
        vk_ref[b] = zb[:, ATTN_WIDTH + KV_WIDTH:ucol]
    kv_scr[:, 0:WINDOW, :] = kv_scr[:, WINDOW:2 * WINDOW, :]


def _prompt_proj_attn(x, g_pre, w_in, sinks):
    nb, seq, _ = x.shape
    nslab = ATTN_SLABS + N_STRIP
    return pl.pallas_call(
        functools.partial(_proj_attn_kernel, nb=nb),
        grid=(seq // WINDOW,),
        in_specs=[
            pl.BlockSpec(memory_space=pltpu.SMEM),
            pl.BlockSpec((nb, WINDOW, D_MODEL), lambda i: (0, i, 0)),
            _const_spec((1, D_MODEL)),
            _const_spec(w_in.shape),
        ],
        out_specs=[
            pl.BlockSpec((nslab, WINDOW * nb, LANE), lambda i: (0, i, 0)),
            pl.BlockSpec((nb, WINDOW, KV_WIDTH), lambda i: (0, 0, 0)),
            pl.BlockSpec((nb, WINDOW, KV_WIDTH), lambda i: (0, 0, 0)),
        ],
        out_shape=[
            jax.ShapeDtypeStruct((nslab, seq * nb, LANE), F32),
            jax.ShapeDtypeStruct((nb, WINDOW, KV_WIDTH), F32),
            jax.ShapeDtypeStruct((nb, WINDOW, KV_WIDTH), F32),
        ],
        scratch_shapes=[pltpu.VMEM((nb, 2 * WINDOW, 2 * KV_WIDTH), F32)],
        compiler_params=_params(1),
        name="prompt_proj_attn",
    )(sinks, x, g_pre, w_in)


def _s5_prompt_kernel(au_ref, bb_ref, cc_ref, are_ref, aim_ref, d_ref, wglu_ref, wout_ref,
                      m_ref, hfin_ref, hbuf, slab_scr, *, steps, nb):
    i = pl.program_id(0)

    @pl.when(i == 0)
    def _():
        hfin_ref[...] = jnp.zeros(hfin_ref.shape, F32)

    ys = []
    for j in range(N_STRIP):
        uj = au_ref[ATTN_SLABS + j]
        hbuf[...] = _dot(uj.astype(BF16), bb_ref[j])
        are = jnp.broadcast_to(are_ref[j:j + 1, :], (nb, STRIP_STATES))
        aim = jnp.broadcast_to(aim_ref[j:j + 1, :], (nb, STRIP_STATES))

        def step(t, carry):
            hr, hi = carry
            r = pl.multiple_of(t * nb, nb)
            br = hbuf[pl.ds(r, nb), 0:STRIP_STATES]
            bi = hbuf[pl.ds(r, nb), STRIP_STATES:2 * STRIP_STATES]
            nr = are * hr - aim * hi + br
            ni = are * hi + aim * hr + bi
            hbuf[pl.ds(r, nb), 0:STRIP_STATES] = nr
            hbuf[pl.ds(r, nb), STRIP_STATES:2 * STRIP_STATES] = ni
            return nr, ni

        h0 = (hfin_ref[j, :, 0:STRIP_STATES], hfin_ref[j, :, STRIP_STATES:2 * STRIP_STATES])
        hr, hi = lax.fori_loop(0, steps, step, h0, unroll=True)
        hfin_ref[j, :, 0:STRIP_STATES] = hr
        hfin_ref[j, :, STRIP_STATES:2 * STRIP_STATES] = hi
        ys.append(_dot(hbuf[...].astype(BF16), cc_ref[j]) + d_ref[:, j * LANE:(j + 1) * LANE] * uj)
    s = _gelu_glu(jnp.concatenate(ys, axis=1), wglu_ref)
    a = jnp.concatenate([au_ref[j] for j in range(ATTN_SLABS)], axis=1).astype(BF16)
    m = _dot(a, wout_ref[0:ATTN_WIDTH, :]) + _dot(s.astype(BF16), wout_ref[ATTN_WIDTH:, :])
    _store_sequence_major(m, slab_scr, m_ref, nb)


def _prompt_s5(au, tabs, d_skip, w_glu, w_out, *, nb, steps):
    a_re, a_im, bb, cc = tabs
    nslab, rows, _ = au.shape
    blk = steps * nb
    return pl.pallas_call(
        functools.partial(_s5_prompt_kernel, steps=steps, nb=nb),
        grid=(rows // blk,),
        in_specs=[
            pl.BlockSpec((nslab, blk, LANE), lambda i: (0, i, 0)),
            _const_spec(bb.shape), _const_spec(cc.shape),
            _const_spec(a_re.shape), _const_spec(a_im.shape),
            _const_spec((1, SSM_WIDTH)),
            _const_spec(w_glu.shape), _const_spec(w_out.shape),
        ],
        out_specs=[
            pl.BlockSpec((nb, steps, D_MODEL), lambda i: (0, i, 0)),
            pl.BlockSpec((N_STRIP, nb, 2 * STRIP_STATES), lambda i: (0, 0, 0)),
        ],
        out_shape=[
            jax.ShapeDtypeStruct((nb, rows // nb, D_MODEL), BF16),
            jax.ShapeDtypeStruct((N_STRIP, nb, 2 * STRIP_STATES), F32),
        ],
        scratch_shapes=[pltpu.VMEM((blk, 2 * STRIP_STATES), F32),
                        pltpu.VMEM((D_SLABS, blk, LANE), F32)],
        compiler_params=_params(1),
        name="prompt_s5",
    )(au, bb, cc, a_re, a_im, d_skip, w_glu, w_out)


def _post_ffn_kernel(x_ref, m_ref, gpm_ref, gpf_ref, gof_ref, wg_ref, wu_ref, wd_ref, o_ref):
    group = x_ref.shape[0] // FFN_SPLIT
    for r in range(FFN_SPLIT):
        rows = slice(r * group, (r + 1) * group)
        x1 = x_ref[rows, :] + _rms(m_ref[rows, :].astype(F32), gpm_ref[...])
        h = _rms(x1, gpf_ref[...]).astype(BF16)
        gate = _dot(h, wg_ref[...])
        up = _dot(h, wu_ref[...])
        act = (gate * jax.nn.sigmoid(gate) * up).astype(BF16)
        f = _dot(act, wd_ref[...])
        o_ref[rows, :] = x1 + _rms(f, gof_ref[...])


def _post_ffn(x, m, gains, ffn_w, layer, *, rows, name):
    gpm, gpf, gof = gains
    wg, wu, wd = ffn_w
    row_spec = pl.BlockSpec((rows, D_MODEL), lambda i: (i, 0))
    layer_spec = lambda w: pl.BlockSpec(
        (None,) + w.shape[1:], lambda i: (layer, 0, 0), pipeline_mode=pl.Buffered(1))
    return pl.pallas_call(
        _post_ffn_kernel,
        grid=(x.shape[0] // rows,),
        in_specs=[row_spec, row_spec,
                  _const_spec((1, D_MODEL)), _const_spec((1, D_MODEL)), _const_spec((1, D_MODEL)),
                  layer_spec(wg), layer_spec(wu), layer_spec(wd)],
        out_specs=row_spec,
        out_shape=jax.ShapeDtypeStruct(x.shape, F32),
        compiler_params=_params(1),
        name=name,
    )(x, m, gpm, gpf, gof, wg, wu, wd)


CONV_HALO = 32
CONV_ROWS = 64


def _layer_norm_silu(c, lng_ref, lnb_ref):
    mu = jnp.mean(c, axis=-1, keepdims=True)
    cc = c - mu
    var = jnp.mean(cc * cc, axis=-1, keepdims=True)
    cn = cc * lax.rsqrt(var + LN_EPS) * lng_ref[...] + lnb_ref[...]
    return cn * jax.nn.sigmoid(cn)


def _conv_prompt_kernel(x_ref, g_ref, w1_ref, b1_ref, wdw_ref, bdw_ref, lng_ref, lnb_ref, w2_ref, b2_ref,
                        m_ref, tail_ref, gp_scr, c_scr, slab_scr, *, blk, nb):
    i = pl.program_id(0)
    halo = CONV_HALO * nb
    tail = CONV_TAIL * nb

    @pl.when(i == 0)
    def _():
        gp_scr[0:halo, :] = jnp.zeros((halo, D_MODEL), F32)

    x = _load_time_major(x_ref, slab_scr, nb)
    h = _rms(x, g_ref[...]).astype(BF16)
    a = _dot(h, w1_ref[...]) + b1_ref[...]
    gp_scr[halo:halo + blk, :] = a[:, :D_MODEL] * jax.nn.sigmoid(a[:, D_MODEL:])

    def chunk(ci, carry):
        base = pl.multiple_of(ci * CONV_ROWS, CONV_ROWS)
        for l in range(D_SLABS):
            lanes = slice(l * LANE, (l + 1) * LANE)
            xs = gp_scr[pl.ds(base + halo - tail, CONV_ROWS + tail), lanes]
            acc = jnp.broadcast_to(bdw_ref[:, lanes], (CONV_ROWS, LANE))
            for j in range(CONV_WIDTH):
                wj = wdw_ref[j * SUBLANE:(j + 1) * SUBLANE, lanes]
                wj = jnp.concatenate([wj] * (CONV_ROWS // SUBLANE), axis=0)
                acc = acc + wj * xs[j * nb:j * nb + CONV_ROWS]
            c_scr[pl.ds(base, CONV_ROWS), lanes] = acc
        return carry

    lax.fori_loop(0, blk // CONV_ROWS, chunk, 0)
    act = _layer_norm_silu(c_scr[...], lng_ref, lnb_ref).astype(BF16)
    _store_sequence_major(_dot(act, w2_ref[...]) + b2_ref[...], slab_scr, m_ref, nb)
    tail_ref[...] = gp_scr[halo + blk - tail:halo + blk, :]
    gp_scr[0:halo, :] = gp_scr[blk:blk + halo, :]


def _prompt_conv(x, g_pre, cw, *, steps):
    w1, b1, wdw8, bdw, lng, lnb, w2, b2 = cw
    nb, seq, _ = x.shape
    blk = steps * nb
    assert steps >= CONV_HALO and blk % CONV_ROWS == 0
    vec = lambda n: _const_spec((1, n))
    seq_spec = pl.BlockSpec((nb, steps, D_MODEL), lambda i: (0, i, 0))
    return pl.pallas_call(
        functools.partial(_conv_prompt_kernel, blk=blk, nb=nb),
        grid=(seq // steps,),
        in_specs=[
            seq_spec,
            vec(D_MODEL), _const_spec(w1.shape), vec(2 * D_MODEL),
            _const_spec(wdw8.shape), vec(D_MODEL), vec(D_MODEL), vec(D_MODEL),
            _const_spec(w2.shape), vec(D_MODEL),
        ],
        out_specs=[
            seq_spec,
            pl.BlockSpec((CONV_TAIL * nb, D_MODEL), lambda i: (0, 0)),
        ],
        out_shape=[
            jax.ShapeDtypeStruct((nb, seq, D_MODEL), BF16),
            jax.ShapeDtypeStruct((CONV_TAIL * nb, D_MODEL), F32),
        ],
        scratch_shapes=[pltpu.VMEM((blk + CONV_HALO * nb, D_MODEL), F32),
                        pltpu.VMEM((blk, D_MODEL), F32),
                        pltpu.VMEM((D_SLABS, blk, LANE), F32)],
        compiler_params=_params(1),
        name="prompt_conv",
    )(x, g_pre, w1, b1, wdw8, bdw, lng, lnb, w2, b2)


def _sample_l0_kernel(sink_ref, x_ref, g_ref, w_ref, ck_ref, cv_ref, sre_ref, sim_ref,
                      bb_ref, cc_ref, are_ref, aim_ref, d_ref, wglu_ref, wout_ref,
                      m_ref, ko_ref, vo_ref, ore_ref, oim_ref,
                      q_scr, k_scr, v_scr, a_scr, u_scr, y_scr, *, nseq, ntok):
    qw = N_Q_HEADS * LANE
    h = _rms(x_ref[...], g_ref[...]).astype(BF16)
    z = _dot(h, w_ref[...])
    q_scr[...] = z[:, :qw] * (HEAD_DIM ** -0.5)
    kn = z[:, qw:qw + KV_WIDTH]
    vn = z[:, qw + KV_WIDTH:qw + 2 * KV_WIDTH]
    k_scr[...] = kn
    v_scr[...] = vn
    for j in range(N_STRIP):
        u_scr[j] = z[:, qw + 2 * KV_WIDTH + j * LANE:qw + 2 * KV_WIDTH + (j + 1) * LANE]
    y_scr[...] = jnp.zeros(y_scr.shape, F32)

    for j in range(N_STRIP):
        hr = sre_ref[:, j * STRIP_STATES:(j + 1) * STRIP_STATES]
        hi = sim_ref[:, j * STRIP_STATES:(j + 1) * STRIP_STATES]
        are = are_ref[j:j + 1, :]
        aim = aim_ref[j:j + 1, :]
        dj = d_ref[:, j * LANE:(j + 1) * LANE]
        for t in range(ntok):
            ut = u_scr[j, pl.ds(t, nseq, stride=SAMPLE_T), :]
            bu = _dot(ut.astype(BF16), bb_ref[j])
            hr, hi = (are * hr - aim * hi + bu[:, :STRIP_STATES],
                      are * hi + aim * hr + bu[:, STRIP_STATES:])
            hcat = jnp.concatenate([hr, hi], axis=1).astype(BF16)
            y_scr[j, pl.ds(t, nseq, stride=SAMPLE_T), :] = _dot(hcat, cc_ref[j]) + dj * ut
        ore_ref[:, j * STRIP_STATES:(j + 1) * STRIP_STATES] = hr
        oim_ref[:, j * STRIP_STATES:(j + 1) * STRIP_STATES] = hi
    s = _gelu_glu(jnp.concatenate([y_scr[j] for j in range(N_STRIP)], axis=1), wglu_ref)

    nkeys = WINDOW + SAMPLE_T
    shape = (N_Q_HEADS * SAMPLE_T, nkeys)
    tq = jnp.bitwise_and(lax.broadcasted_iota(jnp.int32, shape, 0), SAMPLE_T - 1)
    col = lax.broadcasted_iota(jnp.int32, shape, 1)
    ninf = jnp.float32(-jnp.inf)
    bias = jnp.where(col < WINDOW, jnp.where(col > tq, 0.0, ninf),
                     jnp.where(col - WINDOW <= tq, 0.0, ninf))
    sink = jnp.concatenate(
        [jnp.full((SAMPLE_T, 1), sink_ref[hh], F32) for hh in range(N_Q_HEADS)], axis=0)
    left = lax.broadcasted_iota(jnp.int32, (SAMPLE_T, LANE), 1) < HEAD_DIM

    def seq_body(n, carry):
        r = pl.multiple_of(n * SAMPLE_T, SAMPLE_T)
        qs = jnp.concatenate(
            [q_scr[pl.ds(r, SAMPLE_T), hh * LANE:(hh + 1) * LANE] for hh in range(N_Q_HEADS)], axis=0)
        kall = jnp.concatenate([ck_ref[n], k_scr[pl.ds(r, SAMPLE_T), :]], axis=0)
        vall = jnp.concatenate([cv_ref[n], v_scr[pl.ds(r, SAMPLE_T), :]], axis=0)
        ko_ref[n] = kall[ntok:ntok + WINDOW]
        vo_ref[n] = vall[ntok:ntok + WINDOW]
        sc = _dot_nt(qs.astype(BF16), kall.astype(BF16)) + bias
        mx = jnp.maximum(jnp.max(sc, axis=1, keepdims=True), sink)
        p = jnp.exp(sc - mx)
        den = jnp.sum(p, axis=1, keepdims=True) + jnp.exp(sink - mx)
        p = (p / den).astype(BF16)
        o1 = _dot(p, vall.astype(BF16))
        o2 = _dot(p, pltpu.roll(vall, HEAD_DIM, axis=1).astype(BF16))
        slabs = []
        for hp in range(N_Q_HEADS // 2):
            lo, ro = (o1, o2) if hp < Q_PER_KV // 2 else (o2, o1)
            lrow = 2 * hp * SAMPLE_T
            slabs.append(jnp.where(left, lo[lrow:lrow + SAMPLE_T], ro[lrow + SAMPLE_T:lrow + 2 * SAMPLE_T]))
        a_scr[pl.ds(r, SAMPLE_T), :] = jnp.concatenate(slabs, axis=1)
        return carry

    lax.fori_loop(0, nseq, seq_body, 0, unroll=SAMPLE_ATTN_UNROLL)
    m = (_dot(a_scr[...].astype(BF16), wout_ref[0:ATTN_WIDTH, :])
         + _dot(s.astype(BF16), wout_ref[ATTN_WIDTH:, :]))
    m_ref[...] = m.astype(m_ref.dtype)


def _sample_l0(x, g_pre, w_in_s, sinks, ck, cv, s_re, s_im, tabs, d_skip, w_glu, w_out, *, nseq, ntok):
    a_re, a_im, bb, cc = tabs
    rows_all = x.shape[0]
    rows = nseq * SAMPLE_T
    nstate = SSM_GROUPS * SSM_STATE
    row_spec = lambda w: pl.BlockSpec((rows, w), lambda i: (i, 0))
    return pl.pallas_call(
        functools.partial(_sample_l0_kernel, nseq=nseq, ntok=ntok),
        grid=(rows_all // rows,),
        in_specs=[
            pl.BlockSpec(memory_space=pltpu.SMEM),
            row_spec(D_MODEL), _const_spec((1, D_MODEL)), _const_spec(w_in_s.shape),
            pl.BlockSpec((nseq, WINDOW, KV_WIDTH), lambda i: (i, 0, 0)),
            pl.BlockSpec((nseq, WINDOW, KV_WIDTH), lambda i: (i, 0, 0)),
            pl.BlockSpec((nseq, nstate), lambda i: (i, 0)),
            pl.BlockSpec((nseq, nstate), lambda i: (i, 0)),
            _const_spec(bb.shape), _const_spec(cc.shape),
            _const_spec(a_re.shape), _const_spec(a_im.shape),
            _const_spec((1, SSM_WIDTH)),
            _const_spec(w_glu.shape), _const_spec(w_out.shape),
        ],
        out_specs=[
            row_spec(D_MODEL),
            pl.BlockSpec((nseq, WINDOW, KV_WIDTH), lambda i: (i, 0, 0)),
            pl.BlockSpec((nseq, WINDOW, KV_WIDTH), lambda i: (i, 0, 0)),
            pl.BlockSpec((nseq, nstate), lambda i: (i, 0)),
            pl.BlockSpec((nseq, nstate), lambda i: (i, 0)),
        ],
        out_shape=[
            jax.ShapeDtypeStruct((rows_all, D_MODEL), BF16),
            jax.ShapeDtypeStruct(ck.shape, F32),
            jax.ShapeDtypeStruct(cv.shape, F32),
            jax.ShapeDtypeStruct((rows_all // SAMPLE_T, nstate), F32),
            jax.ShapeDtypeStruct((rows_all // SAMPLE_T, nstate), F32),
        ],
        scratch_shapes=[
            pltpu.VMEM((rows, N_Q_HEADS * LANE), F32),
            pltpu.VMEM((rows, KV_WIDTH), F32),
            pltpu.VMEM((rows, KV_WIDTH), F32),
            pltpu.VMEM((rows, ATTN_WIDTH), F32),
            pltpu.VMEM((N_STRIP, rows, LANE), F32),
            pltpu.VMEM((N_STRIP, rows, LANE), F32),
        ],
        compiler_params=_params(1),
        name="sample_l0",
    )(sinks, x, g_pre, w_in_s, ck, cv, s_re, s_im, bb, cc, a_re, a_im, d_skip, w_glu, w_out)


def _sample_l1_kernel(x_ref, g_ref, w1_ref, b1_ref, st_ref, wdw_ref, bdw_ref, lng_ref, lnb_ref,
                      w2_ref, b2_ref, m_ref, sto_ref, g_scr, c_scr, *, nseq, ntok):
    h = _rms(x_ref[...], g_ref[...]).astype(BF16)
    a = _dot(h, w1_ref[...]) + b1_ref[...]
    gg = a[:, :D_MODEL] * jax.nn.sigmoid(a[:, D_MODEL:])
    for j in range(D_SLABS):
        g_scr[j] = gg[:, j * LANE:(j + 1) * LANE]
    c_scr[...] = jnp.zeros(c_scr.shape, F32)
    keep = CONV_TAIL - ntok
    for j in range(D_SLABS):
        lanes = slice(j * LANE, (j + 1) * LANE)
        new = [g_scr[j, pl.ds(t, nseq, stride=SAMPLE_T), :] for t in range(ntok)]
        acc = [jnp.broadcast_to(bdw_ref[:, lanes], (nseq, LANE)) for _ in range(ntok)]
        for r in range(CONV_TAIL):
            row = st_ref[r, :, lanes]
            if r >= ntok:
                sto_ref[r - ntok, :, lanes] = row
            for t in range(min(r, ntok - 1) + 1):
                acc[t] = acc[t] + row * wdw_ref[r - t:r - t + 1, lanes]
        for t in range(ntok):
            sto_ref[keep + t, :, lanes] = new[t]
            for t2 in range(t + 1):
                tap = CONV_TAIL + t2 - t
                acc[t] = acc[t] + new[t2] * wdw_ref[tap:tap + 1, lanes]
            c_scr[j, pl.ds(t, nseq, stride=SAMPLE_T), :] = acc[t]
    c = jnp.concatenate([c_scr[j] for j in range(D_SLABS)], axis=1)
    act = _layer_norm_silu(c, lng_ref, lnb_ref).astype(BF16)
    m_ref[...] = (_dot(act, w2_ref[...]) + b2_ref[...]).astype(m_ref.dtype)


def _sample_l1(x, g_pre, cw, st, wdw, *, nseq, ntok):
    w1, b1, _, bdw, lng, lnb, w2, b2 = cw
    rows_all = x.shape[0]
    rows = nseq * SAMPLE_T
    vec = lambda n: _const_spec((1, n))
    row_spec = pl.BlockSpec((rows, D_MODEL), lambda i: (i, 0))
    st_spec = pl.BlockSpec((CONV_TAIL, nseq, D_MODEL), lambda i: (0, i, 0))
    return pl.pallas_call(
        functools.partial(_sample_l1_kernel, nseq=nseq, ntok=ntok),
        grid=(rows_all // rows,),
        in_specs=[
            row_spec, vec(D_MODEL), _const_spec(w1.shape), vec(2 * D_MODEL),
            st_spec, _const_spec(wdw.shape),
            vec(D_MODEL), vec(D_MODEL), vec(D_MODEL), _const_spec(w2.shape), vec(D_MODEL),
        ],
        out_specs=[row_spec, st_spec],
        out_shape=[
            jax.ShapeDtypeStruct((rows_all, D_MODEL), BF16),
            jax.ShapeDtypeStruct(st.shape, F32),
        ],
        scratch_shapes=[pltpu.VMEM((D_SLABS, rows, LANE), F32),
                        pltpu.VMEM((D_SLABS, rows, LANE), F32)],
        compiler_params=_params(1),
        name="sample_l1",
    )(x, g_pre, w1, b1, st, wdw, bdw, lng, lnb, w2, b2)


PROMPT_FFN_ROWS = 1024
PROMPT_STEPS = 64
SAMPLE_SEQS = 32
SAMPLE_FFN_ROWS = 512
SAMPLE_ATTN_UNROLL = 8


def kernel(x_prompt, x_sample, cache_k, cache_v, state_s5_re, state_s5_im, state_conv, norm_pre_mix, norm_post_mix, norm_pre_ffn, norm_post_ffn, w_in_ab, attn_sinks, s5_lambda_re, s5_lambda_im, s5_log_step, s5_b_re, s5_b_im, s5_c_re, s5_c_im, s5_d, w_glu, w_out_ab, w_pw1, b_pw1, w_dw, b_dw, conv_ln_g, conv_ln_b, w_pw2, b_pw2, w_ffn_gate, w_ffn_up, w_ffn_down):
    nb, seq, _ = x_prompt.shape
    ns, ntok, _ = x_sample.shape
    assert nb == SUBLANE and ntok <= SAMPLE_T and seq % WINDOW == 0 and seq % PROMPT_STEPS == 0
    row = lambda v: v.reshape(1, -1)
    gains = lambda layer: tuple(row(n[layer]) for n in (norm_post_mix, norm_pre_ffn, norm_post_ffn))

    w_in = w_in_ab[0].astype(BF16)
    wq = w_in_ab[0][:, :ATTN_WIDTH].reshape(D_MODEL, N_Q_HEADS, HEAD_DIM)
    lane_kv = (jnp.arange(LANE) // HEAD_DIM)[None, :]
    head_kv = (jnp.arange(N_Q_HEADS) // Q_PER_KV)[:, None]
    wq_slab = jnp.where((lane_kv == head_kv)[None], jnp.concatenate([wq, wq], axis=2), 0.0)
    w_in_s = jnp.concatenate(
        [wq_slab.reshape(D_MODEL, N_Q_HEADS * LANE), w_in_ab[0][:, ATTN_WIDTH:]], axis=1).astype(BF16)
    w_glu_b = w_glu[0].astype(BF16)
    w_out = w_out_ab[0].astype(BF16)
    ffn_w = (w_ffn_gate.astype(BF16), w_ffn_up.astype(BF16), w_ffn_down.astype(BF16))
    sinks = attn_sinks[0]
    tabs = _s5_tables(s5_lambda_re[0], s5_lambda_im[0], s5_log_step[0], s5_b_re[0], s5_b_im[0],
                      s5_c_re[0], s5_c_im[0])
    d_skip = row(s5_d)
    wdw = w_dw[0]
    conv_w = (w_pw1[0].astype(BF16), row(b_pw1), jnp.repeat(wdw, SUBLANE, axis=0), row(b_dw),
              row(conv_ln_g), row(conv_ln_b), w_pw2[0].astype(BF16), row(b_pw2))

    flat = lambda v: v.reshape(nb * seq, D_MODEL)
    au_p, kk_p, vk_p = _prompt_proj_attn(x_prompt, row(norm_pre_mix[0]), w_in, sinks)
    m0_p, hfin_p = _prompt_s5(au_p, tabs, d_skip, w_glu_b, w_out, nb=nb, steps=PROMPT_STEPS)
    x1_p = _post_ffn(flat(x_prompt), flat(m0_p), gains(0), ffn_w, 0, rows=PROMPT_FFN_ROWS, name="prompt_ffn0")
    m1_p, tail_p = _prompt_conv(x1_p.reshape(nb, seq, D_MODEL), row(norm_pre_mix[1]), conv_w, steps=PROMPT_STEPS)
    y_prompt = _post_ffn(x1_p, flat(m1_p), gains(1), ffn_w, 1, rows=PROMPT_FFN_ROWS, name="prompt_ffn1")
    y_prompt = y_prompt.reshape(nb, seq, D_MODEL)

    pad_rows = lambda v: jnp.pad(
        v.reshape(ns, ntok, D_MODEL), ((0, 0), (0, SAMPLE_T - ntok), (0, 0))).reshape(ns * SAMPLE_T, D_MODEL)
    real_rows = lambda v: v.reshape(ns, SAMPLE_T, D_MODEL)[:, :ntok].reshape(ns * ntok, D_MODEL)
    xs_c = x_sample.reshape(ns * ntok, D_MODEL)
    xs = pad_rows(xs_c)
    nstate = SSM_GROUPS * SSM_STATE
    m0_s, ko_s, vo_s, sre_s, sim_s = _sample_l0(
        xs, row(norm_pre_mix[0]), w_in_s, sinks,
        cache_k.reshape(ns, WINDOW, KV_WIDTH), cache_v.reshape(ns, WINDOW, KV_WIDTH),
        state_s5_re.reshape(ns, nstate), state_s5_im.reshape(ns, nstate),
        tabs, d_skip, w_glu_b, w_out, nseq=SAMPLE_SEQS, ntok=ntok)
    x1_c = _post_ffn(xs_c, real_rows(m0_s), gains(0), ffn_w, 0, rows=SAMPLE_FFN_ROWS, name="sample_ffn0")
    st_t = jnp.swapaxes(state_conv.reshape(ns, CONV_TAIL, D_MODEL), 0, 1)
    m1_s, sto_t = _sample_l1(
        pad_rows(x1_c), row(norm_pre_mix[1]), conv_w, st_t, wdw, nseq=SAMPLE_SEQS, ntok=ntok)
    y_c = _post_ffn(x1_c, real_rows(m1_s), gains(1), ffn_w, 1, rows=SAMPLE_FFN_ROWS, name="sample_ffn1")

    y_sample = y_c.reshape(ns, ntok, D_MODEL)
    k_prompt = kk_p.reshape(1, nb, WINDOW, N_KV_HEADS, HEAD_DIM)
    v_prompt = vk_p.reshape(1, nb, WINDOW, N_KV_HEADS, HEAD_DIM)
    hfin = jnp.swapaxes(hfin_p, 0, 1)
    s5_re_prompt = hfin[:, :, :STRIP_STATES].reshape(1, nb, SSM_GROUPS, SSM_STATE)
    s5_im_prompt = hfin[:, :, STRIP_STATES:].reshape(1, nb, SSM_GROUPS, SSM_STATE)
    conv_prompt = jnp.swapaxes(tail_p.reshape(CONV_TAIL, nb, D_MODEL), 0, 1)[None]
    k_sample = ko_s.reshape(1, ns, WINDOW, N_KV_HEADS, HEAD_DIM)
    v_sample = vo_s.reshape(1, ns, WINDOW, N_KV_HEADS, HEAD_DIM)
    s5_re_sample = sre_s.reshape(1, ns, SSM_GROUPS, SSM_STATE)
    s5_im_sample = sim_s.reshape(1, ns, SSM_GROUPS, SSM_STATE)
    conv_sample = jnp.swapaxes(sto_t, 0, 1).reshape(1, ns, CONV_TAIL, D_MODEL)
    return (y_prompt, y_sample, k_prompt, v_prompt, s5_re_prompt, s5_im_prompt, conv_prompt,
            k_sample, v_sample, s5_re_sample, s5_im_sample, conv_sample)
```

```python
import functools

import jax
import jax.numpy as jnp
from jax import lax
from jax.experimental import pallas as pl
from jax.experimental.pallas import tpu as pltpu

F32 = jnp.float32
BF16 = jnp.bfloat16

LANE = 128
SUBLANE = 8
VMEM_LIMIT_BYTES = 56 * 1024 * 1024

D_MODEL = 1024
HEAD_DIM = 64
N_Q_HEADS = 8
N_KV_HEADS = 2
Q_PER_KV = N_Q_HEADS // N_KV_HEADS
WINDOW = 128
ATTN_WIDTH = N_Q_HEADS * HEAD_DIM
KV_WIDTH = N_KV_HEADS * HEAD_DIM
SSM_WIDTH = 512
SSM_GROUP = 16
SSM_GROUPS = 32
SSM_STATE = 64
D_FF = 2816
CONV_WIDTH = 31
CONV_TAIL = CONV_WIDTH - 1
RMS_EPS = 1e-6
LN_EPS = 1e-5

N_STRIP = SSM_WIDTH // LANE
STRIP_STATES = (SSM_GROUPS // N_STRIP) * SSM_STATE
D_SLABS = D_MODEL // LANE
ATTN_SLABS = ATTN_WIDTH // LANE
FFN_SPLIT = 4

SAMPLE_T = SUBLANE


def _rms(x, g):
    return x * lax.rsqrt(jnp.mean(x * x, axis=-1, keepdims=True) + RMS_EPS) * g


def _dot(a, b):
    return jnp.dot(a, b, preferred_element_type=F32)


def _dot_nt(a, b):
    return lax.dot_general(a, b, (((1,), (1,)), ((), ())), preferred_element_type=F32)


def _const_spec(shape):
    nd = len(shape)
    return pl.BlockSpec(shape, lambda *_: (0,) * nd, pipeline_mode=pl.Buffered(1))


def _params(vmem_mib):
    assert vmem_mib * 1024 * 1024 <= VMEM_LIMIT_BYTES
    return pltpu.CompilerParams(
        dimension_semantics=("arbitrary",), vmem_limit_bytes=vmem_mib * 1024 * 1024)


def _load_time_major(x_ref, slab_scr, nb):
    steps = x_ref.shape[1]
    nslab = x_ref.shape[2] // LANE
    for b in range(nb):
        for j in range(nslab):
            slab_scr[j, pl.ds(b, steps, stride=nb), :] = x_ref[b, :, j * LANE:(j + 1) * LANE]
    return jnp.concatenate([slab_scr[j] for j in range(nslab)], axis=1)


def _store_sequence_major(val, slab_scr, out_ref, nb):
    steps = out_ref.shape[1]
    nslab = out_ref.shape[2] // LANE
    for j in range(nslab):
        slab_scr[j] = val[:, j * LANE:(j + 1) * LANE]
    for b in range(nb):
        for j in range(nslab):
            out_ref[b, :, j * LANE:(j + 1) * LANE] = (
                slab_scr[j, pl.ds(b, steps, stride=nb), :].astype(out_ref.dtype))


def _s5_disc_kernel(lre_ref, lim_ref, ls_ref, bre_ref, bim_ref, are_ref, aim_ref, bbre_ref, bbim_ref):
    lam_re = lre_ref[...]
    lam_im = lim_ref[...]
    dt = jnp.exp(ls_ref[...])
    mag = jnp.exp(lam_re * dt)
    lb_re = mag * jnp.cos(lam_im * dt)
    lb_im = mag * jnp.sin(lam_im * dt)
    den = lam_re * lam_re + lam_im * lam_im
    coef_re = ((lb_re - 1.0) * lam_re + lb_im * lam_im) / den
    coef_im = (lb_im * lam_re - (lb_re - 1.0) * lam_im) / den
    b_re = bre_ref[...]
    b_im = bim_ref[...]
    are_ref[...] = lb_re
    aim_ref[...] = lb_im
    bbre_ref[...] = coef_re * b_re - coef_im * b_im
    bbim_ref[...] = coef_re * b_im + coef_im * b_re


def _s5_tables(lam_re, lam_im, log_step, b_re, b_im, c_re, c_im):
    rep = lambda v: jnp.repeat(v, SSM_GROUP, axis=0)
    rows = SSM_GROUPS * SSM_GROUP
    bt = lambda b: jnp.swapaxes(b, 1, 2).reshape(rows, SSM_STATE)
    ls = jnp.broadcast_to(log_step[:, None], (SSM_GROUPS, SSM_STATE))
    shp = jax.ShapeDtypeStruct((rows, SSM_STATE), F32)
    a_re, a_im, bb_re, bb_im = pl.pallas_call(
        _s5_disc_kernel, out_shape=(shp, shp, shp, shp), name="s5_disc",
    )(rep(lam_re), rep(lam_im), rep(ls), bt(b_re), bt(b_im))
    a_re = a_re[::SSM_GROUP].reshape(N_STRIP, STRIP_STATES)
    a_im = a_im[::SSM_GROUP].reshape(N_STRIP, STRIP_STATES)
    gl = SSM_GROUPS // N_STRIP
    eye = jnp.eye(gl, dtype=F32).reshape(1, gl, 1, gl, 1)

    def in_blockdiag(m):
        m = m.reshape(N_STRIP, gl, SSM_GROUP, 1, SSM_STATE)
        return (m * eye).reshape(N_STRIP, LANE, STRIP_STATES)

    def out_blockdiag(m):
        m = jnp.swapaxes(m.reshape(N_STRIP, gl, SSM_GROUP, SSM_STATE), 2, 3)
        return (m.reshape(N_STRIP, gl, SSM_STATE, 1, SSM_GROUP) * eye).reshape(N_STRIP, STRIP_STATES, LANE)

    bb = jnp.concatenate([in_blockdiag(bb_re), in_blockdiag(bb_im)], axis=2).astype(BF16)
    cc = jnp.concatenate([out_blockdiag(c_re), -out_blockdiag(c_im)], axis=1).astype(BF16)
    return a_re, a_im, bb, cc


def _gelu_glu(y, wglu_ref):
    g = jax.nn.gelu(y)
    return g * jax.nn.sigmoid(_dot(g.astype(BF16), wglu_ref[...]))


def _band_attention(q, kv_scr, sinks, from_prev, prev_bias, seqs):
    gw = Q_PER_KV * HEAD_DIM
    left = lax.broadcasted_iota(jnp.int32, (2 * WINDOW, KV_WIDTH), 1) < HEAD_DIM
    head_of_lane = lax.shift_right_logical(
        lax.broadcasted_iota(jnp.int32, (WINDOW, gw), 1), HEAD_DIM.bit_length() - 1)
    scores, values = [], []
    for n, b in enumerate(seqs):
        kb = kv_scr[b, :, 0:KV_WIDTH]
        vb = kv_scr[b, :, KV_WIDTH:2 * KV_WIDTH]
        kr = pltpu.roll(kb, HEAD_DIM, axis=1)
        vr = pltpu.roll(vb, HEAD_DIM, axis=1)
        for g in range(N_KV_HEADS):
            kk = jnp.where(left, kb, kr) if g == 0 else jnp.where(left, kr, kb)
            vv = jnp.where(left, vb, vr) if g == 0 else jnp.where(left, vr, vb)
            k4 = jnp.concatenate([kk, kk], axis=1).astype(BF16)
            values.append(jnp.concatenate([vv, vv], axis=1).astype(BF16))
            qg = q[n * WINDOW:(n + 1) * WINDOW, g * gw:(g + 1) * gw]
            qst = jnp.concatenate(
                [jnp.where(head_of_lane == r, qg, 0.0) for r in range(Q_PER_KV)], axis=0).astype(BF16)
            scores.append(_dot_nt(qst, k4))
    nblk = len(scores)
    s = jnp.concatenate(scores, axis=0)
    prev = jnp.concatenate([from_prev] * nblk, axis=0)
    s = jnp.where(prev, s[:, :WINDOW] + prev_bias, s[:, WINDOW:])
    sink = jnp.concatenate(
        [jnp.full((WINDOW, WINDOW), sinks[(n % N_KV_HEADS) * Q_PER_KV + r], F32)
         for n in range(nblk) for r in range(Q_PER_KV)], axis=0)
    m = jnp.maximum(jnp.max(s, axis=1, keepdims=True), sink)
    p = jnp.exp(s - m)
    pb = p.astype(BF16)
    hrows = pb.shape[0] // 2
    ri = lax.broadcasted_iota(jnp.int32, (2 * WINDOW, 2 * WINDOW), 0) < WINDOW
    ci = lax.broadcasted_iota(jnp.int32, (2 * WINDOW, 2 * WINDOW), 1) < WINDOW
    ones2 = jnp.where(ri == ci, 1.0, 0.0).astype(BF16)
    sums = _dot(jnp.concatenate([pb[:hrows], pb[hrows:]], axis=1), ones2)
    den = jnp.concatenate([sums[:, :WINDOW], sums[:, WINDOW:]], axis=0) + jnp.exp(sink - m)
    p = p / den
    p = jnp.concatenate([jnp.where(prev, p, 0.0).astype(BF16), jnp.where(prev, 0.0, p).astype(BF16)], axis=1)
    rows = Q_PER_KV * WINDOW
    outs = []
    for b in range(len(seqs)):
        og = []
        for g in range(N_KV_HEADS):
            n = b * N_KV_HEADS + g
            o4 = _dot(p[n * rows:(n + 1) * rows], values[n])
            o = jnp.where(head_of_lane == 0, o4[0:WINDOW], 0.0)
            for r in range(1, Q_PER_KV):
                o = jnp.where(head_of_lane == r, o4[r * WINDOW:(r + 1) * WINDOW], o)
            og.append(o)
        outs.append(jnp.concatenate(og, axis=1))
    return outs


def _proj_attn_kernel(sink_ref, x_ref, g_ref, w_ref, au_ref, kk_ref, vk_ref, kv_scr, *, nb):
    i = pl.program_id(0)
    z = jnp.concatenate(
        [_dot(_rms(x_ref[b:b + 2].reshape(2 * WINDOW, D_MODEL), g_ref[...]).astype(BF16), w_ref[...])
         for b in range(0, nb, 2)], axis=0)
    ucol = ATTN_WIDTH + 2 * KV_WIDTH

    @pl.when(i == 0)
    def _():
        kv_scr[:, 0:WINDOW, :] = jnp.zeros((nb, WINDOW, 2 * KV_WIDTH), F32)

    for b in range(nb):
        kv_scr[b, WINDOW:2 * WINDOW, :] = z[b * WINDOW:(b + 1) * WINDOW, ATTN_WIDTH:ucol]
    q = z[:, :ATTN_WIDTH] * (HEAD_DIM ** -0.5)

    shape = (Q_PER_KV * WINDOW, WINDOW)
    row = jnp.bitwise_and(lax.broadcasted_iota(jnp.int32, shape, 0), WINDOW - 1)
    col = lax.broadcasted_iota(jnp.int32, shape, 1)
    prev_bias = jnp.where(i == 0, -jnp.inf, 0.0).astype(F32)
    sinks = [sink_ref[hh] for hh in range(N_Q_HEADS)]
    hq = nb // 2 * WINDOW
    outs = (_band_attention(q[:hq], kv_scr, sinks, col > row, prev_bias, range(0, nb // 2))
            + _band_attention(q[hq:], kv_scr, sinks, col > row, prev_bias, range(nb // 2, nb)))
    for b in range(nb):
        zb = z[b * WINDOW:(b + 1) * WINDOW]
        for j in range(ATTN_SLABS):
            au_ref[j, pl.ds(b, WINDOW, stride=nb), :] = outs[b][:, j * LANE:(j + 1) * LANE]
        for j in range(N_STRIP):
            au_ref[ATTN_SLABS + j, pl.ds(b, WINDOW, stride=nb), :] = zb[:, ucol + j * LANE:ucol + (j + 1) * LANE]
        kk_ref[b] = zb[:, ATTN_WIDTH:ATTN_WIDTH + KV_WIDTH]
        vk_ref[b] = zb[:, ATTN_WIDTH + KV_WIDTH:ucol]
    kv_scr[:, 0:WINDOW, :] = kv_scr[:, WINDOW:2 * WINDOW, :]


def _prompt_proj_attn(x, g_pre, w_in, sinks):
    nb, seq, _ = x.shape
    nslab = ATTN_SLABS + N_STRIP
    return pl.pallas_call(
        functools.partial(_proj_attn_kernel, nb=nb),
        grid=(seq // WINDOW,),
        in_specs=[
            pl.BlockSpec(memory_space=pltpu.SMEM),
            pl.BlockSpec((nb, WINDOW, D_MODEL), lambda i: (0, i, 0)),
            _const_spec((1, D_MODEL)),
            _const_spec(w_in.shape),
        ],
        out_specs=[
            pl.BlockSpec((nslab, WINDOW * nb, LANE), lambda i: (0, i, 0)),
            pl.BlockSpec((nb, WINDOW, KV_WIDTH), lambda i: (0, 0, 0)),
            pl.BlockSpec((nb, WINDOW, KV_WIDTH), lambda i: (0, 0, 0)),
        ],
        out_shape=[
            jax.ShapeDtypeStruct((nslab, seq * nb, LANE), F32),
            jax.ShapeDtypeStruct((nb, WINDOW, KV_WIDTH), F32),
            jax.ShapeDtypeStruct((nb, WINDOW, KV_WIDTH), F32),
        ],
        scratch_shapes=[pltpu.VMEM((nb, 2 * WINDOW, 2 * KV_WIDTH), F32)],
        compiler_params=_params(32),
        name="prompt_proj_attn",
    )(sinks, x, g_pre, w_in)


def _s5_prompt_kernel(au_ref, bb_ref, cc_ref, are_ref, aim_ref, d_ref, wglu_ref, wout_ref,
                      m_ref, hfin_ref, hbuf, slab_scr, *, steps, nb):
    i = pl.program_id(0)

    @pl.when(i == 0)
    def _():
        hfin_ref[...] = jnp.zeros(hfin_ref.shape, F32)

    ys = []
    for j in range(N_STRIP):
        uj = au_ref[ATTN_SLABS + j]
        hbuf[...] = _dot(uj.astype(BF16), bb_ref[j])
        are = jnp.broadcast_to(are_ref[j:j + 1, :], (nb, STRIP_STATES))
        aim = jnp.broadcast_to(aim_ref[j:j + 1, :], (nb, STRIP_STATES))

        def step(t, carry):
            hr, hi = carry
            r = pl.multiple_of(t * nb, nb)
            br = hbuf[pl.ds(r, nb), 0:STRIP_STATES]
            bi = hbuf[pl.ds(r, nb), STRIP_STATES:2 * STRIP_STATES]
            nr = are * hr - aim * hi + br
            ni = are * hi + aim * hr + bi
            hbuf[pl.ds(r, nb), 0:STRIP_STATES] = nr
            hbuf[pl.ds(r, nb), STRIP_STATES:2 * STRIP_STATES] = ni
            return nr, ni

        h0 = (hfin_ref[j, :, 0:STRIP_STATES], hfin_ref[j, :, STRIP_STATES:2 * STRIP_STATES])
        hr, hi = lax.fori_loop(0, steps, step, h0, unroll=True)
        hfin_ref[j, :, 0:STRIP_STATES] = hr
        hfin_ref[j, :, STRIP_STATES:2 * STRIP_STATES] = hi
        ys.append(_dot(hbuf[...].astype(BF16), cc_ref[j]) + d_ref[:, j * LANE:(j + 1) * LANE] * uj)
    s = _gelu_glu(jnp.concatenate(ys, axis=1), wglu_ref)
    a = jnp.concatenate([au_ref[j] for j in range(ATTN_SLABS)], axis=1).astype(BF16)
    m = _dot(a, wout_ref[0:ATTN_WIDTH, :]) + _dot(s.astype(BF16), wout_ref[ATTN_WIDTH:, :])
    _store_sequence_major(m, slab_scr, m_ref, nb)


def _prompt_s5(au, tabs, d_skip, w_glu, w_out, *, nb, steps):
    a_re, a_im, bb, cc = tabs
    nslab, rows, _ = au.shape
    blk = steps * nb
    return pl.pallas_call(
        functools.partial(_s5_prompt_kernel, steps=steps, nb=nb),
        grid=(rows // blk,),
        in_specs=[
            pl.BlockSpec((nslab, blk, LANE), lambda i: (0, i, 0)),
            _const_spec(bb.shape), _const_spec(cc.shape),
            _const_spec(a_re.shape), _const_spec(a_im.shape),
            _const_spec((1, SSM_WIDTH)),
            _const_spec(w_glu.shape), _const_spec(w_out.shape),
        ],
        out_specs=[
            pl.BlockSpec((nb, steps, D_MODEL), lambda i: (0, i, 0)),
            pl.BlockSpec((N_STRIP, nb, 2 * STRIP_STATES), lambda i: (0, 0, 0)),
        ],
        out_shape=[
            jax.ShapeDtypeStruct((nb, rows // nb, D_MODEL), BF16),
            jax.ShapeDtypeStruct((N_STRIP, nb, 2 * STRIP_STATES), F32),
        ],
        scratch_shapes=[pltpu.VMEM((blk, 2 * STRIP_STATES), F32),
                        pltpu.VMEM((D_SLABS, blk, LANE), F32)],
        compiler_params=_params(16),
        name="prompt_s5",
    )(au, bb, cc, a_re, a_im, d_skip, w_glu, w_out)


def _post_ffn_kernel(x_ref, m_ref, gpm_ref, gpf_ref, gof_ref, wg_ref, wu_ref, wd_ref, o_ref):
    group = x_ref.shape[0] // FFN_SPLIT
    for r in range(FFN_SPLIT):
        rows = slice(r * group, (r + 1) * group)
        x1 = x_ref[rows, :] + _rms(m_ref[rows, :].astype(F32), gpm_ref[...])
        h = _rms(x1, gpf_ref[...]).astype(BF16)
        gate = _dot(h, wg_ref[...])
        up = _dot(h, wu_ref[...])
        act = (gate * jax.nn.sigmoid(gate) * up).astype(BF16)
        f = _dot(act, wd_ref[...])
        o_ref[rows, :] = x1 + _rms(f, gof_ref[...])


def _post_ffn(x, m, gains, ffn_w, layer, *, rows, vmem_mib, name):
    gpm, gpf, gof = gains
    wg, wu, wd = ffn_w
    row_spec = pl.BlockSpec((rows, D_MODEL), lambda i: (i, 0))
    layer_spec = lambda w: pl.BlockSpec(
        (None,) + w.shape[1:], lambda i: (layer, 0, 0), pipeline_mode=pl.Buffered(1))
    return pl.pallas_call(
        _post_ffn_kernel,
        grid=(x.shape[0] // rows,),
        in_specs=[row_spec, row_spec,
                  _const_spec((1, D_MODEL)), _const_spec((1, D_MODEL)), _const_spec((1, D_MODEL)),
                  layer_spec(wg), layer_spec(wu), layer_spec(wd)],
        out_specs=row_spec,
        out_shape=jax.ShapeDtypeStruct(x.shape, F32),
        compiler_params=_params(vmem_mib),
        name=name,
    )(x, m, gpm, gpf, gof, wg, wu, wd)


CONV_HALO = 32
CONV_ROWS = 64


def _layer_norm_silu(c, lng_ref, lnb_ref):
    mu = jnp.mean(c, axis=-1, keepdims=True)
    cc = c - mu
    var = jnp.mean(cc * cc, axis=-1, keepdims=True)
    cn = cc * lax.rsqrt(var + LN_EPS) * lng_ref[...] + lnb_ref[...]
    return cn * jax.nn.sigmoid(cn)


def _conv_prompt_kernel(x_ref, g_ref, w1_ref, b1_ref, wdw_ref, bdw_ref, lng_ref, lnb_ref, w2_ref, b2_ref,
                        m_ref, tail_ref, gp_scr, c_scr, slab_scr, *, blk, nb):
    i = pl.program_id(0)
    halo = CONV_HALO * nb
    tail = CONV_TAIL * nb

    @pl.when(i == 0)
    def _():
        gp_scr[0:halo, :] = jnp.zeros((halo, D_MODEL), F32)

    x = _load_time_major(x_ref, slab_scr, nb)
    h = _rms(x, g_ref[...]).astype(BF16)
    a = _dot(h, w1_ref[...]) + b1_ref[...]
    gp_scr[halo:halo + blk, :] = a[:, :D_MODEL] * jax.nn.sigmoid(a[:, D_MODEL:])

    def chunk(ci, carry):
        base = pl.multiple_of(ci * CONV_ROWS, CONV_ROWS)
        for l in range(D_SLABS):
            lanes = slice(l * LANE, (l + 1) * LANE)
            xs = gp_scr[pl.ds(base + halo - tail, CONV_ROWS + tail), lanes]
            acc = jnp.broadcast_to(bdw_ref[:, lanes], (CONV_ROWS, LANE))
            for j in range(CONV_WIDTH):
                wj = wdw_ref[j * SUBLANE:(j + 1) * SUBLANE, lanes]
                wj = jnp.concatenate([wj] * (CONV_ROWS // SUBLANE), axis=0)
                acc = acc + wj * xs[j * nb:j * nb + CONV_ROWS]
            c_scr[pl.ds(base, CONV_ROWS), lanes] = acc
        return carry

    lax.fori_loop(0, blk // CONV_ROWS, chunk, 0)
    act = _layer_norm_silu(c_scr[...], lng_ref, lnb_ref).astype(BF16)
    _store_sequence_major(_dot(act, w2_ref[...]) + b2_ref[...], slab_scr, m_ref, nb)
    tail_ref[...] = gp_scr[halo + blk - tail:halo + blk, :]
    gp_scr[0:halo, :] = gp_scr[blk:blk + halo, :]


def _prompt_conv(x, g_pre, cw, *, steps):
    w1, b1, wdw8, bdw, lng, lnb, w2, b2 = cw
    nb, seq, _ = x.shape
    blk = steps * nb
    assert steps >= CONV_HALO and blk % CONV_ROWS == 0
    vec = lambda n: _const_spec((1, n))
    seq_spec = pl.BlockSpec((nb, steps, D_MODEL), lambda i: (0, i, 0))
    return pl.pallas_call(
        functools.partial(_conv_prompt_kernel, blk=blk, nb=nb),
        grid=(seq // steps,),
        in_specs=[
            seq_spec,
            vec(D_MODEL), _const_spec(w1.shape), vec(2 * D_MODEL),
            _const_spec(wdw8.shape), vec(D_MODEL), vec(D_MODEL), vec(D_MODEL),
            _const_spec(w2.shape), vec(D_MODEL),
        ],
        out_specs=[
            seq_spec,
            pl.BlockSpec((CONV_TAIL * nb, D_MODEL), lambda i: (0, 0)),
        ],
        out_shape=[
            jax.ShapeDtypeStruct((nb, seq, D_MODEL), BF16),
            jax.ShapeDtypeStruct((CONV_TAIL * nb, D_MODEL), F32),
        ],
        scratch_shapes=[pltpu.VMEM((blk + CONV_HALO * nb, D_MODEL), F32),
                        pltpu.VMEM((blk, D_MODEL), F32),
                        pltpu.VMEM((D_SLABS, blk, LANE), F32)],
        compiler_params=_params(46),
        name="prompt_conv",
    )(x, g_pre, w1, b1, wdw8, bdw, lng, lnb, w2, b2)


def _sample_l0_kernel(sink_ref, x_ref, g_ref, w_ref, ck_ref, cv_ref, sre_ref, sim_ref,
                      bb_ref, cc_ref, are_ref, aim_ref, d_ref, wglu_ref, wout_ref,
                      m_ref, ko_ref, vo_ref, ore_ref, oim_ref,
                      q_scr, k_scr, v_scr, a_scr, u_scr, y_scr, *, nseq, ntok):
    qw = N_Q_HEADS * LANE
    h = _rms(x_ref[...], g_ref[...]).astype(BF16)
    z = _dot(h, w_ref[...])
    q_scr[...] = z[:, :qw] * (HEAD_DIM ** -0.5)
    kn = z[:, qw:qw + KV_WIDTH]
    vn = z[:, qw + KV_WIDTH:qw + 2 * KV_WIDTH]
    k_scr[...] = kn
    v_scr[...] = vn
    for j in range(N_STRIP):
        u_scr[j] = z[:, qw + 2 * KV_WIDTH + j * LANE:qw + 2 * KV_WIDTH + (j + 1) * LANE]
    y_scr[...] = jnp.zeros(y_scr.shape, F32)

    for j in range(N_STRIP):
        hr = sre_ref[:, j * STRIP_STATES:(j + 1) * STRIP_STATES]
        hi = sim_ref[:, j * STRIP_STATES:(j + 1) * STRIP_STATES]
        are = are_ref[j:j + 1, :]
        aim = aim_ref[j:j + 1, :]
        dj = d_ref[:, j * LANE:(j + 1) * LANE]
        for t in range(ntok):
            ut = u_scr[j, pl.ds(t, nseq, stride=SAMPLE_T), :]
            bu = _dot(ut.astype(BF16), bb_ref[j])
            hr, hi = (are * hr - aim * hi + bu[:, :STRIP_STATES],
                      are * hi + aim * hr + bu[:, STRIP_STATES:])
            hcat = jnp.concatenate([hr, hi], axis=1).astype(BF16)
            y_scr[j, pl.ds(t, nseq, stride=SAMPLE_T), :] = _dot(hcat, cc_ref[j]) + dj * ut
        ore_ref[:, j * STRIP_STATES:(j + 1) * STRIP_STATES] = hr
        oim_ref[:, j * STRIP_STATES:(j + 1) * STRIP_STATES] = hi
    s = _gelu_glu(jnp.concatenate([y_scr[j] for j in range(N_STRIP)], axis=1), wglu_ref)

    nkeys = WINDOW + SAMPLE_T
    shape = (N_Q_HEADS * SAMPLE_T, nkeys)
    tq = jnp.bitwise_and(lax.broadcasted_iota(jnp.int32, shape, 0), SAMPLE_T - 1)
    col = lax.broadcasted_iota(jnp.int32, shape, 1)
    ninf = jnp.float32(-jnp.inf)
    bias = jnp.where(col < WINDOW, jnp.where(col > tq, 0.0, ninf),
                     jnp.where(col - WINDOW <= tq, 0.0, ninf))
    sink = jnp.concatenate(
        [jnp.full((SAMPLE_T, 1), sink_ref[hh], F32) for hh in range(N_Q_HEADS)], axis=0)
    left = lax.broadcasted_iota(jnp.int32, (SAMPLE_T, LANE), 1) < HEAD_DIM

    def seq_body(n, carry):
        r = pl.multiple_of(n * SAMPLE_T, SAMPLE_T)
        qs = jnp.concatenate(
            [q_scr[pl.ds(r, SAMPLE_T), hh * LANE:(hh + 1) * LANE] for hh in range(N_Q_HEADS)], axis=0)
        kall = jnp.concatenate([ck_ref[n], k_scr[pl.ds(r, SAMPLE_T), :]], axis=0)
        vall = jnp.concatenate([cv_ref[n], v_scr[pl.ds(r, SAMPLE_T), :]], axis=0)
        ko_ref[n] = kall[ntok:ntok + WINDOW]
        vo_ref[n] = vall[ntok:ntok + WINDOW]
        sc = _dot_nt(qs.astype(BF16), kall.astype(BF16)) + bias
        mx = jnp.maximum(jnp.max(sc, axis=1, keepdims=True), sink)
        p = jnp.exp(sc - mx)
        den = jnp.sum(p, axis=1, keepdims=True) + jnp.exp(sink - mx)
        p = (p / den).astype(BF16)
        o1 = _dot(p, vall.astype(BF16))
        o2 = _dot(p, pltpu.roll(vall, HEAD_DIM, axis=1).astype(BF16))
        slabs = []
        for hp in range(N_Q_HEADS // 2):
            lo, ro = (o1, o2) if hp < Q_PER_KV // 2 else (o2, o1)
            lrow = 2 * hp * SAMPLE_T
            slabs.append(jnp.where(left, lo[lrow:lrow + SAMPLE_T], ro[lrow + SAMPLE_T:lrow + 2 * SAMPLE_T]))
        a_scr[pl.ds(r, SAMPLE_T), :] = jnp.concatenate(slabs, axis=1)
        return carry

    lax.fori_loop(0, nseq, seq_body, 0, unroll=SAMPLE_ATTN_UNROLL)
    m = (_dot(a_scr[...].astype(BF16), wout_ref[0:ATTN_WIDTH, :])
         + _dot(s.astype(BF16), wout_ref[ATTN_WIDTH:, :]))
    m_ref[...] = m.astype(m_ref.dtype)


def _sample_l0(x, g_pre, w_in_s, sinks, ck, cv, s_re, s_im, tabs, d_skip, w_glu, w_out, *, nseq, ntok):
    a_re, a_im, bb, cc = tabs
    rows_all = x.shape[0]
    rows = nseq * SAMPLE_T
    nstate = SSM_GROUPS * SSM_STATE
    row_spec = lambda w: pl.BlockSpec((rows, w), lambda i: (i, 0))
    return pl.pallas_call(
        functools.partial(_sample_l0_kernel, nseq=nseq, ntok=ntok),
        grid=(rows_all // rows,),
        in_specs=[
            pl.BlockSpec(memory_space=pltpu.SMEM),
            row_spec(D_MODEL), _const_spec((1, D_MODEL)), _const_spec(w_in_s.shape),
            pl.BlockSpec((nseq, WINDOW, KV_WIDTH), lambda i: (i, 0, 0)),
            pl.BlockSpec((nseq, WINDOW, KV_WIDTH), lambda i: (i, 0, 0)),
            pl.BlockSpec((nseq, nstate), lambda i: (i, 0)),
            pl.BlockSpec((nseq, nstate), lambda i: (i, 0)),
            _const_spec(bb.shape), _const_spec(cc.shape),
            _const_spec(a_re.shape), _const_spec(a_im.shape),
            _const_spec((1, SSM_WIDTH)),
            _const_spec(w_glu.shape), _const_spec(w_out.shape),
        ],
        out_specs=[
            row_spec(D_MODEL),
            pl.BlockSpec((nseq, WINDOW, KV_WIDTH), lambda i: (i, 0, 0)),
            pl.BlockSpec((nseq, WINDOW, KV_WIDTH), lambda i: (i, 0, 0)),
            pl.BlockSpec((nseq, nstate), lambda i: (i, 0)),
            pl.BlockSpec((nseq, nstate), lambda i: (i, 0)),
        ],
        out_shape=[
            jax.ShapeDtypeStruct((rows_all, D_MODEL), BF16),
            jax.ShapeDtypeStruct(ck.shape, F32),
            jax.ShapeDtypeStruct(cv.shape, F32),
            jax.ShapeDtypeStruct((rows_all // SAMPLE_T, nstate), F32),
            jax.ShapeDtypeStruct((rows_all // SAMPLE_T, nstate), F32),
        ],
        scratch_shapes=[
            pltpu.VMEM((rows, N_Q_HEADS * LANE), F32),
            pltpu.VMEM((rows, KV_WIDTH), F32),
            pltpu.VMEM((rows, KV_WIDTH), F32),
            pltpu.VMEM((rows, ATTN_WIDTH), F32),
            pltpu.VMEM((N_STRIP, rows, LANE), F32),
            pltpu.VMEM((N_STRIP, rows, LANE), F32),
        ],
        compiler_params=_params(34),
        name="sample_l0",
    )(sinks, x, g_pre, w_in_s, ck, cv, s_re, s_im, bb, cc, a_re, a_im, d_skip, w_glu, w_out)


def _sample_l1_kernel(x_ref, g_ref, w1_ref, b1_ref, st_ref, wdw_ref, bdw_ref, lng_ref, lnb_ref,
                      w2_ref, b2_ref, m_ref, sto_ref, g_scr, c_scr, *, nseq, ntok):
    h = _rms(x_ref[...], g_ref[...]).astype(BF16)
    a = _dot(h, w1_ref[...]) + b1_ref[...]
    gg = a[:, :D_MODEL] * jax.nn.sigmoid(a[:, D_MODEL:])
    for j in range(D_SLABS):
        g_scr[j] = gg[:, j * LANE:(j + 1) * LANE]
    c_scr[...] = jnp.zeros(c_scr.shape, F32)
    keep = CONV_TAIL - ntok
    for j in range(D_SLABS):
        lanes = slice(j * LANE, (j + 1) * LANE)
        new = [g_scr[j, pl.ds(t, nseq, stride=SAMPLE_T), :] for t in range(ntok)]
        acc = [jnp.broadcast_to(bdw_ref[:, lanes], (nseq, LANE)) for _ in range(ntok)]
        for r in range(CONV_TAIL):
            row = st_ref[r, :, lanes]
            if r >= ntok:
                sto_ref[r - ntok, :, lanes] = row
            for t in range(min(r, ntok - 1) + 1):
                acc[t] = acc[t] + row * wdw_ref[r - t:r - t + 1, lanes]
        for t in range(ntok):
            sto_ref[keep + t, :, lanes] = new[t]
            for t2 in range(t + 1):
                tap = CONV_TAIL + t2 - t
                acc[t] = acc[t] + new[t2] * wdw_ref[tap:tap + 1, lanes]
            c_scr[j, pl.ds(t, nseq, stride=SAMPLE_T), :] = acc[t]
    c = jnp.concatenate([c_scr[j] for j in range(D_SLABS)], axis=1)
    act = _layer_norm_silu(c, lng_ref, lnb_ref).astype(BF16)
    m_ref[...] = (_dot(act, w2_ref[...]) + b2_ref[...]).astype(m_ref.dtype)


def _sample_l1(x, g_pre, cw, st, wdw, *, nseq, ntok):
    w1, b1, _, bdw, lng, lnb, w2, b2 = cw
    rows_all = x.shape[0]
    rows = nseq * SAMPLE_T
    vec = lambda n: _const_spec((1, n))
    row_spec = pl.BlockSpec((rows, D_MODEL), lambda i: (i, 0))
    st_spec = pl.BlockSpec((CONV_TAIL, nseq, D_MODEL), lambda i: (0, i, 0))
    return pl.pallas_call(
        functools.partial(_sample_l1_kernel, nseq=nseq, ntok=ntok),
        grid=(rows_all // rows,),
        in_specs=[
            row_spec, vec(D_MODEL), _const_spec(w1.shape), vec(2 * D_MODEL),
            st_spec, _const_spec(wdw.shape),
            vec(D_MODEL), vec(D_MODEL), vec(D_MODEL), _const_spec(w2.shape), vec(D_MODEL),
        ],
        out_specs=[row_spec, st_spec],
        out_shape=[
            jax.ShapeDtypeStruct((rows_all, D_MODEL), BF16),
            jax.ShapeDtypeStruct(st.shape, F32),
        ],
        scratch_shapes=[pltpu.VMEM((D_SLABS, rows, LANE), F32),
                        pltpu.VMEM((D_SLABS, rows, LANE), F32)],
        compiler_params=_params(26),
        name="sample_l1",
    )(x, g_pre, w1, b1, st, wdw, bdw, lng, lnb, w2, b2)


PROMPT_FFN_ROWS = 1024
SCAN_STEPS = 64
CONV_STEPS = 128
SAMPLE_SEQS = 32
SAMPLE_FFN_ROWS = 512
SAMPLE_ATTN_UNROLL = 8


def kernel(x_prompt, x_sample, cache_k, cache_v, state_s5_re, state_s5_im, state_conv, norm_pre_mix, norm_post_mix, norm_pre_ffn, norm_post_ffn, w_in_ab, attn_sinks, s5_lambda_re, s5_lambda_im, s5_log_step, s5_b_re, s5_b_im, s5_c_re, s5_c_im, s5_d, w_glu, w_out_ab, w_pw1, b_pw1, w_dw, b_dw, conv_ln_g, conv_ln_b, w_pw2, b_pw2, w_ffn_gate, w_ffn_up, w_ffn_down):
    nb, seq, _ = x_prompt.shape
    ns, ntok, _ = x_sample.shape
    assert nb == SUBLANE and ntok <= SAMPLE_T and seq % WINDOW == 0 and seq % SCAN_STEPS == 0 and seq % CONV_STEPS == 0
    row = lambda v: v.reshape(1, -1)
    gains = lambda layer: tuple(row(n[layer]) for n in (norm_post_mix, norm_pre_ffn, norm_post_ffn))

    w_in = w_in_ab[0].astype(BF16)
    wq = w_in_ab[0][:, :ATTN_WIDTH].reshape(D_MODEL, N_Q_HEADS, HEAD_DIM)
    lane_kv = (jnp.arange(LANE) // HEAD_DIM)[None, :]
    head_kv = (jnp.arange(N_Q_HEADS) // Q_PER_KV)[:, None]
    wq_slab = jnp.where((lane_kv == head_kv)[None], jnp.concatenate([wq, wq], axis=2), 0.0)
    w_in_s = jnp.concatenate(
        [wq_slab.reshape(D_MODEL, N_Q_HEADS * LANE), w_in_ab[0][:, ATTN_WIDTH:]], axis=1).astype(BF16)
    w_glu_b = w_glu[0].astype(BF16)
    w_out = w_out_ab[0].astype(BF16)
    ffn_w = (w_ffn_gate.astype(BF16), w_ffn_up.astype(BF16), w_ffn_down.astype(BF16))
    sinks = attn_sinks[0]
    tabs = _s5_tables(s5_lambda_re[0], s5_lambda_im[0], s5_log_step[0], s5_b_re[0], s5_b_im[0],
                      s5_c_re[0], s5_c_im[0])
    d_skip = row(s5_d)
    wdw = w_dw[0]
    conv_w = (w_pw1[0].astype(BF16), row(b_pw1), jnp.repeat(wdw, SUBLANE, axis=0), row(b_dw),
              row(conv_ln_g), row(conv_ln_b), w_pw2[0].astype(BF16), row(b_pw2))

    flat = lambda v: v.reshape(nb * seq, D_MODEL)
    au_p, kk_p, vk_p = _prompt_proj_attn(x_prompt, row(norm_pre_mix[0]), w_in, sinks)
    m0_p, hfin_p = _prompt_s5(au_p, tabs, d_skip, w_glu_b, w_out, nb=nb, steps=SCAN_STEPS)
    x1_p = _post_ffn(flat(x_prompt), flat(m0_p), gains(0), ffn_w, 0, rows=PROMPT_FFN_ROWS, vmem_mib=52, name="prompt_ffn0")
    m1_p, tail_p = _prompt_conv(x1_p.reshape(nb, seq, D_MODEL), row(norm_pre_mix[1]), conv_w, steps=CONV_STEPS)
    y_prompt = _post_ffn(x1_p, flat(m1_p), gains(1), ffn_w, 1, rows=PROMPT_FFN_ROWS, vmem_mib=52, name="prompt_ffn1")
    y_prompt = y_prompt.reshape(nb, seq, D_MODEL)

    pad_rows = lambda v: jnp.pad(
        v.reshape(ns, ntok, D_MODEL), ((0, 0), (0, SAMPLE_T - ntok), (0, 0))).reshape(ns * SAMPLE_T, D_MODEL)
    real_rows = lambda v: v.reshape(ns, SAMPLE_T, D_MODEL)[:, :ntok].reshape(ns * ntok, D_MODEL)
    xs_c = x_sample.reshape(ns * ntok, D_MODEL)
    xs = pad_rows(xs_c)
    nstate = SSM_GROUPS * SSM_STATE
    m0_s, ko_s, vo_s, sre_s, sim_s = _sample_l0(
        xs, row(norm_pre_mix[0]), w_in_s, sinks,
        cache_k.reshape(ns, WINDOW, KV_WIDTH), cache_v.reshape(ns, WINDOW, KV_WIDTH),
        state_s5_re.reshape(ns, nstate), state_s5_im.reshape(ns, nstate),
        tabs, d_skip, w_glu_b, w_out, nseq=SAMPLE_SEQS, ntok=ntok)
    x1_c = _post_ffn(xs_c, real_rows(m0_s), gains(0), ffn_w, 0, rows=SAMPLE_FFN_ROWS, vmem_mib=22, name="sample_ffn0")
    st_t = jnp.swapaxes(state_conv.reshape(ns, CONV_TAIL, D_MODEL), 0, 1)
    m1_s, sto_t = _sample_l1(
        pad_rows(x1_c), row(norm_pre_mix[1]), conv_w, st_t, wdw, nseq=SAMPLE_SEQS, ntok=ntok)
    y_c = _post_ffn(x1_c, real_rows(m1_s), gains(1), ffn_w, 1, rows=SAMPLE_FFN_ROWS, vmem_mib=22, name="sample_ffn1")

    y_sample = y_c.reshape(ns, ntok, D_MODEL)
    k_prompt = kk_p.reshape(1, nb, WINDOW, N_KV_HEADS, HEAD_DIM)
    v_prompt = vk_p.reshape(1, nb, WINDOW, N_KV_HEADS, HEAD_DIM)
    hfin = jnp.swapaxes(hfin_p, 0, 1)
    s5_re_prompt = hfin[:, :, :STRIP_STATES].reshape(1, nb, SSM_GROUPS, SSM_STATE)
    s5_im_prompt = hfin[:, :, STRIP_STATES:].reshape(1, nb, SSM_GROUPS, SSM_STATE)
    conv_prompt = jnp.swapaxes(tail_p.reshape(CONV_TAIL, nb, D_MODEL), 0, 1)[None]
    k_sample = ko_s.reshape(1, ns, WINDOW, N_KV_HEADS, HEAD_DIM)
    v_sample = vo_s.reshape(1, ns, WINDOW, N_KV_HEADS, HEAD_DIM)
    s5_re_sample = sre_s.reshape(1, ns, SSM_GROUPS, SSM_STATE)
    s5_im_sample = sim_s.reshape(1, ns, SSM_GROUPS, SSM_STATE)
    conv_sample = jnp.swapaxes(sto_t, 0, 1).reshape(1, ns, CONV_TAIL, D_MODEL)
    return (y_prompt, y_sample, k_prompt, v_prompt, s5_re_prompt, s5_im_prompt, conv_prompt,
            k_sample, v_sample, s5_re_sample, s5_im_sample, conv_sample)
```

```python
import functools

import jax
import jax.numpy as jnp
from jax import lax
from jax.experimental import pallas as pl
from jax.experimental.pallas import tpu as pltpu

F32 = jnp.float32
BF16 = jnp.bfloat16

LANE = 128
SUBLANE = 8
VMEM_LIMIT_BYTES = 56 * 1024 * 1024

D_MODEL = 1024
HEAD_DIM = 64
N_Q_HEADS = 8
N_KV_HEADS = 2
Q_PER_KV = N_Q_HEADS // N_KV_HEADS
WINDOW = 128
ATTN_WIDTH = N_Q_HEADS * HEAD_DIM
KV_WIDTH = N_KV_HEADS * HEAD_DIM
SSM_WIDTH = 512
SSM_GROUP = 16
SSM_GROUPS = 32
SSM_STATE = 64
D_FF = 2816
CONV_WIDTH = 31
CONV_TAIL = CONV_WIDTH - 1
RMS_EPS = 1e-6
LN_EPS = 1e-5

N_STRIP = SSM_WIDTH // LANE
STRIP_STATES = (SSM_GROUPS // N_STRIP) * SSM_STATE
D_SLABS = D_MODEL // LANE
ATTN_SLABS = ATTN_WIDTH // LANE
FFN_SPLIT = 4

SAMPLE_T = SUBLANE


def _rms(x, g):
    return x * lax.rsqrt(jnp.mean(x * x, axis=-1, keepdims=True) + RMS_EPS) * g


def _dot(a, b):
    return jnp.dot(a, b, preferred_element_type=F32)


def _dot_nt(a, b):
    return lax.dot_general(a, b, (((1,), (1,)), ((), ())), preferred_element_type=F32)


def _const_spec(shape):
    nd = len(shape)
    return pl.BlockSpec(shape, lambda *_: (0,) * nd, pipeline_mode=pl.Buffered(1))


def _params(n_axes):
    return pltpu.CompilerParams(
        dimension_semantics=("arbitrary",) * n_axes, vmem_limit_bytes=VMEM_LIMIT_BYTES)


def _load_time_major(x_ref, slab_scr, nb):
    steps = x_ref.shape[1]
    nslab = x_ref.shape[2] // LANE
    for b in range(nb):
        for j in range(nslab):
            slab_scr[j, pl.ds(b, steps, stride=nb), :] = x_ref[b, :, j * LANE:(j + 1) * LANE]
    return jnp.concatenate([slab_scr[j] for j in range(nslab)], axis=1)


def _store_sequence_major(val, slab_scr, out_ref, nb):
    steps = out_ref.shape[1]
    nslab = out_ref.shape[2] // LANE
    for j in range(nslab):
        slab_scr[j] = val[:, j * LANE:(j + 1) * LANE]
    for b in range(nb):
        for j in range(nslab):
            out_ref[b, :, j * LANE:(j + 1) * LANE] = (
                slab_scr[j, pl.ds(b, steps, stride=nb), :].astype(out_ref.dtype))


def _s5_disc_kernel(lre_ref, lim_ref, ls_ref, bre_ref, bim_ref, are_ref, aim_ref, bbre_ref, bbim_ref):
    lam_re = lre_ref[...]
    lam_im = lim_ref[...]
    dt = jnp.exp(ls_ref[...])
    mag = jnp.exp(lam_re * dt)
    lb_re = mag * jnp.cos(lam_im * dt)
    lb_im = mag * jnp.sin(lam_im * dt)
    den = lam_re * lam_re + lam_im * lam_im
    coef_re = ((lb_re - 1.0) * lam_re + lb_im * lam_im) / den
    coef_im = (lb_im * lam_re - (lb_re - 1.0) * lam_im) / den
    b_re = bre_ref[...]
    b_im = bim_ref[...]
    are_ref[...] = lb_re
    aim_ref[...] = lb_im
    bbre_ref[...] = coef_re * b_re - coef_im * b_im
    bbim_ref[...] = coef_re * b_im + coef_im * b_re


def _s5_tables(lam_re, lam_im, log_step, b_re, b_im, c_re, c_im):
    rep = lambda v: jnp.repeat(v, SSM_GROUP, axis=0)
    rows = SSM_GROUPS * SSM_GROUP
    bt = lambda b: jnp.swapaxes(b, 1, 2).reshape(rows, SSM_STATE)
    ls = jnp.broadcast_to(log_step[:, None], (SSM_GROUPS, SSM_STATE))
    shp = jax.ShapeDtypeStruct((rows, SSM_STATE), F32)
    a_re, a_im, bb_re, bb_im = pl.pallas_call(
        _s5_disc_kernel, out_shape=(shp, shp, shp, shp), name="s5_disc",
    )(rep(lam_re), rep(lam_im), rep(ls), bt(b_re), bt(b_im))
    a_re = a_re[::SSM_GROUP].reshape(N_STRIP, STRIP_STATES)
    a_im = a_im[::SSM_GROUP].reshape(N_STRIP, STRIP_STATES)
    gl = SSM_GROUPS // N_STRIP
    eye = jnp.eye(gl, dtype=F32).reshape(1, gl, 1, gl, 1)

    def in_blockdiag(m):
        m = m.reshape(N_STRIP, gl, SSM_GROUP, 1, SSM_STATE)
        return (m * eye).reshape(N_STRIP, LANE, STRIP_STATES)

    def out_blockdiag(m):
        m = jnp.swapaxes(m.reshape(N_STRIP, gl, SSM_GROUP, SSM_STATE), 2, 3)
        return (m.reshape(N_STRIP, gl, SSM_STATE, 1, SSM_GROUP) * eye).reshape(N_STRIP, STRIP_STATES, LANE)

    bb = jnp.concatenate([in_blockdiag(bb_re), in_blockdiag(bb_im)], axis=2).astype(BF16)
    cc = jnp.concatenate([out_blockdiag(c_re), -out_blockdiag(c_im)], axis=1).astype(BF16)
    return a_re, a_im, bb, cc


def _gelu_glu(y, wglu_ref):
    g = jax.nn.gelu(y)
    return g * jax.nn.sigmoid(_dot(g.astype(BF16), wglu_ref[...]))


def _band_attention(q, kv_scr, sinks, from_prev, prev_bias, seqs):
    gw = Q_PER_KV * HEAD_DIM
    left = lax.broadcasted_iota(jnp.int32, (2 * WINDOW, KV_WIDTH), 1) < HEAD_DIM
    head_of_lane = lax.shift_right_logical(
        lax.broadcasted_iota(jnp.int32, (WINDOW, gw), 1), HEAD_DIM.bit_length() - 1)
    scores, values = [], []
    for n, b in enumerate(seqs):
        kb = kv_scr[b, :, 0:KV_WIDTH]
        vb = kv_scr[b, :, KV_WIDTH:2 * KV_WIDTH]
        kr = pltpu.roll(kb, HEAD_DIM, axis=1)
        vr = pltpu.roll(vb, HEAD_DIM, axis=1)
        for g in range(N_KV_HEADS):
            kk = jnp.where(left, kb, kr) if g == 0 else jnp.where(left, kr, kb)
            vv = jnp.where(left, vb, vr) if g == 0 else jnp.where(left, vr, vb)
            k4 = jnp.concatenate([kk, kk], axis=1).astype(BF16)
            values.append(jnp.concatenate([vv, vv], axis=1).astype(BF16))
            qg = q[n * WINDOW:(n + 1) * WINDOW, g * gw:(g + 1) * gw]
            qst = jnp.concatenate(
                [jnp.where(head_of_lane == r, qg, 0.0) for r in range(Q_PER_KV)], axis=0).astype(BF16)
            scores.append(_dot_nt(qst, k4))
    nblk = len(scores)
    s = jnp.concatenate(scores, axis=0)
    prev = jnp.concatenate([from_prev] * nblk, axis=0)
    s = jnp.where(prev, s[:, :WINDOW] + prev_bias, s[:, WINDOW:])
    sink = jnp.concatenate(
        [jnp.full((WINDOW, WINDOW), sinks[(n % N_KV_HEADS) * Q_PER_KV + r], F32)
         for n in range(nblk) for r in range(Q_PER_KV)], axis=0)
    m = jnp.maximum(jnp.max(s, axis=1, keepdims=True), sink)
    p = jnp.exp(s - m)
    pb = p.astype(BF16)
    hrows = pb.shape[0] // 2
    ri = lax.broadcasted_iota(jnp.int32, (2 * WINDOW, 2 * WINDOW), 0) < WINDOW
    ci = lax.broadcasted_iota(jnp.int32, (2 * WINDOW, 2 * WINDOW), 1) < WINDOW
    ones2 = jnp.where(ri == ci, 1.0, 0.0).astype(BF16)
    sums = _dot(jnp.concatenate([pb[:hrows], pb[hrows:]], axis=1), ones2)
    den = jnp.concatenate([sums[:, :WINDOW], sums[:, WINDOW:]], axis=0) + jnp.exp(sink - m)
    p = p / den
    p = jnp.concatenate([jnp.where(prev, p, 0.0).astype(BF16), jnp.where(prev, 0.0, p).astype(BF16)], axis=1)
    rows = Q_PER_KV * WINDOW
    outs = []
    for b in range(len(seqs)):
        og = []
        for g in range(N_KV_HEADS):
            n = b * N_KV_HEADS + g
            o4 = _dot(p[n * rows:(n + 1) * rows], values[n])
            o = jnp.where(head_of_lane == 0, o4[0:WINDOW], 0.0)
            for r in range(1, Q_PER_KV):
                o = jnp.where(head_of_lane == r, o4[r * WINDOW:(r + 1) * WINDOW], o)
            og.append(o)
        outs.append(jnp.concatenate(og, axis=1))
    return outs


def _proj_attn_kernel(sink_ref, x_ref, g_ref, w_ref, au_ref, kk_ref, vk_ref, kv_scr, *, nb):
    i = pl.program_id(0)
    z = jnp.concatenate(
        [_dot(_rms(x_ref[b:b + 2].reshape(2 * WINDOW, D_MODEL), g_ref[...]).astype(BF16), w_ref[...])
         for b in range(0, nb, 2)], axis=0)
    ucol = ATTN_WIDTH + 2 * KV_WIDTH

    @pl.when(i == 0)
    def _():
        kv_scr[:, 0:WINDOW, :] = jnp.zeros((nb, WINDOW, 2 * KV_WIDTH), F32)

    for b in range(nb):
        kv_scr[b, WINDOW:2 * WINDOW, :] = z[b * WINDOW:(b + 1) * WINDOW, ATTN_WIDTH:ucol]
    q = z[:, :ATTN_WIDTH] * (HEAD_DIM ** -0.5)

    shape = (Q_PER_KV * WINDOW, WINDOW)
    row = jnp.bitwise_and(lax.broadcasted_iota(jnp.int32, shape, 0), WINDOW - 1)
    col = lax.broadcasted_iota(jnp.int32, shape, 1)
    prev_bias = jnp.where(i == 0, -jnp.inf, 0.0).astype(F32)
    sinks = [sink_ref[hh] for hh in range(N_Q_HEADS)]
    hq = nb // 2 * WINDOW
    outs = (_band_attention(q[:hq], kv_scr, sinks, col > row, prev_bias, range(0, nb // 2))
            + _band_attention(q[hq:], kv_scr, sinks, col > row, prev_bias, range(nb // 2, nb)))
    for b in range(nb):
        zb = z[b * WINDOW:(b + 1) * WINDOW]
        for j in range(ATTN_SLABS):
            au_ref[j, pl.ds(b, WINDOW, stride=nb), :] = outs[b][:, j * LANE:(j + 1) * LANE]
        for j in range(N_STRIP):
            au_ref[ATTN_SLABS + j, pl.ds(b, WINDOW, stride=nb), :] = zb[:, ucol + j * LANE:ucol + (j + 1) * LANE]
        kk_ref[b] = zb[:, ATTN_WIDTH:ATTN_WIDTH + KV_WIDTH]
        vk_ref[b] = zb[:, ATTN_WIDTH + KV_WIDTH:ucol]
    kv_scr[:, 0:WINDOW, :] = kv_scr[:, WINDOW:2 * WINDOW, :]


def _prompt_proj_attn(x, g_pre, w_in, sinks):
    nb, seq, _ = x.shape
    nslab = ATTN_SLABS + N_STRIP
    return pl.pallas_call(
        functools.partial(_proj_attn_kernel, nb=nb),
        grid=(seq // WINDOW,),
        in_specs=[
            pl.BlockSpec(memory_space=pltpu.SMEM),
            pl.BlockSpec((nb, WINDOW, D_MODEL), lambda i: (0, i, 0)),
            _const_spec((1, D_MODEL)),
            _const_spec(w_in.shape),
        ],
        out_specs=[
            pl.BlockSpec((nslab, WINDOW * nb, LANE), lambda i: (0, i, 0)),
            pl.BlockSpec((nb, WINDOW, KV_WIDTH), lambda i: (0, 0, 0)),
            pl.BlockSpec((nb, WINDOW, KV_WIDTH), lambda i: (0, 0, 0)),
        ],
        out_shape=[
            jax.ShapeDtypeStruct((nslab, seq * nb, LANE), F32),
            jax.ShapeDtypeStruct((nb, WINDOW, KV_WIDTH), F32),
            jax.ShapeDtypeStruct((nb, WINDOW, KV_WIDTH), F32),
        ],
        scratch_shapes=[pltpu.VMEM((nb, 2 * WINDOW, 2 * KV_WIDTH), F32)],
        compiler_params=_params(1),
        name="prompt_proj_attn",
    )(sinks, x, g_pre, w_in)


def _s5_prompt_kernel(au_ref, bb_ref, cc_ref, are_ref, aim_ref, d_ref, wglu_ref, wout_ref,
                      m_ref, hfin_ref, hbuf, slab_scr, *, steps, nb):
    i = pl.program_id(0)

    @pl.when(i == 0)
    def _():
        hfin_ref[...] = jnp.zeros(hfin_ref.shape, F32)

    ys = []
    for j in range(N_STRIP):
        uj = au_ref[ATTN_SLABS + j]
        hbuf[...] = _dot(uj.astype(BF16), bb_ref[j])
        are = jnp.broadcast_to(are_ref[j:j + 1, :], (nb, STRIP_STATES))
        aim = jnp.broadcast_to(aim_ref[j:j + 1, :], (nb, STRIP_STATES))

        def step(t, carry):
            hr, hi = carry
            r = pl.multiple_of(t * nb, nb)
            br = hbuf[pl.ds(r, nb), 0:STRIP_STATES]
            bi = hbuf[pl.ds(r, nb), STRIP_STATES:2 * STRIP_STATES]
            nr = are * hr - aim * hi + br
            ni = are * hi + aim * hr + bi
            hbuf[pl.ds(r, nb), 0:STRIP_STATES] = nr
            hbuf[pl.ds(r, nb), STRIP_STATES:2 * STRIP_STATES] = ni
            return nr, ni

        h0 = (hfin_ref[j, :, 0:STRIP_STATES], hfin_ref[j, :, STRIP_STATES:2 * STRIP_STATES])
        hr, hi = lax.fori_loop(0, steps, step, h0, unroll=True)
        hfin_ref[j, :, 0:STRIP_STATES] = hr
        hfin_ref[j, :, STRIP_STATES:2 * STRIP_STATES] = hi
        ys.append(_dot(hbuf[...].astype(BF16), cc_ref[j]) + d_ref[:, j * LANE:(j + 1) * LANE] * uj)
    s = _gelu_glu(jnp.concatenate(ys, axis=1), wglu_ref)
    a = jnp.concatenate([au_ref[j] for j in range(ATTN_SLABS)], axis=1).astype(BF16)
    m = _dot(a, wout_ref[0:ATTN_WIDTH, :]) + _dot(s.astype(BF16), wout_ref[ATTN_WIDTH:, :])
    _store_sequence_major(m, slab_scr, m_ref, nb)


def _prompt_s5(au, tabs, d_skip, w_glu, w_out, *, nb, steps):
    a_re, a_im, bb, cc = tabs
    nslab, rows, _ = au.shape
    blk = steps * nb
    return pl.pallas_call(
        functools.partial(_s5_prompt_kernel, steps=steps, nb=nb),
        grid=(rows // blk,),
        in_specs=[
            pl.BlockSpec((nslab, blk, LANE), lambda i: (0, i, 0)),
            _const_spec(bb.shape), _const_spec(cc.shape),
            _const_spec(a_re.shape), _const_spec(a_im.shape),
            _const_spec((1, SSM_WIDTH)),
            _const_spec(w_glu.shape), _const_spec(w_out.shape),
        ],
        out_specs=[
            pl.BlockSpec((nb, steps, D_MODEL), lambda i: (0, i, 0)),
            pl.BlockSpec((N_STRIP, nb, 2 * STRIP_STATES), lambda i: (0, 0, 0)),
        ],
        out_shape=[
            jax.ShapeDtypeStruct((nb, rows // nb, D_MODEL), BF16),
            jax.ShapeDtypeStruct((N_STRIP, nb, 2 * STRIP_STATES), F32),
        ],
        scratch_shapes=[pltpu.VMEM((blk, 2 * STRIP_STATES), F32),
                        pltpu.VMEM((D_SLABS, blk, LANE), F32)],
        compiler_params=_params(1),
        name="prompt_s5",
    )(au, bb, cc, a_re, a_im, d_skip, w_glu, w_out)


def _post_ffn_kernel(x_ref, m_ref, gpm_ref, gpf_ref, gof_ref, wg_ref, wu_ref, wd_ref, o_ref):
    group = x_ref.shape[0] // FFN_SPLIT
    for r in range(FFN_SPLIT):
        rows = slice(r * group, (r + 1) * group)
        x1 = x_ref[rows, :] + _rms(m_ref[rows, :].astype(F32), gpm_ref[...])
        h = _rms(x1, gpf_ref[...]).astype(BF16)
        gate = _dot(h, wg_ref[...])
        up = _dot(h, wu_ref[...])
        act = (gate * jax.nn.sigmoid(gate) * up).astype(BF16)
        f = _dot(act, wd_ref[...])
        o_ref[rows, :] = x1 + _rms(f, gof_ref[...])


def _post_ffn(x, m, gains, ffn_w, layer, *, rows, name, vmem_bytes=VMEM_LIMIT_BYTES):
    gpm, gpf, gof = gains
    wg, wu, wd = ffn_w
    row_spec = pl.BlockSpec((rows, D_MODEL), lambda i: (i, 0))
    layer_spec = lambda w: pl.BlockSpec(
        (None,) + w.shape[1:], lambda i: (layer, 0, 0), pipeline_mode=pl.Buffered(1))
    return pl.pallas_call(
        _post_ffn_kernel,
        grid=(x.shape[0] // rows,),
        in_specs=[row_spec, row_spec,
                  _const_spec((1, D_MODEL)), _const_spec((1, D_MODEL)), _const_spec((1, D_MODEL)),
                  layer_spec(wg), layer_spec(wu), layer_spec(wd)],
        out_specs=row_spec,
        out_shape=jax.ShapeDtypeStruct(x.shape, F32),
        compiler_params=pltpu.CompilerParams(
            dimension_semantics=("arbitrary",), vmem_limit_bytes=vmem_bytes),
        name=name,
    )(x, m, gpm, gpf, gof, wg, wu, wd)


CONV_HALO = 32
CONV_ROWS = 64


def _layer_norm_silu(c, lng_ref, lnb_ref):
    mu = jnp.mean(c, axis=-1, keepdims=True)
    cc = c - mu
    var = jnp.mean(cc * cc, axis=-1, keepdims=True)
    cn = cc * lax.rsqrt(var + LN_EPS) * lng_ref[...] + lnb_ref[...]
    return cn * jax.nn.sigmoid(cn)


def _conv_prompt_kernel(x_ref, g_ref, w1_ref, b1_ref, wdw_ref, bdw_ref, lng_ref, lnb_ref, w2_ref, b2_ref,
                        m_ref, tail_ref, gp_scr, c_scr, slab_scr, *, blk, nb):
    i = pl.program_id(0)
    halo = CONV_HALO * nb
    tail = CONV_TAIL * nb

    @pl.when(i == 0)
    def _():
        gp_scr[0:halo, :] = jnp.zeros((halo, D_MODEL), F32)

    x = _load_time_major(x_ref, slab_scr, nb)
    h = _rms(x, g_ref[...]).astype(BF16)
    a = _dot(h, w1_ref[...]) + b1_ref[...]
    gp_scr[halo:halo + blk, :] = a[:, :D_MODEL] * jax.nn.sigmoid(a[:, D_MODEL:])

    def chunk(ci, carry):
        base = pl.multiple_of(ci * CONV_ROWS, CONV_ROWS)
        for l in range(D_SLABS):
            lanes = slice(l * LANE, (l + 1) * LANE)
            xs = gp_scr[pl.ds(base + halo - tail, CONV_ROWS + tail), lanes]
            acc = jnp.broadcast_to(bdw_ref[:, lanes], (CONV_ROWS, LANE))
            for j in range(CONV_WIDTH):
                wj = wdw_ref[j * SUBLANE:(j + 1) * SUBLANE, lanes]
                wj = jnp.concatenate([wj] * (CONV_ROWS // SUBLANE), axis=0)
                acc = acc + wj * xs[j * nb:j * nb + CONV_ROWS]
            c_scr[pl.ds(base, CONV_ROWS), lanes] = acc
        return carry

    lax.fori_loop(0, blk // CONV_ROWS, chunk, 0)
    act = _layer_norm_silu(c_scr[...], lng_ref, lnb_ref).astype(BF16)
    _store_sequence_major(_dot(act, w2_ref[...]) + b2_ref[...], slab_scr, m_ref, nb)
    tail_ref[...] = gp_scr[halo + blk - tail:halo + blk, :]
    gp_scr[0:halo, :] = gp_scr[blk:blk + halo, :]


def _prompt_conv(x, g_pre, cw, *, steps):
    w1, b1, wdw8, bdw, lng, lnb, w2, b2 = cw
    nb, seq, _ = x.shape
    blk = steps * nb
    assert steps >= CONV_HALO and blk % CONV_ROWS == 0
    vec = lambda n: _const_spec((1, n))
    seq_spec = pl.BlockSpec((nb, steps, D_MODEL), lambda i: (0, i, 0))
    return pl.pallas_call(
        functools.partial(_conv_prompt_kernel, blk=blk, nb=nb),
        grid=(seq // steps,),
        in_specs=[
            seq_spec,
            vec(D_MODEL), _const_spec(w1.shape), vec(2 * D_MODEL),
            _const_spec(wdw8.shape), vec(D_MODEL), vec(D_MODEL), vec(D_MODEL),
            _const_spec(w2.shape), vec(D_MODEL),
        ],
        out_specs=[
            seq_spec,
            pl.BlockSpec((CONV_TAIL * nb, D_MODEL), lambda i: (0, 0)),
        ],
        out_shape=[
            jax.ShapeDtypeStruct((nb, seq, D_MODEL), BF16),
            jax.ShapeDtypeStruct((CONV_TAIL * nb, D_MODEL), F32),
        ],
        scratch_shapes=[pltpu.VMEM((blk + CONV_HALO * nb, D_MODEL), F32),
                        pltpu.VMEM((blk, D_MODEL), F32),
                        pltpu.VMEM((D_SLABS, blk, LANE), F32)],
        compiler_params=_params(1),
        name="prompt_conv",
    )(x, g_pre, w1, b1, wdw8, bdw, lng, lnb, w2, b2)


def _sample_l0_kernel(sink_ref, x_ref, g_ref, w_ref, ck_ref, cv_ref, sre_ref, sim_ref,
                      bb_ref, cc_ref, are_ref, aim_ref, d_ref, wglu_ref, wout_ref,
                      m_ref, ko_ref, vo_ref, ore_ref, oim_ref,
                      q_scr, k_scr, v_scr, a_scr, u_scr, y_scr, *, nseq, ntok):
    qw = N_Q_HEADS * LANE
    h = _rms(x_ref[...], g_ref[...]).astype(BF16)
    z = _dot(h, w_ref[...])
    q_scr[...] = z[:, :qw] * (HEAD_DIM ** -0.5)
    kn = z[:, qw:qw + KV_WIDTH]
    vn = z[:, qw + KV_WIDTH:qw + 2 * KV_WIDTH]
    k_scr[...] = kn
    v_scr[...] = vn
    for j in range(N_STRIP):
        u_scr[j] = z[:, qw + 2 * KV_WIDTH + j * LANE:qw + 2 * KV_WIDTH + (j + 1) * LANE]
    y_scr[...] = jnp.zeros(y_scr.shape, F32)

    for j in range(N_STRIP):
        hr = sre_ref[:, j * STRIP_STATES:(j + 1) * STRIP_STATES]
        hi = sim_ref[:, j * STRIP_STATES:(j + 1) * STRIP_STATES]
        are = are_ref[j:j + 1, :]
        aim = aim_ref[j:j + 1, :]
        dj = d_ref[:, j * LANE:(j + 1) * LANE]
        for t in range(ntok):
            ut = u_scr[j, pl.ds(t, nseq, stride=SAMPLE_T), :]
            bu = _dot(ut.astype(BF16), bb_ref[j])
            hr, hi = (are * hr - aim * hi + bu[:, :STRIP_STATES],
                      are * hi + aim * hr + bu[:, STRIP_STATES:])
            hcat = jnp.concatenate([hr, hi], axis=1).astype(BF16)
            y_scr[j, pl.ds(t, nseq, stride=SAMPLE_T), :] = _dot(hcat, cc_ref[j]) + dj * ut
        ore_ref[:, j * STRIP_STATES:(j + 1) * STRIP_STATES] = hr
        oim_ref[:, j * STRIP_STATES:(j + 1) * STRIP_STATES] = hi
    s = _gelu_glu(jnp.concatenate([y_scr[j] for j in range(N_STRIP)], axis=1), wglu_ref)

    nkeys = WINDOW + SAMPLE_T
    shape = (N_Q_HEADS * SAMPLE_T, nkeys)
    tq = jnp.bitwise_and(lax.broadcasted_iota(jnp.int32, shape, 0), SAMPLE_T - 1)
    col = lax.broadcasted_iota(jnp.int32, shape, 1)
    ninf = jnp.float32(-jnp.inf)
    bias = jnp.where(col < WINDOW, jnp.where(col > tq, 0.0, ninf),
                     jnp.where(col - WINDOW <= tq, 0.0, ninf))
    sink = jnp.concatenate(
        [jnp.full((SAMPLE_T, 1), sink_ref[hh], F32) for hh in range(N_Q_HEADS)], axis=0)
    left = lax.broadcasted_iota(jnp.int32, (SAMPLE_T, LANE), 1) < HEAD_DIM

    def seq_body(n, carry):
        r = pl.multiple_of(n * SAMPLE_T, SAMPLE_T)
        qs = jnp.concatenate(
            [q_scr[pl.ds(r, SAMPLE_T), hh * LANE:(hh + 1) * LANE] for hh in range(N_Q_HEADS)], axis=0)
        kall = jnp.concatenate([ck_ref[n], k_scr[pl.ds(r, SAMPLE_T), :]], axis=0)
        vall = jnp.concatenate([cv_ref[n], v_scr[pl.ds(r, SAMPLE_T), :]], axis=0)
        ko_ref[n] = kall[ntok:ntok + WINDOW]
        vo_ref[n] = vall[ntok:ntok + WINDOW]
        sc = _dot_nt(qs.astype(BF16), kall.astype(BF16)) + bias
        mx = jnp.maximum(jnp.max(sc, axis=1, keepdims=True), sink)
        p = jnp.exp(sc - mx)
        den = jnp.sum(p, axis=1, keepdims=True) + jnp.exp(sink - mx)
        p = (p / den).astype(BF16)
        o1 = _dot(p, vall.astype(BF16))
        o2 = _dot(p, pltpu.roll(vall, HEAD_DIM, axis=1).astype(BF16))
        slabs = []
        for hp in range(N_Q_HEADS // 2):
            lo, ro = (o1, o2) if hp < Q_PER_KV // 2 else (o2, o1)
            lrow = 2 * hp * SAMPLE_T
            slabs.append(jnp.where(left, lo[lrow:lrow + SAMPLE_T], ro[lrow + SAMPLE_T:lrow + 2 * SAMPLE_T]))
        a_scr[pl.ds(r, SAMPLE_T), :] = jnp.concatenate(slabs, axis=1)
        return carry

    lax.fori_loop(0, nseq, seq_body, 0, unroll=SAMPLE_ATTN_UNROLL)
    m = (_dot(a_scr[...].astype(BF16), wout_ref[0:ATTN_WIDTH, :])
         + _dot(s.astype(BF16), wout_ref[ATTN_WIDTH:, :]))
    m_ref[...] = m.astype(m_ref.dtype)


def _sample_l0(x, g_pre, w_in_s, sinks, ck, cv, s_re, s_im, tabs, d_skip, w_glu, w_out, *, nseq, ntok):
    a_re, a_im, bb, cc = tabs
    rows_all = x.shape[0]
    rows = nseq * SAMPLE_T
    nstate = SSM_GROUPS * SSM_STATE
    row_spec = lambda w: pl.BlockSpec((rows, w), lambda i: (i, 0))
    return pl.pallas_call(
        functools.partial(_sample_l0_kernel, nseq=nseq, ntok=ntok),
        grid=(rows_all // rows,),
        in_specs=[
            pl.BlockSpec(memory_space=pltpu.SMEM),
            row_spec(D_MODEL), _const_spec((1, D_MODEL)), _const_spec(w_in_s.shape),
            pl.BlockSpec((nseq, WINDOW, KV_WIDTH), lambda i: (i, 0, 0)),
            pl.BlockSpec((nseq, WINDOW, KV_WIDTH), lambda i: (i, 0, 0)),
            pl.BlockSpec((nseq, nstate), lambda i: (i, 0)),
            pl.BlockSpec((nseq, nstate), lambda i: (i, 0)),
            _const_spec(bb.shape), _const_spec(cc.shape),
            _const_spec(a_re.shape), _const_spec(a_im.shape),
            _const_spec((1, SSM_WIDTH)),
            _const_spec(w_glu.shape), _const_spec(w_out.shape),
        ],
        out_specs=[
            row_spec(D_MODEL),
            pl.BlockSpec((nseq, WINDOW, KV_WIDTH), lambda i: (i, 0, 0)),
            pl.BlockSpec((nseq, WINDOW, KV_WIDTH), lambda i: (i, 0, 0)),
            pl.BlockSpec((nseq, nstate), lambda i: (i, 0)),
            pl.BlockSpec((nseq, nstate), lambda i: (i, 0)),
        ],
        out_shape=[
            jax.ShapeDtypeStruct((rows_all, D_MODEL), BF16),
            jax.ShapeDtypeStruct(ck.shape, F32),
            jax.ShapeDtypeStruct(cv.shape, F32),
            jax.ShapeDtypeStruct((rows_all // SAMPLE_T, nstate), F32),
            jax.ShapeDtypeStruct((rows_all // SAMPLE_T, nstate), F32),
        ],
        scratch_shapes=[
            pltpu.VMEM((rows, N_Q_HEADS * LANE), F32),
            pltpu.VMEM((rows, KV_WIDTH), F32),
            pltpu.VMEM((rows, KV_WIDTH), F32),
            pltpu.VMEM((rows, ATTN_WIDTH), F32),
            pltpu.VMEM((N_STRIP, rows, LANE), F32),
            pltpu.VMEM((N_STRIP, rows, LANE), F32),
        ],
        compiler_params=_params(1),
        name="sample_l0",
    )(sinks, x, g_pre, w_in_s, ck, cv, s_re, s_im, bb, cc, a_re, a_im, d_skip, w_glu, w_out)


def _sample_l1_kernel(x_ref, g_ref, w1_ref, b1_ref, st_ref, wdw_ref, bdw_ref, lng_ref, lnb_ref,
                      w2_ref, b2_ref, m_ref, sto_ref, g_scr, c_scr, *, nseq, ntok):
    h = _rms(x_ref[...], g_ref[...]).astype(BF16)
    a = _dot(h, w1_ref[...]) + b1_ref[...]
    gg = a[:, :D_MODEL] * jax.nn.sigmoid(a[:, D_MODEL:])
    for j in range(D_SLABS):
        g_scr[j] = gg[:, j * LANE:(j + 1) * LANE]
    c_scr[...] = jnp.zeros(c_scr.shape, F32)
    keep = CONV_TAIL - ntok
    for j in range(D_SLABS):
        lanes = slice(j * LANE, (j + 1) * LANE)
        new = [g_scr[j, pl.ds(t, nseq, stride=SAMPLE_T), :] for t in range(ntok)]
        acc = [jnp.broadcast_to(bdw_ref[:, lanes], (nseq, LANE)) for _ in range(ntok)]
        for r in range(CONV_TAIL):
            row = st_ref[r, :, lanes]
            if r >= ntok:
                sto_ref[r - ntok, :, lanes] = row
            for t in range(min(r, ntok - 1) + 1):
                acc[t] = acc[t] + row * wdw_ref[r - t:r - t + 1, lanes]
        for t in range(ntok):
            sto_ref[keep + t, :, lanes] = new[t]
            for t2 in range(t + 1):
                tap = CONV_TAIL + t2 - t
                acc[t] = acc[t] + new[t2] * wdw_ref[tap:tap + 1, lanes]
            c_scr[j, pl.ds(t, nseq, stride=SAMPLE_T), :] = acc[t]
    c = jnp.concatenate([c_scr[j] for j in range(D_SLABS)], axis=1)
    act = _layer_norm_silu(c, lng_ref, lnb_ref).astype(BF16)
    m_ref[...] = (_dot(act, w2_ref[...]) + b2_ref[...]).astype(m_ref.dtype)


def _sample_l1(x, g_pre, cw, st, wdw, *, nseq, ntok):
    w1, b1, _, bdw, lng, lnb, w2, b2 = cw
    rows_all = x.shape[0]
    rows = nseq * SAMPLE_T
    vec = lambda n: _const_spec((1, n))
    row_spec = pl.BlockSpec((rows, D_MODEL), lambda i: (i, 0))
    st_spec = pl.BlockSpec((CONV_TAIL, nseq, D_MODEL), lambda i: (0, i, 0))
    return pl.pallas_call(
        functools.partial(_sample_l1_kernel, nseq=nseq, ntok=ntok),
        grid=(rows_all // rows,),
        in_specs=[
            row_spec, vec(D_MODEL), _const_spec(w1.shape), vec(2 * D_MODEL),
            st_spec, _const_spec(wdw.shape),
            vec(D_MODEL), vec(D_MODEL), vec(D_MODEL), _const_spec(w2.shape), vec(D_MODEL),
        ],
        out_specs=[row_spec, st_spec],
        out_shape=[
            jax.ShapeDtypeStruct((rows_all, D_MODEL), BF16),
            jax.ShapeDtypeStruct(st.shape, F32),
        ],
        scratch_shapes=[pltpu.VMEM((D_SLABS, rows, LANE), F32),
                        pltpu.VMEM((D_SLABS, rows, LANE), F32)],
        compiler_params=_params(1),
        name="sample_l1",
    )(x, g_pre, w1, b1, st, wdw, bdw, lng, lnb, w2, b2)


PROMPT_FFN_ROWS = 1024
SCAN_STEPS = 64
CONV_STEPS = 128
SAMPLE_SEQS = 32
SAMPLE_FFN_ROWS = 512
SAMPLE_FFN_VMEM_BYTES = 22 * 1024 * 1024
SAMPLE_ATTN_UNROLL = 8


def kernel(x_prompt, x_sample, cache_k, cache_v, state_s5_re, state_s5_im, state_conv, norm_pre_mix, norm_post_mix, norm_pre_ffn, norm_post_ffn, w_in_ab, attn_sinks, s5_lambda_re, s5_lambda_im, s5_log_step, s5_b_re, s5_b_im, s5_c_re, s5_c_im, s5_d, w_glu, w_out_ab, w_pw1, b_pw1, w_dw, b_dw, conv_ln_g, conv_ln_b, w_pw2, b_pw2, w_ffn_gate, w_ffn_up, w_ffn_down):
    nb, seq, _ = x_prompt.shape
    ns, ntok, _ = x_sample.shape
    assert nb == SUBLANE and ntok <= SAMPLE_T and seq % WINDOW == 0 and seq % SCAN_STEPS == 0 and seq % CONV_STEPS == 0
    row = lambda v: v.reshape(1, -1)
    gains = lambda layer: tuple(row(n[layer]) for n in (norm_post_mix, norm_pre_ffn, norm_post_ffn))

    w_in = w_in_ab[0].astype(BF16)
    wq = w_in_ab[0][:, :ATTN_WIDTH].reshape(D_MODEL, N_Q_HEADS, HEAD_DIM)
    lane_kv = (jnp.arange(LANE) // HEAD_DIM)[None, :]
    head_kv = (jnp.arange(N_Q_HEADS) // Q_PER_KV)[:, None]
    wq_slab = jnp.where((lane_kv == head_kv)[None], jnp.concatenate([wq, wq], axis=2), 0.0)
    w_in_s = jnp.concatenate(
        [wq_slab.reshape(D_MODEL, N_Q_HEADS * LANE), w_in_ab[0][:, ATTN_WIDTH:]], axis=1).astype(BF16)
    w_glu_b = w_glu[0].astype(BF16)
    w_out = w_out_ab[0].astype(BF16)
    ffn_w = (w_ffn_gate.astype(BF16), w_ffn_up.astype(BF16), w_ffn_down.astype(BF16))
    sinks = attn_sinks[0]
    tabs = _s5_tables(s5_lambda_re[0], s5_lambda_im[0], s5_log_step[0], s5_b_re[0], s5_b_im[0],
                      s5_c_re[0], s5_c_im[0])
    d_skip = row(s5_d)
    wdw = w_dw[0]
    conv_w = (w_pw1[0].astype(BF16), row(b_pw1), jnp.repeat(wdw, SUBLANE, axis=0), row(b_dw),
              row(conv_ln_g), row(conv_ln_b), w_pw2[0].astype(BF16), row(b_pw2))

    flat = lambda v: v.reshape(nb * seq, D_MODEL)
    au_p, kk_p, vk_p = _prompt_proj_attn(x_prompt, row(norm_pre_mix[0]), w_in, sinks)
    m0_p, hfin_p = _prompt_s5(au_p, tabs, d_skip, w_glu_b, w_out, nb=nb, steps=SCAN_STEPS)
    x1_p = _post_ffn(flat(x_prompt), flat(m0_p), gains(0), ffn_w, 0, rows=PROMPT_FFN_ROWS, name="prompt_ffn0")
    m1_p, tail_p = _prompt_conv(x1_p.reshape(nb, seq, D_MODEL), row(norm_pre_mix[1]), conv_w, steps=CONV_STEPS)
    y_prompt = _post_ffn(x1_p, flat(m1_p), gains(1), ffn_w, 1, rows=PROMPT_FFN_ROWS, name="prompt_ffn1")
    y_prompt = y_prompt.reshape(nb, seq, D_MODEL)

    pad_rows = lambda v: jnp.pad(
        v.reshape(ns, ntok, D_MODEL), ((0, 0), (0, SAMPLE_T - ntok), (0, 0))).reshape(ns * SAMPLE_T, D_MODEL)
    real_rows = lambda v: v.reshape(ns, SAMPLE_T, D_MODEL)[:, :ntok].reshape(ns * ntok, D_MODEL)
    xs_c = x_sample.reshape(ns * ntok, D_MODEL)
    xs = pad_rows(xs_c)
    nstate = SSM_GROUPS * SSM_STATE
    m0_s, ko_s, vo_s, sre_s, sim_s = _sample_l0(
        xs, row(norm_pre_mix[0]), w_in_s, sinks,
        cache_k.reshape(ns, WINDOW, KV_WIDTH), cache_v.reshape(ns, WINDOW, KV_WIDTH),
        state_s5_re.reshape(ns, nstate), state_s5_im.reshape(ns, nstate),
        tabs, d_skip, w_glu_b, w_out, nseq=SAMPLE_SEQS, ntok=ntok)
    x1_c = _post_ffn(xs_c, real_rows(m0_s), gains(0), ffn_w, 0, rows=SAMPLE_FFN_ROWS, vmem_bytes=SAMPLE_FFN_VMEM_BYTES, name="sample_ffn0")
    st_t = jnp.swapaxes(state_conv.reshape(ns, CONV_TAIL, D_MODEL), 0, 1)
    m1_s, sto_t = _sample_l1(
        pad_rows(x1_c), row(norm_pre_mix[1]), conv_w, st_t, wdw, nseq=SAMPLE_SEQS, ntok=ntok)
    y_c = _post_ffn(x1_c, real_rows(m1_s), gains(1), ffn_w, 1, rows=SAMPLE_FFN_ROWS, vmem_bytes=SAMPLE_FFN_VMEM_BYTES, name="sample_ffn1")

    y_sample = y_c.reshape(ns, ntok, D_MODEL)
    k_prompt = kk_p.reshape(1, nb, WINDOW, N_KV_HEADS, HEAD_DIM)
    v_prompt = vk_p.reshape(1, nb, WINDOW, N_KV_HEADS, HEAD_DIM)
    hfin = jnp.swapaxes(hfin_p, 0, 1)
    s5_re_prompt = hfin[:, :, :STRIP_STATES].reshape(1, nb, SSM_GROUPS, SSM_STATE)
    s5_im_prompt = hfin[:, :, STRIP_STATES:].reshape(1, nb, SSM_GROUPS, SSM_STATE)
    conv_prompt = jnp.swapaxes(tail_p.reshape(CONV_TAIL, nb, D_MODEL), 0, 1)[None]
    k_sample = ko_s.reshape(1, ns, WINDOW, N_KV_HEADS, HEAD_DIM)
    v_sample = vo_s.reshape(1, ns, WINDOW, N_KV_HEADS, HEAD_DIM)
    s5_re_sample = sre_s.reshape(1, ns, SSM_GROUPS, SSM_STATE)
    s5_im_sample = sim_s.reshape(1, ns, SSM_GROUPS, SSM_STATE)
    conv_sample = jnp.swapaxes(sto_t, 0, 1).reshape(1, ns, CONV_TAIL, D_MODEL)
    return (y_prompt, y_sample, k_prompt, v_prompt, s5_re_prompt, s5_im_prompt, conv_prompt,
            k_sample, v_sample, s5_re_sample, s5_im_sample, conv_sample)
```
